```python
import math
import jax, jax.numpy as jnp
from jax import lax
import numpy as np


D_MODEL = 1024
BATCH = 8
SEQ = 2048
DEPTH = 2

HEAD_DIM = 64
DIL_CONFIGS = ((128, 1), (512, 4), (2048, 16))
N_DIL_GROUPS = len(DIL_CONFIGS)
HEADS_PER_GROUP = 4
N_DIL_HEADS = N_DIL_GROUPS * HEADS_PER_GROUP
N_DIFF_HEADS = 4
DIFF_V_DIM = 2 * HEAD_DIM
N_ALIBI_HEADS = N_DIL_HEADS + N_DIFF_HEADS
D_FF = ((8 * D_MODEL // 3 + 127) // 128) * 128
BLOCK = 128
EPS = 1e-6

DIL_WIDTH = N_DIL_HEADS * HEAD_DIM
DIL_OUT = HEADS_PER_GROUP * HEAD_DIM
DIFF_QK_WIDTH = N_DIFF_HEADS * 2 * HEAD_DIM
DIFF_V_WIDTH = N_DIFF_HEADS * DIFF_V_DIM
IN_WIDTHS = (DIL_WIDTH, DIL_WIDTH, DIL_WIDTH, DIFF_QK_WIDTH, DIFF_QK_WIDTH, DIFF_V_WIDTH, D_MODEL, D_MODEL)
IN_COLS = sum(IN_WIDTHS)
SPLITS = tuple(int(c) for c in np.cumsum(IN_WIDTHS)[:-1])

kernel_name = 'hybrid_dilated_diff_attn_macaron'


def rms_norm(x, g):
    xf = x.astype(jnp.float32)
    y = xf * lax.rsqrt(jnp.mean(xf * xf, axis=-1, keepdims=True) + EPS)
    return (y * g.astype(jnp.float32)).astype(x.dtype)


def alibi_slopes(n):
    return jnp.exp2(-8.0 * jnp.arange(1, n + 1, dtype=jnp.float32) / n)


def swiglu_ffn(x, norm_g, w_in, w_out):
    h = rms_norm(x, norm_g)
    gate, up = jnp.split(h @ w_in, 2, axis=-1)
    return (jax.nn.silu(gate) * up) @ w_out


def dilated_window_attention(q, k, v, slopes, window, dilation):
    b, s, h, d = q.shape
    steps = window // dilation
    assert steps <= BLOCK
    span = dilation * BLOCK
    s_pad = -(-s // span) * span
    n_sub = s_pad // dilation
    n_blk = n_sub // BLOCK

    def to_blocks(t):
        t = jnp.pad(t, ((0, 0), (0, s_pad - s), (0, 0), (0, 0)))
        t = t.reshape(b, n_sub, dilation, h, d).transpose(0, 2, 1, 3, 4)
        return t.reshape(b, dilation, n_blk, BLOCK, h, d)

    def with_prev(t):
        prev = jnp.pad(t, ((0, 0), (0, 0), (1, 0), (0, 0), (0, 0), (0, 0)))[:, :, :-1]
        return jnp.concatenate([prev, t], axis=3)

    qb = to_blocks(q)
    kw = with_prev(to_blocks(k))
    vw = with_prev(to_blocks(v))
    logits = jnp.einsum('brnqhd,brnkhd->brnhqk', qb, kw).astype(jnp.float32)
    qi = jnp.arange(BLOCK)[:, None]
    ki = jnp.arange(2 * BLOCK)[None, :]
    dist = BLOCK + qi - ki
    key_sub = (jnp.arange(n_blk)[:, None, None] - 1) * BLOCK + ki[None]
    valid = (dist >= 0)[None] & (dist <= steps)[None] & (key_sub >= 0)
    bias = -slopes[:, None, None] * (dilation * dist).astype(jnp.float32)
    logits = jnp.where(valid[None, None, :, None], logits + bias, -jnp.inf)
    m = jnp.max(logits, axis=-1, keepdims=True)
    p = jnp.exp(logits - m)
    denom = jnp.sum(p, axis=-1, keepdims=True)
    out = jnp.einsum('brnhqk,brnkhd->brnqhd', p, vw.astype(jnp.float32))
    out = out / jnp.swapaxes(denom, 3, 4)
    lse = jnp.swapaxes((m + jnp.log(denom))[..., 0], 3, 4)
    out = out.reshape(b, dilation, n_sub, h, d).transpose(0, 2, 1, 3, 4).reshape(b, s_pad, h, d)[:, :s]
    lse = lse.reshape(b, dilation, n_sub, h).transpose(0, 2, 1, 3).reshape(b, s_pad, h)[:, :s]
    return out, lse


def differential_attention(q, k, v, slopes, lam):
    b, s, h, _, d = q.shape
    n_blk = s // BLOCK
    qb = q.reshape(b, n_blk, BLOCK, h, 2, d).transpose(1, 0, 2, 3, 4, 5)
    kpos = jnp.arange(s)
    vf = v.astype(jnp.float32)

    def one_block(args):
        q_blk, i = args
        qpos = i * BLOCK + jnp.arange(BLOCK)
        logits = jnp.einsum('bqhmd,bkhmd->bhmqk', q_blk, k).astype(jnp.float32)
        dist = qpos[:, None] - kpos[None, :]
        logits = logits - slopes[:, None, None, None] * dist.astype(jnp.float32)
        logits = jnp.where(dist >= 0, logits, -jnp.inf)
        p = jax.nn.softmax(logits, axis=-1)
        a = p[:, :, 0] - lam * p[:, :, 1]
        return jnp.einsum('bhqk,bkhe->bqhe', a, vf)

    out = lax.map(one_block, (qb, jnp.arange(n_blk)))
    return out.transpose(1, 0, 2, 3, 4).reshape(b, s, h, v.shape[-1])


def mixer_block(x, layer_idx, slopes, mix_norm, w_in, qk_gain_dil, qk_gain_diff, lambda_q, lambda_k,
                diff_subnorm, w_branch_dil, w_branch_diff, w_out):
    b, s, _ = x.shape
    h = rms_norm(x, mix_norm)
    proj = h @ w_in
    qa, ka, va, qd, kd, vd, gate_a, gate_b = jnp.split(proj, SPLITS, axis=-1)

    qa = rms_norm(qa.reshape(b, s, N_DIL_GROUPS, HEADS_PER_GROUP, HEAD_DIM), qk_gain_dil[0]) * HEAD_DIM ** -0.5
    ka = rms_norm(ka.reshape(b, s, N_DIL_GROUPS, HEADS_PER_GROUP, HEAD_DIM), qk_gain_dil[1])
    va = va.reshape(b, s, N_DIL_GROUPS, HEADS_PER_GROUP, HEAD_DIM)
    outs, lses = [], []
    for g, (window, dilation) in enumerate(DIL_CONFIGS):
        o_g, lse_g = dilated_window_attention(qa[:, :, g], ka[:, :, g], va[:, :, g],
                                              slopes[g * HEADS_PER_GROUP:(g + 1) * HEADS_PER_GROUP],
                                              window, dilation)
        outs.append(o_g)
        lses.append(lse_g)
    o_stack = jnp.stack(outs, axis=2)
    alpha = jax.nn.softmax(jnp.stack(lses, axis=2), axis=2)
    o_dil = jnp.sum(alpha[..., None] * o_stack, axis=2).reshape(b, s, DIL_OUT).astype(x.dtype)

    qd = rms_norm(qd.reshape(b, s, N_DIFF_HEADS, 2, HEAD_DIM), qk_gain_diff[0]) * HEAD_DIM ** -0.5
    kd = rms_norm(kd.reshape(b, s, N_DIFF_HEADS, 2, HEAD_DIM), qk_gain_diff[1])
    vd = vd.reshape(b, s, N_DIFF_HEADS, DIFF_V_DIM)
    lam_init = 0.8 - 0.6 * math.exp(-0.3 * layer_idx)
    lq = lambda_q.astype(jnp.float32)
    lk = lambda_k.astype(jnp.float32)
    lam = jnp.exp(jnp.sum(lq[0] * lk[0])) - jnp.exp(jnp.sum(lq[1] * lk[1])) + lam_init
    o_diff = differential_attention(qd, kd, vd, slopes[N_DIL_HEADS:], lam)
    o_diff = (rms_norm(o_diff, diff_subnorm) * (1.0 - lam_init)).reshape(b, s, DIFF_V_WIDTH).astype(x.dtype)

    y = jax.nn.sigmoid(gate_a) * (o_dil @ w_branch_dil) + jax.nn.sigmoid(gate_b) * (o_diff @ w_branch_diff)
    return y @ w_out


def setup_inputs(seed: int = 0) -> dict:
    key = jax.random.key(seed)
    ks = jax.random.split(key, 17)
    L, D, F = DEPTH, D_MODEL, D_FF
    f32 = jnp.float32

    def w(k, shape, fan_in):
        return jax.random.normal(k, shape, f32) * fan_in ** -0.5

    def gain(k, shape):
        return 1.0 + 0.02 * jax.random.normal(k, shape, f32)

    return {
        'x': jax.random.normal(ks[0], (BATCH, SEQ, D), f32),
        'ffn1_norm': gain(ks[1], (L, D)),
        'ffn1_w_in': w(ks[2], (L, D, 2 * F), D),
        'ffn1_w_out': w(ks[3], (L, F, D), F),
        'mix_norm': gain(ks[4], (L, D)),
        'w_in': w(ks[5], (L, D, IN_COLS), D),
        'qk_gain_dil': gain(ks[6], (L, 2, N_DIL_GROUPS, HEADS_PER_GROUP, HEAD_DIM)),
        'qk_gain_diff': gain(ks[7], (L, 2, N_DIFF_HEADS, 2, HEAD_DIM)),
        'lambda_q': 0.1 * jax.random.normal(ks[8], (L, 2, HEAD_DIM), f32),
        'lambda_k': 0.1 * jax.random.normal(ks[9], (L, 2, HEAD_DIM), f32),
        'diff_subnorm': gain(ks[10], (L, N_DIFF_HEADS, DIFF_V_DIM)),
        'w_branch_dil': w(ks[11], (L, DIL_OUT, D), DIL_OUT),
        'w_branch_diff': w(ks[12], (L, DIFF_V_WIDTH, D), DIFF_V_WIDTH),
        'w_out': w(ks[13], (L, D, D), D),
        'ffn2_norm': gain(ks[14], (L, D)),
        'ffn2_w_in': w(ks[15], (L, D, 2 * F), D),
        'ffn2_w_out': w(ks[16], (L, F, D), F),
    }


def reference(x, ffn1_norm, ffn1_w_in, ffn1_w_out, mix_norm, w_in, qk_gain_dil, qk_gain_diff,
              lambda_q, lambda_k, diff_subnorm, w_branch_dil, w_branch_diff, w_out,
              ffn2_norm, ffn2_w_in, ffn2_w_out):
    slopes = alibi_slopes(N_ALIBI_HEADS)
    for l in range(DEPTH):
        x = x + 0.5 * swiglu_ffn(x, ffn1_norm[l], ffn1_w_in[l], ffn1_w_out[l])
        x = x + mixer_block(x, l, slopes, mix_norm[l], w_in[l], qk_gain_dil[l], qk_gain_diff[l],
                            lambda_q[l], lambda_k[l], diff_subnorm[l], w_branch_dil[l],
                            w_branch_diff[l], w_out[l])
        x = x + 0.5 * swiglu_ffn(x, ffn2_norm[l], ffn2_w_in[l], ffn2_w_out[l])
    return x
```

```python
import functools
import math

import jax
import jax.numpy as jnp
import numpy as np
from jax import lax
from jax.experimental import pallas as pl
from jax.experimental.pallas import tpu as pltpu

F32 = jnp.float32
BF16 = jnp.bfloat16

D_MODEL = 1024
BATCH = 8
SEQ = 2048
DEPTH = 2
HEAD_DIM = 64
DIL_CONFIGS = ((128, 1), (512, 4), (2048, 16))
N_DIL_GROUPS = 3
HEADS_PER_GROUP = 4
N_DIL_HEADS = 12
N_DIFF_HEADS = 4
DIFF_V_DIM = 128
N_ALIBI_HEADS = 16
D_FF = 2816
BLOCK = 128
EPS = 1e-6
DIL_WIDTH = 768
DIL_OUT = 256
DIFF_QK_WIDTH = 512
DIFF_V_WIDTH = 512
IN_COLS = 5888
NEG = -1e30

LANES = 128
VMEM_LIMIT_BYTES = 56 * 1024 * 1024

M_ROWS = BATCH * SEQ
SEG = 256
N_SEG = IN_COLS // SEG
FFN_TM = 512
FFN_FC = 256
PROJ_TM = 512
MERGE_TM = 512
DIFF_TQ = 256
DIFF_TK = 256

SLOPES = tuple(float(np.float32(2.0) ** np.float32(-8.0 * i / N_ALIBI_HEADS)) for i in range(1, N_ALIBI_HEADS + 1))


def _rms_rows(x, gain):
    ms = jnp.mean(x * x, axis=-1, keepdims=True)
    return x * lax.rsqrt(ms + EPS) * gain


def _ffn_kernel(x_ref, g_ref, win_ref, wout_ref, o_ref, a_scr):
    x = x_ref[...]
    h = _rms_rows(x, g_ref[...]).astype(BF16)
    for c in range(D_FF // FFN_FC):
        lo = c * FFN_FC
        gate = jnp.dot(h, win_ref[:, lo:lo + FFN_FC], preferred_element_type=F32)
        up = jnp.dot(h, win_ref[:, D_FF + lo:D_FF + lo + FFN_FC], preferred_element_type=F32)
        a_scr[:, lo:lo + FFN_FC] = (gate * jax.nn.sigmoid(gate) * up).astype(BF16)
    y = jnp.dot(a_scr[...], wout_ref[...], preferred_element_type=F32)
    o_ref[...] = x + 0.5 * y


def _ffn(x, gain, w_in, w_out):
    tm = FFN_TM
    const = lambda i: (0, 0)
    return pl.pallas_call(
        _ffn_kernel,
        out_shape=jax.ShapeDtypeStruct((M_ROWS, D_MODEL), F32),
        grid=(M_ROWS // tm,),
        in_specs=[
            pl.BlockSpec((tm, D_MODEL), lambda i: (i, 0)),
            pl.BlockSpec((1, D_MODEL), const),
            pl.BlockSpec((D_MODEL, 2 * D_FF), const, pipeline_mode=pl.Buffered(1)),
            pl.BlockSpec((D_FF, D_MODEL), const, pipeline_mode=pl.Buffered(1)),
        ],
        out_specs=pl.BlockSpec((tm, D_MODEL), lambda i: (i, 0)),
        scratch_shapes=[pltpu.VMEM((tm, D_FF), BF16)],
        compiler_params=pltpu.CompilerParams(
            dimension_semantics=("arbitrary",), vmem_limit_bytes=VMEM_LIMIT_BYTES),
        name="ffn",
    )(x, gain, w_in, w_out)


def _proj_kernel(x_ref, g_ref, w_ref, gain_ref, bd_ref,
                 o_g0, o_g1, o_g2, o_qd, o_kd, o_vd, o_gate, scr1, scr2):
    tm = PROJ_TM
    x = x_ref[...]
    h = _rms_rows(x, g_ref[...]).astype(BF16)
    bd = bd_ref[...]

    def seg(i):
        return jnp.dot(h, w_ref[:, i * SEG:(i + 1) * SEG], preferred_element_type=F32)

    def qk_norm(y, gi):
        sq = y * y
        hi = sq.astype(BF16)
        lo = (sq - hi.astype(F32)).astype(BF16)
        ms = (jnp.dot(hi, bd, preferred_element_type=F32) + jnp.dot(lo, bd, preferred_element_type=F32))
        return y * lax.rsqrt(ms + EPS) * gain_ref[gi:gi + 1, :]

    o_g0[:, 0:SEG] = qk_norm(seg(0), 0).astype(BF16)
    o_g0[:, SEG:2 * SEG] = qk_norm(seg(1), 1).astype(BF16)
    o_g0[:, 2 * SEG:3 * SEG] = seg(2).astype(BF16)
    for g, (scr, o_g) in ((1, (scr1, o_g1)), (2, (scr2, o_g2))):
        d = DIL_CONFIGS[g][1]
        ys = (qk_norm(seg(3 * g), 2 * g), qk_norm(seg(3 * g + 1), 2 * g + 1), seg(3 * g + 2))
        for t, y in enumerate(ys):
            for u in range(SEG // LANES):
                scr[2 * t + u] = y[:, u * LANES:(u + 1) * LANES]
        for c in range(d):
            for t in range(DIL_WIDTH // LANES):
                o_g[0, c, :, t * LANES:(t + 1) * LANES] = scr[t, pl.ds(c, tm // d, stride=d), :].astype(BF16)
    for t in range(2):
        o_qd[:, t * SEG:(t + 1) * SEG] = qk_norm(seg(9 + t), 6 + t).astype(BF16)
        o_kd[:, t * SEG:(t + 1) * SEG] = qk_norm(seg(11 + t), 8 + t).astype(BF16)
        o_vd[:, t * SEG:(t + 1) * SEG] = seg(13 + t).astype(BF16)
    for t in range(8):
        o_gate[:, t * SEG:(t + 1) * SEG] = jax.nn.sigmoid(seg(15 + t)).astype(BF16)


def _proj(x, gain, w, qk_gain, bd):
    tm = PROJ_TM
    tiles_per_seq = SEQ // tm
    const = lambda i: (0, 0)
    row = lambda i: (i, 0)
    perm = lambda i: (i // tiles_per_seq, 0, i % tiles_per_seq, 0)
    out_shape = (
        jax.ShapeDtypeStruct((M_ROWS, DIL_WIDTH), BF16),
        jax.ShapeDtypeStruct((BATCH, 4, SEQ // 4, DIL_WIDTH), BF16),
        jax.ShapeDtypeStruct((BATCH, 16, SEQ // 16, DIL_WIDTH), BF16),
        jax.ShapeDtypeStruct((M_ROWS, DIFF_QK_WIDTH), BF16),
        jax.ShapeDtypeStruct((M_ROWS, DIFF_QK_WIDTH), BF16),
        jax.ShapeDtypeStruct((M_ROWS, DIFF_V_WIDTH), BF16),
        jax.ShapeDtypeStruct((M_ROWS, 2 * D_MODEL), BF16),
    )
    out_specs = (
        pl.BlockSpec((tm, DIL_WIDTH), row),
        pl.BlockSpec((1, 4, tm // 4, DIL_WIDTH), perm),
        pl.BlockSpec((1, 16, tm // 16, DIL_WIDTH), perm),
        pl.BlockSpec((tm, DIFF_QK_WIDTH), row),
        pl.BlockSpec((tm, DIFF_QK_WIDTH), row),
        pl.BlockSpec((tm, DIFF_V_WIDTH), row),
        pl.BlockSpec((tm, 2 * D_MODEL), row),
    )
    return pl.pallas_call(
        _proj_kernel,
        out_shape=out_shape,
        grid=(M_ROWS // tm,),
        in_specs=[
            pl.BlockSpec((tm, D_MODEL), row),
            pl.BlockSpec((1, D_MODEL), const),
            pl.BlockSpec((D_MODEL, IN_COLS), const, pipeline_mode=pl.Buffered(1)),
            pl.BlockSpec((16, SEG), const),
            pl.BlockSpec((SEG, SEG), const),
        ],
        out_specs=out_specs,
        scratch_shapes=[pltpu.VMEM((DIL_WIDTH // LANES, tm, LANES), F32),
                        pltpu.VMEM((DIL_WIDTH // LANES, tm, LANES), F32)],
        compiler_params=pltpu.CompilerParams(
            dimension_semantics=("arbitrary",), vmem_limit_bytes=VMEM_LIMIT_BYTES),
        name="proj",
    )(x, gain, w, qk_gain, bd)


def _dil_block(ref, row_cur, row_prev, bias, lane_head):
    q = ref[0, pl.ds(row_cur, BLOCK), 0:SEG]
    k = ref[0, pl.ds(row_cur, BLOCK), SEG:2 * SEG]
    v = ref[0, pl.ds(row_cur, BLOCK), 2 * SEG:3 * SEG]
    if row_prev is not None:
        k = jnp.concatenate([ref[0, pl.ds(row_prev, BLOCK), SEG:2 * SEG], k], axis=0)
        v = jnp.concatenate([ref[0, pl.ds(row_prev, BLOCK), 2 * SEG:3 * SEG], v], axis=0)
    zero = jnp.zeros_like(q)
    qs = jnp.concatenate([jnp.where(lane_head == j, q, zero) for j in range(HEADS_PER_GROUP)], axis=0)
    s = lax.dot_general(qs, k, (((1,), (1,)), ((), ())), preferred_element_type=F32) + bias
    m = jnp.max(s, axis=-1, keepdims=True)
    p = jnp.exp(s - m)
    l = jnp.sum(p, axis=-1, keepdims=True)
    pv = jnp.dot(p.astype(BF16), v, preferred_element_type=F32)
    o_rows = pv * (1.0 / l)
    lse_rows = m + jnp.log(l)
    out = jnp.zeros((BLOCK, SEG), F32)
    lse = jnp.zeros((BLOCK, SEG), F32)
    for j in range(HEADS_PER_GROUP):
        sel = lane_head == j
        out = jnp.where(sel, o_rows[j * BLOCK:(j + 1) * BLOCK], out)
        lse = jnp.where(sel, jnp.broadcast_to(lse_rows[j * BLOCK:(j + 1) * BLOCK], (BLOCK, SEG)), lse)
    return out, lse


def _dil_kernel(g0_ref, g1_ref, g2_ref, bias_ref, o_ref, o_scr, l_scr):
    j = pl.program_id(1)
    lane_head = lax.broadcasted_iota(jnp.int32, (1, SEG), 1) // HEAD_DIM
    first = (j == 0).astype(jnp.int32)

    row = pl.multiple_of(j * BLOCK, BLOCK)
    prev = pl.multiple_of(jnp.maximum(j - 1, 0) * BLOCK, BLOCK)
    def put(g, rows, out, lse):
        for u in range(SEG // LANES):
            o_scr[g, u, rows, :] = out[:, u * LANES:(u + 1) * LANES]
            l_scr[g, u, rows, :] = lse[:, u * LANES:(u + 1) * LANES]

    out, lse = _dil_block(g0_ref, row, prev, bias_ref[0, first], lane_head)
    put(0, pl.ds(row, BLOCK), out, lse)

    c = j // 4
    n = j % 4
    first1 = (n == 0).astype(jnp.int32)
    prev = pl.multiple_of((c * 4 + jnp.maximum(n - 1, 0)) * BLOCK, BLOCK)
    out, lse = _dil_block(g1_ref, row, prev, bias_ref[1, first1], lane_head)
    put(1, pl.ds(n * (4 * BLOCK) + c, BLOCK, stride=4), out, lse)

    out, lse = _dil_block(g2_ref, row, None, bias_ref[2, 1, :, BLOCK:2 * BLOCK], lane_head)
    put(2, pl.ds(j, BLOCK, stride=16), out, lse)

    @pl.when(j == pl.num_programs(1) - 1)
    def _():
        rows = 256
        for r in range(SEQ // rows):
            sl = pl.ds(r * rows, rows)
            for u in range(SEG // LANES):
                l0, l1, l2 = l_scr[0, u, sl, :], l_scr[1, u, sl, :], l_scr[2, u, sl, :]
                mx = jnp.maximum(jnp.maximum(l0, l1), l2)
                w0, w1, w2 = jnp.exp(l0 - mx), jnp.exp(l1 - mx), jnp.exp(l2 - mx)
                num = w0 * o_scr[0, u, sl, :] + w1 * o_scr[1, u, sl, :] + w2 * o_scr[2, u, sl, :]
                o_ref[0, sl, u * LANES:(u + 1) * LANES] = (num / (w0 + w1 + w2)).astype(BF16)


def _dil_bias_table():
    qi = np.arange(BLOCK)[:, None]
    ki = np.arange(2 * BLOCK)[None, :]
    dist = BLOCK + qi - ki
    valid = (dist >= 0) & (dist <= BLOCK)
    tab = np.full((N_DIL_GROUPS, 2, HEADS_PER_GROUP * BLOCK, 2 * BLOCK), NEG, np.float32)
    for g, (_, dil) in enumerate(DIL_CONFIGS):
        for h in range(HEADS_PER_GROUP):
            slope = np.float32(SLOPES[g * HEADS_PER_GROUP + h])
            b = np.where(valid, -slope * (dil * dist).astype(np.float32), np.float32(NEG)).astype(np.float32)
            tab[g, 0, h * BLOCK:(h + 1) * BLOCK] = b
            tab[g, 1, h * BLOCK:(h + 1) * BLOCK] = np.where(ki >= BLOCK, b, np.float32(NEG))
    return tab


def _dil_attn(g0, g1, g2, bias):
    n_blk = SEQ // BLOCK
    seq_spec = pl.BlockSpec((1, SEQ, DIL_WIDTH), lambda b, j: (b, 0, 0))
    return pl.pallas_call(
        _dil_kernel,
        out_shape=jax.ShapeDtypeStruct((BATCH, SEQ, DIL_OUT), BF16),
        grid=(BATCH, n_blk),
        in_specs=[seq_spec, seq_spec, seq_spec,
                  pl.BlockSpec(bias.shape, lambda b, j: (0, 0, 0, 0))],
        out_specs=pl.BlockSpec((1, SEQ, DIL_OUT), lambda b, j: (b, 0, 0)),
        scratch_shapes=[pltpu.VMEM((N_DIL_GROUPS, DIL_OUT // LANES, SEQ, LANES), F32),
                        pltpu.VMEM((N_DIL_GROUPS, DIL_OUT // LANES, SEQ, LANES), F32)],
        compiler_params=pltpu.CompilerParams(
            dimension_semantics=("arbitrary", "arbitrary"), vmem_limit_bytes=VMEM_LIMIT_BYTES),
        name="dil_attn",
    )(g0, g1, g2, bias)


def _diff_kernel(q_ref, k_ref, v_ref, lq_ref, lk_ref, sn_ref, o_ref, m_scr, l_scr, acc_scr, *, lam_init):
    tq, tk = DIFF_TQ, DIFF_TK
    pair = pl.program_id(1)
    qi = pl.program_id(2)
    q = q_ref[0]
    lane_map = lax.broadcasted_iota(jnp.int32, (1, SEG), 1) // HEAD_DIM
    zero = jnp.zeros_like(q)
    qs = jnp.concatenate([jnp.where(lane_map == g, q, zero) for g in range(4)], axis=0)
    slope0 = jnp.where(pair == 0, SLOPES[N_DIL_HEADS], SLOPES[N_DIL_HEADS + 2]).astype(F32)
    slope1 = jnp.where(pair == 0, SLOPES[N_DIL_HEADS + 1], SLOPES[N_DIL_HEADS + 3]).astype(F32)
    lane_k = lax.broadcasted_iota(jnp.int32, (1, tk), 1)

    m_scr[...] = jnp.full(m_scr.shape, NEG, F32)
    l_scr[...] = jnp.zeros(l_scr.shape, F32)
    acc_scr[...] = jnp.zeros(acc_scr.shape, F32)

    def step(kb, masked):
        k0 = pl.multiple_of(kb * tk, tk)
        k = k_ref[0, pl.ds(k0, tk), :]
        v = v_ref[0, pl.ds(k0, tk), :]
        s = lax.dot_general(qs, k, (((1,), (1,)), ((), ())), preferred_element_type=F32)
        kpos = (k0 + lane_k).astype(F32)
        s = jnp.concatenate([s[:2 * tq] + slope0 * kpos, s[2 * tq:] + slope1 * kpos], axis=0)
        if masked:
            r = lax.broadcasted_iota(jnp.int32, (4 * tq, tk), 0) & (tq - 1)
            cidx = lax.broadcasted_iota(jnp.int32, (4 * tq, tk), 1)
            s = jnp.where(cidx > r, NEG, s)
        m_prev = m_scr[...]
        m_new = jnp.maximum(m_prev, jnp.max(s, axis=-1, keepdims=True))
        alpha = jnp.exp(m_prev - m_new)
        p = jnp.exp(s - m_new)
        l_scr[...] = alpha * l_scr[...] + jnp.sum(p, axis=-1, keepdims=True)
        acc_scr[...] = alpha * acc_scr[...] + jnp.dot(p.astype(BF16), v, preferred_element_type=F32)
        m_scr[...] = m_new

    def body(kb, carry):
        step(kb, False)
        return carry

    lax.fori_loop(0, qi, body, 0)
    step(qi, True)

    lq = lq_ref[...]
    lk = lk_ref[...]
    lam = (jnp.exp(jnp.sum(lq[0:1] * lk[0:1], axis=-1, keepdims=True))
           - jnp.exp(jnp.sum(lq[1:2] * lk[1:2], axis=-1, keepdims=True)) + lam_init)
    inv = 1.0 / l_scr[...]
    outs = []
    for hh in range(2):
        cols = slice(hh * DIFF_V_DIM, (hh + 1) * DIFF_V_DIM)
        r1 = slice((2 * hh) * tq, (2 * hh + 1) * tq)
        r2 = slice((2 * hh + 1) * tq, (2 * hh + 2) * tq)
        o = acc_scr[r1, cols] * inv[r1] - lam * (acc_scr[r2, cols] * inv[r2])
        outs.append(_rms_rows(o, sn_ref[0, hh:hh + 1, :]) * (1.0 - lam_init))
    o_ref[0] = jnp.concatenate(outs, axis=-1).astype(BF16)


def _diff_attn(qd, kd, vd, lq, lk, subnorm, lam_init):
    tq = DIFF_TQ
    kv_spec = pl.BlockSpec((1, SEQ, SEG), lambda b, p, i: (b, 0, p))
    return pl.pallas_call(
        functools.partial(_diff_kernel, lam_init=lam_init),
        out_shape=jax.ShapeDtypeStruct((BATCH, SEQ, DIFF_V_WIDTH), BF16),
        grid=(BATCH, 2, SEQ // tq),
        in_specs=[
            pl.BlockSpec((1, tq, SEG), lambda b, p, i: (b, i, p)),
            kv_spec, kv_spec,
            pl.BlockSpec((2, HEAD_DIM), lambda b, p, i: (0, 0)),
            pl.BlockSpec((2, HEAD_DIM), lambda b, p, i: (0, 0)),
            pl.BlockSpec((1, 2, DIFF_V_DIM), lambda b, p, i: (p, 0, 0)),
        ],
        out_specs=pl.BlockSpec((1, tq, SEG), lambda b, p, i: (b, i, p)),
        scratch_shapes=[pltpu.VMEM((4 * tq, 1), F32), pltpu.VMEM((4 * tq, 1), F32),
                        pltpu.VMEM((4 * tq, SEG), F32)],
        compiler_params=pltpu.CompilerParams(
            dimension_semantics=("arbitrary", "arbitrary", "arbitrary"), vmem_limit_bytes=VMEM_LIMIT_BYTES),
        name="diff_attn",
    )(qd, kd, vd, lq, lk, subnorm)


def _merge_kernel(x_ref, od_ref, of_ref, gate_ref, wa_ref, wb_ref, wo_ref, o_ref):
    ya = jnp.dot(od_ref[...], wa_ref[...], preferred_element_type=F32)
    yb = jnp.dot(of_ref[...], wb_ref[...], preferred_element_type=F32)
    y = gate_ref[:, 0:D_MODEL].astype(F32) * ya + gate_ref[:, D_MODEL:2 * D_MODEL].astype(F32) * yb
    o_ref[...] = x_ref[...] + jnp.dot(y.astype(BF16), wo_ref[...], preferred_element_type=F32)


def _merge(x, o_dil, o_diff, gates, wa, wb, wo):
    tm = MERGE_TM
    const = lambda i: (0, 0)
    row = lambda i: (i, 0)
    return pl.pallas_call(
        _merge_kernel,
        out_shape=jax.ShapeDtypeStruct((M_ROWS, D_MODEL), F32),
        grid=(M_ROWS // tm,),
        in_specs=[
            pl.BlockSpec((tm, D_MODEL), row),
            pl.BlockSpec((tm, DIL_OUT), row),
            pl.BlockSpec((tm, DIFF_V_WIDTH), row),
            pl.BlockSpec((tm, 2 * D_MODEL), row),
            pl.BlockSpec((DIL_OUT, D_MODEL), const),
            pl.BlockSpec((DIFF_V_WIDTH, D_MODEL), const),
            pl.BlockSpec((D_MODEL, D_MODEL), const),
        ],
        out_specs=pl.BlockSpec((tm, D_MODEL), row),
        compiler_params=pltpu.CompilerParams(
            dimension_semantics=("arbitrary",), vmem_limit_bytes=VMEM_LIMIT_BYTES),
        name="merge",
    )(x, o_dil, o_diff, gates, wa, wb, wo)


def _permute_w_in(w):
    parts = []
    for g in range(N_DIL_GROUPS):
        for t in range(3):
            lo = t * DIL_WIDTH + g * SEG
            parts.append(w[:, lo:lo + SEG])
    parts.append(w[:, 3 * DIL_WIDTH:])
    return jnp.concatenate(parts, axis=1).astype(BF16)


def _qk_gain_rows(qk_gain_dil, qk_gain_diff):
    scale = HEAD_DIM ** -0.5
    rows = []
    for g in range(N_DIL_GROUPS):
        rows.append(qk_gain_dil[0, g].reshape(SEG) * scale)
        rows.append(qk_gain_dil[1, g].reshape(SEG))
    qd = qk_gain_diff[0].reshape(2, SEG) * scale
    kd = qk_gain_diff[1].reshape(2, SEG)
    rows += [qd[0], qd[1], kd[0], kd[1]]
    rows += [jnp.zeros((SEG,), F32)] * (16 - len(rows))
    return jnp.stack(rows).astype(F32)


def kernel(x, ffn1_norm, ffn1_w_in, ffn1_w_out, mix_norm, w_in, qk_gain_dil, qk_gain_diff, lambda_q, lambda_k,
           diff_subnorm, w_branch_dil, w_branch_diff, w_out, ffn2_norm, ffn2_w_in, ffn2_w_out):
    b, s, d = x.shape
    assert (b, s, d) == (BATCH, SEQ, D_MODEL)
    xr = x.reshape(M_ROWS, D_MODEL)
    bd = jnp.asarray(np.kron(np.eye(SEG // HEAD_DIM), np.full((HEAD_DIM, HEAD_DIM), 1.0 / HEAD_DIM)), BF16)
    dil_bias = jnp.asarray(_dil_bias_table())
    for l in range(DEPTH):
        lam_init = 0.8 - 0.6 * math.exp(-0.3 * l)
        xr = _ffn(xr, ffn1_norm[l].reshape(1, D_MODEL), ffn1_w_in[l].astype(BF16), ffn1_w_out[l].astype(BF16))
        g0, g1, g2, qd, kd, vd, gates = _proj(
            xr, mix_norm[l].reshape(1, D_MODEL), _permute_w_in(w_in[l]),
            _qk_gain_rows(qk_gain_dil[l], qk_gain_diff[l]), bd)
        o_dil = _dil_attn(g0.reshape(BATCH, SEQ, DIL_WIDTH), g1.reshape(BATCH, SEQ, DIL_WIDTH),
                          g2.reshape(BATCH, SEQ, DIL_WIDTH), dil_bias)
        o_diff = _diff_attn(qd.reshape(BATCH, SEQ, DIFF_QK_WIDTH), kd.reshape(BATCH, SEQ, DIFF_QK_WIDTH),
                            vd.reshape(BATCH, SEQ, DIFF_V_WIDTH), lambda_q[l], lambda_k[l],
                            diff_subnorm[l].reshape(2, 2, DIFF_V_DIM), lam_init)
        xr = _merge(xr, o_dil.reshape(M_ROWS, DIL_OUT), o_diff.reshape(M_ROWS, DIFF_V_WIDTH), gates,
                    w_branch_dil[l].astype(BF16), w_branch_diff[l].astype(BF16), w_out[l].astype(BF16))
        xr = _ffn(xr, ffn2_norm[l].reshape(1, D_MODEL), ffn2_w_in[l].astype(BF16), ffn2_w_out[l].astype(BF16))
    return xr.reshape(BATCH, SEQ, D_MODEL)
```

```python
import functools
import math

import jax
import jax.numpy as jnp
import numpy as np
from jax import lax
from jax.experimental import pallas as pl
from jax.experimental.pallas import tpu as pltpu

F32 = jnp.float32
BF16 = jnp.bfloat16

D_MODEL = 1024
BATCH = 8
SEQ = 2048
DEPTH = 2
HEAD_DIM = 64
DIL_CONFIGS = ((128, 1), (512, 4), (2048, 16))
N_DIL_GROUPS = 3
HEADS_PER_GROUP = 4
N_DIL_HEADS = 12
N_DIFF_HEADS = 4
DIFF_V_DIM = 128
N_ALIBI_HEADS = 16
D_FF = 2816
BLOCK = 128
EPS = 1e-6
DIL_WIDTH = 768
DIL_OUT = 256
DIFF_QK_WIDTH = 512
DIFF_V_WIDTH = 512
IN_COLS = 5888
NEG = -1e30

LANES = 128
VMEM_LIMIT_BYTES = 56 * 1024 * 1024

M_ROWS = BATCH * SEQ
SEG = 256
N_SEG = IN_COLS // SEG
FFN_TM = 512
FFN_FC = 256
PROJ_TM = 512
MERGE_TM = 512
DIFF_TQ = 512
DIFF_TK = 512
LOG2E = math.log2(math.e)

SLOPES = tuple(float(np.float32(2.0) ** np.float32(-8.0 * i / N_ALIBI_HEADS)) for i in range(1, N_ALIBI_HEADS + 1))


def _rms_rows(x, gain):
    ms = jnp.mean(x * x, axis=-1, keepdims=True)
    return x * lax.rsqrt(ms + EPS) * gain


def _ffn_kernel(x_ref, g_ref, win_ref, wout_ref, o_ref, a_scr):
    x = x_ref[...]
    h = _rms_rows(x, g_ref[...]).astype(BF16)
    for c in range(D_FF // FFN_FC):
        lo = c * FFN_FC
        gate = jnp.dot(h, win_ref[:, lo:lo + FFN_FC], preferred_element_type=F32)
        up = jnp.dot(h, win_ref[:, D_FF + lo:D_FF + lo + FFN_FC], preferred_element_type=F32)
        a_scr[:, lo:lo + FFN_FC] = (gate * jax.nn.sigmoid(gate) * up).astype(BF16)
    y = jnp.dot(a_scr[...], wout_ref[...], preferred_element_type=F32)
    o_ref[...] = x + 0.5 * y


def _ffn(x, gain, w_in, w_out):
    tm = FFN_TM
    const = lambda i: (0, 0)
    return pl.pallas_call(
        _ffn_kernel,
        out_shape=jax.ShapeDtypeStruct((M_ROWS, D_MODEL), F32),
        grid=(M_ROWS // tm,),
        in_specs=[
            pl.BlockSpec((tm, D_MODEL), lambda i: (i, 0)),
            pl.BlockSpec((1, D_MODEL), const),
            pl.BlockSpec((D_MODEL, 2 * D_FF), const, pipeline_mode=pl.Buffered(1)),
            pl.BlockSpec((D_FF, D_MODEL), const, pipeline_mode=pl.Buffered(1)),
        ],
        out_specs=pl.BlockSpec((tm, D_MODEL), lambda i: (i, 0)),
        scratch_shapes=[pltpu.VMEM((tm, D_FF), BF16)],
        compiler_params=pltpu.CompilerParams(
            dimension_semantics=("arbitrary",), vmem_limit_bytes=VMEM_LIMIT_BYTES),
        name="ffn",
    )(x, gain, w_in, w_out)


def _proj_kernel(x_ref, g_ref, w_ref, gain_ref, bd_ref,
                 o_g0, o_g1, o_g2, o_qd, o_kd, o_vd, o_gate, scr1, scr2):
    tm = PROJ_TM
    x = x_ref[...]
    h = _rms_rows(x, g_ref[...]).astype(BF16)
    bd = bd_ref[...]

    def seg(i):
        return jnp.dot(h, w_ref[:, i * SEG:(i + 1) * SEG], preferred_element_type=F32)

    def qk_norm(y, gi):
        sq = y * y
        hi = sq.astype(BF16)
        lo = (sq - hi.astype(F32)).astype(BF16)
        ms = (jnp.dot(hi, bd, preferred_element_type=F32) + jnp.dot(lo, bd, preferred_element_type=F32))
        return y * lax.rsqrt(ms + EPS) * gain_ref[gi:gi + 1, :]

    o_g0[:, 0:SEG] = qk_norm(seg(0), 0).astype(BF16)
    o_g0[:, SEG:2 * SEG] = qk_norm(seg(1), 1).astype(BF16)
    o_g0[:, 2 * SEG:3 * SEG] = seg(2).astype(BF16)
    for g, (scr, o_g) in ((1, (scr1, o_g1)), (2, (scr2, o_g2))):
        d = DIL_CONFIGS[g][1]
        ys = (qk_norm(seg(3 * g), 2 * g), qk_norm(seg(3 * g + 1), 2 * g + 1), seg(3 * g + 2))
        for t, y in enumerate(ys):
            for u in range(SEG // LANES):
                scr[2 * t + u] = y[:, u * LANES:(u + 1) * LANES]
        for c in range(d):
            for t in range(DIL_WIDTH // LANES):
                o_g[0, c, :, t * LANES:(t + 1) * LANES] = scr[t, pl.ds(c, tm // d, stride=d), :].astype(BF16)
    for t in range(2):
        o_qd[:, t * SEG:(t + 1) * SEG] = qk_norm(seg(9 + t), 6 + t).astype(BF16)
        o_kd[:, t * SEG:(t + 1) * SEG] = qk_norm(seg(11 + t), 8 + t).astype(BF16)
        o_vd[:, t * SEG:(t + 1) * SEG] = seg(13 + t).astype(BF16)
    for t in range(8):
        o_gate[:, t * SEG:(t + 1) * SEG] = jax.nn.sigmoid(seg(15 + t)).astype(BF16)


def _proj(x, gain, w, qk_gain, bd):
    tm = PROJ_TM
    tiles_per_seq = SEQ // tm
    const = lambda i: (0, 0)
    row = lambda i: (i, 0)
    perm = lambda i: (i // tiles_per_seq, 0, i % tiles_per_seq, 0)
    out_shape = (
        jax.ShapeDtypeStruct((M_ROWS, DIL_WIDTH), BF16),
        jax.ShapeDtypeStruct((BATCH, 4, SEQ // 4, DIL_WIDTH), BF16),
        jax.ShapeDtypeStruct((BATCH, 16, SEQ // 16, DIL_WIDTH), BF16),
        jax.ShapeDtypeStruct((M_ROWS, DIFF_QK_WIDTH), BF16),
        jax.ShapeDtypeStruct((M_ROWS, DIFF_QK_WIDTH), BF16),
        jax.ShapeDtypeStruct((M_ROWS, DIFF_V_WIDTH), BF16),
        jax.ShapeDtypeStruct((M_ROWS, 2 * D_MODEL), BF16),
    )
    out_specs = (
        pl.BlockSpec((tm, DIL_WIDTH), row),
        pl.BlockSpec((1, 4, tm // 4, DIL_WIDTH), perm),
        pl.BlockSpec((1, 16, tm // 16, DIL_WIDTH), perm),
        pl.BlockSpec((tm, DIFF_QK_WIDTH), row),
        pl.BlockSpec((tm, DIFF_QK_WIDTH), row),
        pl.BlockSpec((tm, DIFF_V_WIDTH), row),
        pl.BlockSpec((tm, 2 * D_MODEL), row),
    )
    return pl.pallas_call(
        _proj_kernel,
        out_shape=out_shape,
        grid=(M_ROWS // tm,),
        in_specs=[
            pl.BlockSpec((tm, D_MODEL), row),
            pl.BlockSpec((1, D_MODEL), const),
            pl.BlockSpec((D_MODEL, IN_COLS), const, pipeline_mode=pl.Buffered(1)),
            pl.BlockSpec((16, SEG), const),
            pl.BlockSpec((SEG, SEG), const),
        ],
        out_specs=out_specs,
        scratch_shapes=[pltpu.VMEM((DIL_WIDTH // LANES, tm, LANES), F32),
                        pltpu.VMEM((DIL_WIDTH // LANES, tm, LANES), F32)],
        compiler_params=pltpu.CompilerParams(
            dimension_semantics=("arbitrary",), vmem_limit_bytes=VMEM_LIMIT_BYTES),
        name="proj",
    )(x, gain, w, qk_gain, bd)


def _dil_block(ref, row_cur, row_prev, bias, lane_head):
    q = ref[0, pl.ds(row_cur, BLOCK), 0:SEG]
    k = ref[0, pl.ds(row_cur, BLOCK), SEG:2 * SEG]
    v = ref[0, pl.ds(row_cur, BLOCK), 2 * SEG:3 * SEG]
    if row_prev is not None:
        k = jnp.concatenate([ref[0, pl.ds(row_prev, BLOCK), SEG:2 * SEG], k], axis=0)
        v = jnp.concatenate([ref[0, pl.ds(row_prev, BLOCK), 2 * SEG:3 * SEG], v], axis=0)
    zero = jnp.zeros_like(q)
    qs = jnp.concatenate([jnp.where(lane_head == j, q, zero) for j in range(HEADS_PER_GROUP)], axis=0)
    s = lax.dot_general(qs, k, (((1,), (1,)), ((), ())), preferred_element_type=F32) + bias
    m = jnp.max(s, axis=-1, keepdims=True)
    p = jnp.exp(s - m)
    l = jnp.sum(p, axis=-1, keepdims=True)
    pv = jnp.dot(p.astype(BF16), v, preferred_element_type=F32)
    o_rows = pv * (1.0 / l)
    lse_rows = m + jnp.log(l)
    out = jnp.zeros((BLOCK, SEG), F32)
    lse = jnp.zeros((BLOCK, SEG), F32)
    for j in range(HEADS_PER_GROUP):
        sel = lane_head == j
        out = jnp.where(sel, o_rows[j * BLOCK:(j + 1) * BLOCK], out)
        lse = jnp.where(sel, jnp.broadcast_to(lse_rows[j * BLOCK:(j + 1) * BLOCK], (BLOCK, SEG)), lse)
    return out, lse


def _dil_kernel(g0_ref, g1_ref, g2_ref, bias_ref, o_ref, o_scr, l_scr):
    j = pl.program_id(1)
    lane_head = lax.broadcasted_iota(jnp.int32, (1, SEG), 1) // HEAD_DIM
    first = (j == 0).astype(jnp.int32)

    row = pl.multiple_of(j * BLOCK, BLOCK)
    prev = pl.multiple_of(jnp.maximum(j - 1, 0) * BLOCK, BLOCK)
    def put(g, rows, out, lse):
        for u in range(SEG // LANES):
            o_scr[g, u, rows, :] = out[:, u * LANES:(u + 1) * LANES]
            l_scr[g, u, rows, :] = lse[:, u * LANES:(u + 1) * LANES]

    out, lse = _dil_block(g0_ref, row, prev, bias_ref[0, first], lane_head)
    put(0, pl.ds(row, BLOCK), out, lse)

    c = j // 4
    n = j % 4
    first1 = (n == 0).astype(jnp.int32)
    prev = pl.multiple_of((c * 4 + jnp.maximum(n - 1, 0)) * BLOCK, BLOCK)
    out, lse = _dil_block(g1_ref, row, prev, bias_ref[1, first1], lane_head)
    put(1, pl.ds(n * (4 * BLOCK) + c, BLOCK, stride=4), out, lse)

    out, lse = _dil_block(g2_ref, row, None, bias_ref[2, 1, :, BLOCK:2 * BLOCK], lane_head)
    put(2, pl.ds(j, BLOCK, stride=16), out, lse)

    @pl.when(j == pl.num_programs(1) - 1)
    def _():
        rows = 256
        for r in range(SEQ // rows):
            sl = pl.ds(r * rows, rows)
            for u in range(SEG // LANES):
                l0, l1, l2 = l_scr[0, u, sl, :], l_scr[1, u, sl, :], l_scr[2, u, sl, :]
                mx = jnp.maximum(jnp.maximum(l0, l1), l2)
                w0, w1, w2 = jnp.exp(l0 - mx), jnp.exp(l1 - mx), jnp.exp(l2 - mx)
                num = w0 * o_scr[0, u, sl, :] + w1 * o_scr[1, u, sl, :] + w2 * o_scr[2, u, sl, :]
                o_ref[0, sl, u * LANES:(u + 1) * LANES] = (num / (w0 + w1 + w2)).astype(BF16)


def _dil_bias_table():
    qi = np.arange(BLOCK)[:, None]
    ki = np.arange(2 * BLOCK)[None, :]
    dist = BLOCK + qi - ki
    valid = (dist >= 0) & (dist <= BLOCK)
    tab = np.full((N_DIL_GROUPS, 2, HEADS_PER_GROUP * BLOCK, 2 * BLOCK), NEG, np.float32)
    for g, (_, dil) in enumerate(DIL_CONFIGS):
        for h in range(HEADS_PER_GROUP):
            slope = np.float32(SLOPES[g * HEADS_PER_GROUP + h])
            b = np.where(valid, -slope * (dil * dist).astype(np.float32), np.float32(NEG)).astype(np.float32)
            tab[g, 0, h * BLOCK:(h + 1) * BLOCK] = b
            tab[g, 1, h * BLOCK:(h + 1) * BLOCK] = np.where(ki >= BLOCK, b, np.float32(NEG))
    return tab


def _dil_attn(g0, g1, g2, bias):
    n_blk = SEQ // BLOCK
    seq_spec = pl.BlockSpec((1, SEQ, DIL_WIDTH), lambda b, j: (b, 0, 0))
    return pl.pallas_call(
        _dil_kernel,
        out_shape=jax.ShapeDtypeStruct((BATCH, SEQ, DIL_OUT), BF16),
        grid=(BATCH, n_blk),
        in_specs=[seq_spec, seq_spec, seq_spec,
                  pl.BlockSpec(bias.shape, lambda b, j: (0, 0, 0, 0))],
        out_specs=pl.BlockSpec((1, SEQ, DIL_OUT), lambda b, j: (b, 0, 0)),
        scratch_shapes=[pltpu.VMEM((N_DIL_GROUPS, DIL_OUT // LANES, SEQ, LANES), F32),
                        pltpu.VMEM((N_DIL_GROUPS, DIL_OUT // LANES, SEQ, LANES), F32)],
        compiler_params=pltpu.CompilerParams(
            dimension_semantics=("arbitrary", "arbitrary"), vmem_limit_bytes=VMEM_LIMIT_BYTES),
        name="dil_attn",
    )(g0, g1, g2, bias)


def _diff_kernel(q_ref, k_ref, v_ref, lq_ref, lk_ref, sn_ref, o_ref,
                 s_scr, mp_scr, m_scr, lp_scr, acc_scr, *, lam_init):
    tq, tk = DIFF_TQ, DIFF_TK
    n_lt = tk // LANES
    pair = pl.program_id(1)
    qi = pl.program_id(2)
    q = q_ref[0]
    lane_map = lax.broadcasted_iota(jnp.int32, (1, SEG), 1) // HEAD_DIM
    zero = jnp.zeros_like(q)
    qs = [jnp.where(lane_map == g, q, zero) for g in range(4)]
    slopes = (jnp.where(pair == 0, SLOPES[N_DIL_HEADS] * LOG2E, SLOPES[N_DIL_HEADS + 2] * LOG2E).astype(F32),
              jnp.where(pair == 0, SLOPES[N_DIL_HEADS + 1] * LOG2E, SLOPES[N_DIL_HEADS + 3] * LOG2E).astype(F32))
    lane_k = lax.broadcasted_iota(jnp.int32, (1, tk), 1)

    mp_scr[...] = jnp.full(mp_scr.shape, NEG, F32)
    lp_scr[...] = jnp.zeros(lp_scr.shape, F32)
    acc_scr[...] = jnp.zeros(acc_scr.shape, F32)

    def lane_tiles(x):
        return [x[:, u * LANES:(u + 1) * LANES] for u in range(n_lt)]

    def pass1(kb, masked):
        k0 = pl.multiple_of(kb * tk, tk)
        k = k_ref[0, pl.ds(k0, tk), :]
        kpos = (k0 + lane_k).astype(F32)
        for g in range(4):
            rows = pl.ds(g * tq, tq)
            s = lax.dot_general(qs[g], k, (((1,), (1,)), ((), ())), preferred_element_type=F32)
            s = s + slopes[g // 2] * kpos
            if masked:
                r = lax.broadcasted_iota(jnp.int32, (tq, tk), 0)
                cidx = lax.broadcasted_iota(jnp.int32, (tq, tk), 1)
                s = jnp.where(cidx > r, NEG, s)
            s_scr[kb, rows, :] = s
            t = lane_tiles(s)
            mp_scr[rows, :] = jnp.maximum(mp_scr[rows, :],
                                          jnp.maximum(jnp.maximum(t[0], t[1]), jnp.maximum(t[2], t[3])))

    def body1(kb, carry):
        pass1(kb, False)
        return carry

    lax.fori_loop(0, qi, body1, 0)
    pass1(qi, True)

    m_scr[...] = jnp.broadcast_to(jnp.max(mp_scr[...], axis=-1, keepdims=True), m_scr.shape)

    def body2(kb, carry):
        k0 = pl.multiple_of(kb * tk, tk)
        v = v_ref[0, pl.ds(k0, tk), :]
        for g in range(4):
            rows = pl.ds(g * tq, tq)
            m = m_scr[rows, :]
            p = jnp.exp2(s_scr[kb, rows, :] - jnp.concatenate([m] * n_lt, axis=1))
            t = lane_tiles(p)
            lp_scr[rows, :] += (t[0] + t[1]) + (t[2] + t[3])
            acc_scr[rows, :] += jnp.dot(p.astype(BF16), v, preferred_element_type=F32)
        return carry

    lax.fori_loop(0, qi + 1, body2, 0)

    lq = lq_ref[...]
    lk = lk_ref[...]
    lam = (jnp.exp(jnp.sum(lq[0:1] * lk[0:1], axis=-1, keepdims=True))
           - jnp.exp(jnp.sum(lq[1:2] * lk[1:2], axis=-1, keepdims=True)) + lam_init)
    inv = 1.0 / jnp.sum(lp_scr[...], axis=-1, keepdims=True)
    outs = []
    for hh in range(2):
        cols = slice(hh * DIFF_V_DIM, (hh + 1) * DIFF_V_DIM)
        r1 = slice((2 * hh) * tq, (2 * hh + 1) * tq)
        r2 = slice((2 * hh + 1) * tq, (2 * hh + 2) * tq)
        o = acc_scr[r1, cols] * inv[r1] - lam * (acc_scr[r2, cols] * inv[r2])
        outs.append(_rms_rows(o, sn_ref[0, hh:hh + 1, :]) * (1.0 - lam_init))
    o_ref[0] = jnp.concatenate(outs, axis=-1).astype(BF16)


def _diff_attn(qd, kd, vd, lq, lk, subnorm, lam_init):
    tq = DIFF_TQ
    kv_spec = pl.BlockSpec((1, SEQ, SEG), lambda b, p, i: (b, 0, p))
    return pl.pallas_call(
        functools.partial(_diff_kernel, lam_init=lam_init),
        out_shape=jax.ShapeDtypeStruct((BATCH, SEQ, DIFF_V_WIDTH), BF16),
        grid=(BATCH, 2, SEQ // tq),
        in_specs=[
            pl.BlockSpec((1, tq, SEG), lambda b, p, i: (b, i, p)),
            kv_spec, kv_spec,
            pl.BlockSpec((2, HEAD_DIM), lambda b, p, i: (0, 0)),
            pl.BlockSpec((2, HEAD_DIM), lambda b, p, i: (0, 0)),
            pl.BlockSpec((1, 2, DIFF_V_DIM), lambda b, p, i: (p, 0, 0)),
        ],
        out_specs=pl.BlockSpec((1, tq, SEG), lambda b, p, i: (b, i, p)),
        scratch_shapes=[pltpu.VMEM((SEQ // DIFF_TK, 4 * tq, DIFF_TK), F32),
                        pltpu.VMEM((4 * tq, LANES), F32),
                        pltpu.VMEM((4 * tq, LANES), F32),
                        pltpu.VMEM((4 * tq, LANES), F32),
                        pltpu.VMEM((4 * tq, SEG), F32)],
        compiler_params=pltpu.CompilerParams(
            dimension_semantics=("arbitrary", "arbitrary", "arbitrary"), vmem_limit_bytes=VMEM_LIMIT_BYTES),
        name="diff_attn",
    )(qd, kd, vd, lq, lk, subnorm)


def _merge_kernel(x_ref, od_ref, of_ref, gate_ref, wa_ref, wb_ref, wo_ref, o_ref):
    ya = jnp.dot(od_ref[...], wa_ref[...], preferred_element_type=F32)
    yb = jnp.dot(of_ref[...], wb_ref[...], preferred_element_type=F32)
    y = gate_ref[:, 0:D_MODEL].astype(F32) * ya + gate_ref[:, D_MODEL:2 * D_MODEL].astype(F32) * yb
    o_ref[...] = x_ref[...] + jnp.dot(y.astype(BF16), wo_ref[...], preferred_element_type=F32)


def _merge(x, o_dil, o_diff, gates, wa, wb, wo):
    tm = MERGE_TM
    const = lambda i: (0, 0)
    row = lambda i: (i, 0)
    return pl.pallas_call(
        _merge_kernel,
        out_shape=jax.ShapeDtypeStruct((M_ROWS, D_MODEL), F32),
        grid=(M_ROWS // tm,),
        in_specs=[
            pl.BlockSpec((tm, D_MODEL), row),
            pl.BlockSpec((tm, DIL_OUT), row),
            pl.BlockSpec((tm, DIFF_V_WIDTH), row),
            pl.BlockSpec((tm, 2 * D_MODEL), row),
            pl.BlockSpec((DIL_OUT, D_MODEL), const),
            pl.BlockSpec((DIFF_V_WIDTH, D_MODEL), const),
            pl.BlockSpec((D_MODEL, D_MODEL), const),
        ],
        out_specs=pl.BlockSpec((tm, D_MODEL), row),
        compiler_params=pltpu.CompilerParams(
            dimension_semantics=("arbitrary",), vmem_limit_bytes=VMEM_LIMIT_BYTES),
        name="merge",
    )(x, o_dil, o_diff, gates, wa, wb, wo)


def _permute_w_in(w):
    parts = []
    for g in range(N_DIL_GROUPS):
        for t in range(3):
            lo = t * DIL_WIDTH + g * SEG
            parts.append(w[:, lo:lo + SEG])
    parts.append(w[:, 3 * DIL_WIDTH:])
    return jnp.concatenate(parts, axis=1).astype(BF16)


def _qk_gain_rows(qk_gain_dil, qk_gain_diff):
    scale = HEAD_DIM ** -0.5
    rows = []
    for g in range(N_DIL_GROUPS):
        rows.append(qk_gain_dil[0, g].reshape(SEG) * scale)
        rows.append(qk_gain_dil[1, g].reshape(SEG))
    qd = qk_gain_diff[0].reshape(2, SEG) * (scale * LOG2E)
    kd = qk_gain_diff[1].reshape(2, SEG)
    rows += [qd[0], qd[1], kd[0], kd[1]]
    rows += [jnp.zeros((SEG,), F32)] * (16 - len(rows))
    return jnp.stack(rows).astype(F32)


def kernel(x, ffn1_norm, ffn1_w_in, ffn1_w_out, mix_norm, w_in, qk_gain_dil, qk_gain_diff, lambda_q, lambda_k,
           diff_subnorm, w_branch_dil, w_branch_diff, w_out, ffn2_norm, ffn2_w_in, ffn2_w_out):
    b, s, d = x.shape
    assert (b, s, d) == (BATCH, SEQ, D_MODEL)
    xr = x.reshape(M_ROWS, D_MODEL)
    bd = jnp.asarray(np.kron(np.eye(SEG // HEAD_DIM), np.full((HEAD_DIM, HEAD_DIM), 1.0 / HEAD_DIM)), BF16)
    dil_bias = jnp.asarray(_dil_bias_table())
    for l in range(DEPTH):
        lam_init = 0.8 - 0.6 * math.exp(-0.3 * l)
        xr = _ffn(xr, ffn1_norm[l].reshape(1, D_MODEL), ffn1_w_in[l].astype(BF16), ffn1_w_out[l].astype(BF16))
        g0, g1, g2, qd, kd, vd, gates = _proj(
            xr, mix_norm[l].reshape(1, D_MODEL), _permute_w_in(w_in[l]),
            _qk_gain_rows(qk_gain_dil[l], qk_gain_diff[l]), bd)
        o_dil = _dil_attn(g0.reshape(BATCH, SEQ, DIL_WIDTH), g1.reshape(BATCH, SEQ, DIL_WIDTH),
                          g2.reshape(BATCH, SEQ, DIL_WIDTH), dil_bias)
        o_diff = _diff_attn(qd.reshape(BATCH, SEQ, DIFF_QK_WIDTH), kd.reshape(BATCH, SEQ, DIFF_QK_WIDTH),
                            vd.reshape(BATCH, SEQ, DIFF_V_WIDTH), lambda_q[l], lambda_k[l],
                            diff_subnorm[l].reshape(2, 2, DIFF_V_DIM), lam_init)
        xr = _merge(xr, o_dil.reshape(M_ROWS, DIL_OUT), o_diff.reshape(M_ROWS, DIFF_V_WIDTH), gates,
                    w_branch_dil[l].astype(BF16), w_branch_diff[l].astype(BF16), w_out[l].astype(BF16))
        xr = _ffn(xr, ffn2_norm[l].reshape(1, D_MODEL), ffn2_w_in[l].astype(BF16), ffn2_w_out[l].astype(BF16))
    return xr.reshape(BATCH, SEQ, D_MODEL)
```

```python
import functools
import math

import jax
import jax.numpy as jnp
import numpy as np
from jax import lax
from jax.experimental import pallas as pl
from jax.experimental.pallas import tpu as pltpu

F32 = jnp.float32
BF16 = jnp.bfloat16

D_MODEL = 1024
BATCH = 8
SEQ = 2048
DEPTH = 2
HEAD_DIM = 64
DIL_CONFIGS = ((128, 1), (512, 4), (2048, 16))
N_DIL_GROUPS = 3
HEADS_PER_GROUP = 4
N_DIL_HEADS = 12
N_DIFF_HEADS = 4
DIFF_V_DIM = 128
N_ALIBI_HEADS = 16
D_FF = 2816
BLOCK = 128
EPS = 1e-6
DIL_WIDTH = 768
DIL_OUT = 256
DIFF_QK_WIDTH = 512
DIFF_V_WIDTH = 512
IN_COLS = 5888
NEG = -1e30

LANES = 128
VMEM_LIMIT_BYTES = 56 * 1024 * 1024

M_ROWS = BATCH * SEQ
SEG = 256
N_SEG = IN_COLS // SEG
FFN_TM = 512
FFN_FC = 256
PROJ_TM = 512
CAST_ROWS = 256
DIFF_TQ = 512
DIFF_TK = 512
LOG2E = math.log2(math.e)

SLOPES = tuple(float(np.float32(2.0) ** np.float32(-8.0 * i / N_ALIBI_HEADS)) for i in range(1, N_ALIBI_HEADS + 1))


def _rms_rows(x, gain):
    ms = jnp.mean(x * x, axis=-1, keepdims=True)
    return x * lax.rsqrt(ms + EPS) * gain


def _ffn_rows(x, g_ref, win_ref, wout_ref, a_scr):
    h = _rms_rows(x, g_ref[...]).astype(BF16)
    for c in range(D_FF // FFN_FC):
        lo = c * FFN_FC
        gate = jnp.dot(h, win_ref[:, lo:lo + FFN_FC], preferred_element_type=F32)
        up = jnp.dot(h, win_ref[:, D_FF + lo:D_FF + lo + FFN_FC], preferred_element_type=F32)
        a_scr[:, lo:lo + FFN_FC] = (gate * jax.nn.sigmoid(gate) * up).astype(BF16)
    y = jnp.dot(a_scr[...], wout_ref[...], preferred_element_type=F32)
    return x + 0.5 * y


def _ffn_kernel(x_ref, g_ref, win_ref, wout_ref, o_ref, a_scr):
    o_ref[...] = _ffn_rows(x_ref[...], g_ref, win_ref, wout_ref, a_scr)


def _merge_ffn_kernel(x_ref, od_ref, of_ref, gate_ref, wa_ref, wb_ref, wo_ref, g_ref, win_ref, wout_ref,
                      o_ref, a_scr):
    ya = jnp.dot(od_ref[...], wa_ref[...], preferred_element_type=F32)
    yb = jnp.dot(of_ref[...], wb_ref[...], preferred_element_type=F32)
    y = gate_ref[:, 0:D_MODEL].astype(F32) * ya + gate_ref[:, D_MODEL:2 * D_MODEL].astype(F32) * yb
    x = x_ref[...] + jnp.dot(y.astype(BF16), wo_ref[...], preferred_element_type=F32)
    o_ref[...] = _ffn_rows(x, g_ref, win_ref, wout_ref, a_scr)


def _resident(shape):
    return pl.BlockSpec(shape, lambda i: (0,) * len(shape), pipeline_mode=pl.Buffered(1))


def _resident_layer(shape, layer):
    return pl.BlockSpec((None,) + shape, lambda i: (layer,) + (0,) * len(shape), pipeline_mode=pl.Buffered(1))


def _ffn(x, gain, w_in, w_out, layer):
    tm = FFN_TM
    row = lambda i: (i, 0)
    return pl.pallas_call(
        _ffn_kernel,
        out_shape=jax.ShapeDtypeStruct((M_ROWS, D_MODEL), F32),
        grid=(M_ROWS // tm,),
        in_specs=[
            pl.BlockSpec((tm, D_MODEL), row),
            _resident((1, D_MODEL)),
            _resident_layer((D_MODEL, 2 * D_FF), layer),
            _resident_layer((D_FF, D_MODEL), layer),
        ],
        out_specs=pl.BlockSpec((tm, D_MODEL), row),
        scratch_shapes=[pltpu.VMEM((tm, D_FF), BF16)],
        compiler_params=pltpu.CompilerParams(
            dimension_semantics=("arbitrary",), vmem_limit_bytes=VMEM_LIMIT_BYTES),
        name="ffn",
    )(x, gain, w_in, w_out)


def _merge_ffn(x, o_dil, o_diff, gates, wa, wb, wo, gain, w_in, w_out, layer):
    tm = FFN_TM
    row = lambda i: (i, 0)
    return pl.pallas_call(
        _merge_ffn_kernel,
        out_shape=jax.ShapeDtypeStruct((M_ROWS, D_MODEL), F32),
        grid=(M_ROWS // tm,),
        in_specs=[
            pl.BlockSpec((tm, D_MODEL), row),
            pl.BlockSpec((tm, DIL_OUT), row),
            pl.BlockSpec((tm, DIFF_V_WIDTH), row),
            pl.BlockSpec((tm, 2 * D_MODEL), row),
            _resident_layer((DIL_OUT, D_MODEL), layer),
            _resident_layer((DIFF_V_WIDTH, D_MODEL), layer),
            _resident_layer((D_MODEL, D_MODEL), layer),
            _resident((1, D_MODEL)),
            _resident_layer((D_MODEL, 2 * D_FF), layer),
            _resident_layer((D_FF, D_MODEL), layer),
        ],
        out_specs=pl.BlockSpec((tm, D_MODEL), row),
        scratch_shapes=[pltpu.VMEM((tm, D_FF), BF16)],
        compiler_params=pltpu.CompilerParams(
            dimension_semantics=("arbitrary",), vmem_limit_bytes=VMEM_LIMIT_BYTES),
        name="merge_ffn",
    )(x, o_dil, o_diff, gates, wa, wb, wo, gain, w_in, w_out)


def _cast_kernel(w_ref, o_ref):
    o_ref[...] = w_ref[...].astype(BF16)


def _cast_bf16(w, block_rows):
    rows, cols = w.shape
    return pl.pallas_call(
        _cast_kernel,
        out_shape=jax.ShapeDtypeStruct((rows, cols), BF16),
        grid=(rows // block_rows,),
        in_specs=[pl.BlockSpec((block_rows, cols), lambda i: (i, 0))],
        out_specs=pl.BlockSpec((block_rows, cols), lambda i: (i, 0)),
        compiler_params=pltpu.CompilerParams(
            dimension_semantics=("arbitrary",), vmem_limit_bytes=VMEM_LIMIT_BYTES),
        name="cast_bf16",
    )(w)


def _w_in_src_seg(j):
    return jnp.where(j < 3 * N_DIL_GROUPS, (j % 3) * N_DIL_GROUPS + j // 3, j)


def _cast_permute_w_in(w):
    rows = w.shape[0]
    return pl.pallas_call(
        _cast_kernel,
        out_shape=jax.ShapeDtypeStruct((rows, IN_COLS), BF16),
        grid=(N_SEG,),
        in_specs=[pl.BlockSpec((rows, SEG), lambda j: (0, _w_in_src_seg(j)))],
        out_specs=pl.BlockSpec((rows, SEG), lambda j: (0, j)),
        compiler_params=pltpu.CompilerParams(
            dimension_semantics=("arbitrary",), vmem_limit_bytes=VMEM_LIMIT_BYTES),
        name="cast_permute_w_in",
    )(w)


def _proj_kernel(x_ref, g_ref, w_ref, gain_ref, bd_ref,
                 o_g0, o_g1, o_g2, o_qd, o_kd, o_vd, o_gate, scr1, scr2):
    tm = PROJ_TM
    x = x_ref[...]
    h = _rms_rows(x, g_ref[...]).astype(BF16)
    bd = bd_ref[...]

    def seg(i):
        return jnp.dot(h, w_ref[:, i * SEG:(i + 1) * SEG], preferred_element_type=F32)

    def qk_norm(y, gi):
        ms = jnp.dot((y * y).astype(BF16), bd, preferred_element_type=F32)
        return y * lax.rsqrt(ms + EPS) * gain_ref[gi:gi + 1, :]

    o_g0[:, 0:SEG] = qk_norm(seg(0), 0).astype(BF16)
    o_g0[:, SEG:2 * SEG] = qk_norm(seg(1), 1).astype(BF16)
    o_g0[:, 2 * SEG:3 * SEG] = seg(2).astype(BF16)
    for g, (scr, o_g) in ((1, (scr1, o_g1)), (2, (scr2, o_g2))):
        d = DIL_CONFIGS[g][1]
        ys = (qk_norm(seg(3 * g), 2 * g), qk_norm(seg(3 * g + 1), 2 * g + 1), seg(3 * g + 2))
        for t, y in enumerate(ys):
            for u in range(SEG // LANES):
                scr[2 * t + u] = y[:, u * LANES:(u + 1) * LANES]
        for c in range(d):
            for t in range(DIL_WIDTH // LANES):
                o_g[0, c, :, t * LANES:(t + 1) * LANES] = scr[t, pl.ds(c, tm // d, stride=d), :].astype(BF16)
    for t in range(2):
        o_qd[:, t * SEG:(t + 1) * SEG] = qk_norm(seg(9 + t), 6 + t).astype(BF16)
        o_kd[:, t * SEG:(t + 1) * SEG] = qk_norm(seg(11 + t), 8 + t).astype(BF16)
        o_vd[:, t * SEG:(t + 1) * SEG] = seg(13 + t).astype(BF16)
    for t in range(8):
        o_gate[:, t * SEG:(t + 1) * SEG] = jax.nn.sigmoid(seg(15 + t)).astype(BF16)


def _proj(x, gain, w, qk_gain, bd, layer):
    tm = PROJ_TM
    tiles_per_seq = SEQ // tm
    const = lambda i: (0, 0)
    row = lambda i: (i, 0)
    perm = lambda i: (i // tiles_per_seq, 0, i % tiles_per_seq, 0)
    out_shape = (
        jax.ShapeDtypeStruct((M_ROWS, DIL_WIDTH), BF16),
        jax.ShapeDtypeStruct((BATCH, 4, SEQ // 4, DIL_WIDTH), BF16),
        jax.ShapeDtypeStruct((BATCH, 16, SEQ // 16, DIL_WIDTH), BF16),
        jax.ShapeDtypeStruct((M_ROWS, DIFF_QK_WIDTH), BF16),
        jax.ShapeDtypeStruct((M_ROWS, DIFF_QK_WIDTH), BF16),
        jax.ShapeDtypeStruct((M_ROWS, DIFF_V_WIDTH), BF16),
        jax.ShapeDtypeStruct((M_ROWS, 2 * D_MODEL), BF16),
    )
    out_specs = (
        pl.BlockSpec((tm, DIL_WIDTH), row),
        pl.BlockSpec((1, 4, tm // 4, DIL_WIDTH), perm),
        pl.BlockSpec((1, 16, tm // 16, DIL_WIDTH), perm),
        pl.BlockSpec((tm, DIFF_QK_WIDTH), row),
        pl.BlockSpec((tm, DIFF_QK_WIDTH), row),
        pl.BlockSpec((tm, DIFF_V_WIDTH), row),
        pl.BlockSpec((tm, 2 * D_MODEL), row),
    )
    return pl.pallas_call(
        _proj_kernel,
        out_shape=out_shape,
        grid=(M_ROWS // tm,),
        in_specs=[
            pl.BlockSpec((tm, D_MODEL), row),
            pl.BlockSpec((1, D_MODEL), const),
            _resident_layer((D_MODEL, IN_COLS), layer),
            pl.BlockSpec((16, SEG), const),
            pl.BlockSpec((SEG, SEG), const),
        ],
        out_specs=out_specs,
        scratch_shapes=[pltpu.VMEM((DIL_WIDTH // LANES, tm, LANES), F32),
                        pltpu.VMEM((DIL_WIDTH // LANES, tm, LANES), F32)],
        compiler_params=pltpu.CompilerParams(
            dimension_semantics=("arbitrary",), vmem_limit_bytes=VMEM_LIMIT_BYTES),
        name="proj",
    )(x, gain, w, qk_gain, bd)


def _dil_block(ref, row_cur, row_prev, bias, lane_head):
    q = ref[0, pl.ds(row_cur, BLOCK), 0:SEG]
    k = ref[0, pl.ds(row_cur, BLOCK), SEG:2 * SEG]
    v = ref[0, pl.ds(row_cur, BLOCK), 2 * SEG:3 * SEG]
    if row_prev is not None:
        k = jnp.concatenate([ref[0, pl.ds(row_prev, BLOCK), SEG:2 * SEG], k], axis=0)
        v = jnp.concatenate([ref[0, pl.ds(row_prev, BLOCK), 2 * SEG:3 * SEG], v], axis=0)
    zero = jnp.zeros_like(q)
    qs = jnp.concatenate([jnp.where(lane_head == j, q, zero) for j in range(HEADS_PER_GROUP)], axis=0)
    s = lax.dot_general(qs, k, (((1,), (1,)), ((), ())), preferred_element_type=F32) + bias
    m = jnp.max(s, axis=-1, keepdims=True)
    p = jnp.exp2(s - m)
    l = jnp.sum(p, axis=-1, keepdims=True)
    pv = jnp.dot(p.astype(BF16), v, preferred_element_type=F32)
    o_rows = pv * (1.0 / l)
    lse_rows = m + jnp.log2(l)
    out = jnp.zeros((BLOCK, SEG), F32)
    lse = jnp.zeros((BLOCK, SEG), F32)
    for j in range(HEADS_PER_GROUP):
        sel = lane_head == j
        out = jnp.where(sel, o_rows[j * BLOCK:(j + 1) * BLOCK], out)
        lse = jnp.where(sel, jnp.broadcast_to(lse_rows[j * BLOCK:(j + 1) * BLOCK], (BLOCK, SEG)), lse)
    return out, lse


def _dil_kernel(g0_ref, g1_ref, g2_ref, bias_ref, o_ref, o_scr, l_scr):
    j = pl.program_id(1)
    lane_head = lax.broadcasted_iota(jnp.int32, (1, SEG), 1) // HEAD_DIM
    first = (j == 0).astype(jnp.int32)

    row = pl.multiple_of(j * BLOCK, BLOCK)
    prev = pl.multiple_of(jnp.maximum(j - 1, 0) * BLOCK, BLOCK)
    def put(g, rows, out, lse):
        for u in range(SEG // LANES):
            o_scr[g, u, rows, :] = out[:, u * LANES:(u + 1) * LANES]
            l_scr[g, u, rows, :] = lse[:, u * LANES:(u + 1) * LANES]

    out, lse = _dil_block(g0_ref, row, prev, bias_ref[0, first], lane_head)
    put(0, pl.ds(row, BLOCK), out, lse)

    c = j // 4
    n = j % 4
    first1 = (n == 0).astype(jnp.int32)
    prev = pl.multiple_of((c * 4 + jnp.maximum(n - 1, 0)) * BLOCK, BLOCK)
    out, lse = _dil_block(g1_ref, row, prev, bias_ref[1, first1], lane_head)
    put(1, pl.ds(n * (4 * BLOCK) + c, BLOCK, stride=4), out, lse)

    out, lse = _dil_block(g2_ref, row, None, bias_ref[2, 1, :, BLOCK:2 * BLOCK], lane_head)
    put(2, pl.ds(j, BLOCK, stride=16), out, lse)

    @pl.when(j == pl.num_programs(1) - 1)
    def _():
        rows = 256
        for r in range(SEQ // rows):
            sl = pl.ds(r * rows, rows)
            for u in range(SEG // LANES):
                l0, l1, l2 = l_scr[0, u, sl, :], l_scr[1, u, sl, :], l_scr[2, u, sl, :]
                mx = jnp.maximum(jnp.maximum(l0, l1), l2)
                w0, w1, w2 = jnp.exp2(l0 - mx), jnp.exp2(l1 - mx), jnp.exp2(l2 - mx)
                num = w0 * o_scr[0, u, sl, :] + w1 * o_scr[1, u, sl, :] + w2 * o_scr[2, u, sl, :]
                o_ref[0, sl, u * LANES:(u + 1) * LANES] = (num / (w0 + w1 + w2)).astype(BF16)


def _dil_bias_table():
    qi = np.arange(BLOCK)[:, None]
    ki = np.arange(2 * BLOCK)[None, :]
    dist = BLOCK + qi - ki
    valid = (dist >= 0) & (dist <= BLOCK)
    tab = np.full((N_DIL_GROUPS, 2, HEADS_PER_GROUP * BLOCK, 2 * BLOCK), NEG, np.float32)
    for g, (_, dil) in enumerate(DIL_CONFIGS):
        for h in range(HEADS_PER_GROUP):
            slope = np.float32(SLOPES[g * HEADS_PER_GROUP + h])
            b = np.where(valid, -slope * np.float32(LOG2E) * (dil * dist).astype(np.float32),
                         np.float32(NEG)).astype(np.float32)
            tab[g, 0, h * BLOCK:(h + 1) * BLOCK] = b
            tab[g, 1, h * BLOCK:(h + 1) * BLOCK] = np.where(ki >= BLOCK, b, np.float32(NEG))
    return tab


def _dil_attn(g0, g1, g2, bias):
    n_blk = SEQ // BLOCK
    seq_spec = pl.BlockSpec((1, SEQ, DIL_WIDTH), lambda b, j: (b, 0, 0))
    return pl.pallas_call(
        _dil_kernel,
        out_shape=jax.ShapeDtypeStruct((BATCH, SEQ, DIL_OUT), BF16),
        grid=(BATCH, n_blk),
        in_specs=[seq_spec, seq_spec, seq_spec,
                  pl.BlockSpec(bias.shape, lambda b, j: (0, 0, 0, 0))],
        out_specs=pl.BlockSpec((1, SEQ, DIL_OUT), lambda b, j: (b, 0, 0)),
        scratch_shapes=[pltpu.VMEM((N_DIL_GROUPS, DIL_OUT // LANES, SEQ, LANES), F32),
                        pltpu.VMEM((N_DIL_GROUPS, DIL_OUT // LANES, SEQ, LANES), F32)],
        compiler_params=pltpu.CompilerParams(
            dimension_semantics=("arbitrary", "arbitrary"), vmem_limit_bytes=VMEM_LIMIT_BYTES),
        name="dil_attn",
    )(g0, g1, g2, bias)


def _diff_kernel(q_ref, k_ref, v_ref, lq_ref, lk_ref, sn_ref, o_ref,
                 s_scr, mp_scr, m_scr, lp_scr, acc_scr, *, lam_init):
    tq, tk = DIFF_TQ, DIFF_TK
    n_lt = tk // LANES
    pair = pl.program_id(1)
    qi = pl.program_id(2)
    q = q_ref[0]
    lane_map = lax.broadcasted_iota(jnp.int32, (1, SEG), 1) // HEAD_DIM
    zero = jnp.zeros_like(q)
    qs = [jnp.where(lane_map == g, q, zero) for g in range(4)]
    slopes = (jnp.where(pair == 0, SLOPES[N_DIL_HEADS] * LOG2E, SLOPES[N_DIL_HEADS + 2] * LOG2E).astype(F32),
              jnp.where(pair == 0, SLOPES[N_DIL_HEADS + 1] * LOG2E, SLOPES[N_DIL_HEADS + 3] * LOG2E).astype(F32))
    lane_k = lax.broadcasted_iota(jnp.int32, (1, tk), 1)

    mp_scr[...] = jnp.full(mp_scr.shape, NEG, F32)
    lp_scr[...] = jnp.zeros(lp_scr.shape, F32)
    acc_scr[...] = jnp.zeros(acc_scr.shape, F32)

    def lane_tiles(x):
        return [x[:, u * LANES:(u + 1) * LANES] for u in range(n_lt)]

    def pass1(kb, masked):
        k0 = pl.multiple_of(kb * tk, tk)
        k = k_ref[0, pl.ds(k0, tk), :]
        kpos = (k0 + lane_k).astype(F32)
        for g in range(4):
            rows = pl.ds(g * tq, tq)
            s = lax.dot_general(qs[g], k, (((1,), (1,)), ((), ())), preferred_element_type=F32)
            s = s + slopes[g // 2] * kpos
            if masked:
                r = lax.broadcasted_iota(jnp.int32, (tq, tk), 0)
                cidx = lax.broadcasted_iota(jnp.int32, (tq, tk), 1)
                s = jnp.where(cidx > r, NEG, s)
            s_scr[kb, rows, :] = s
            t = lane_tiles(s)
            mp_scr[rows, :] = jnp.maximum(mp_scr[rows, :],
                                          jnp.maximum(jnp.maximum(t[0], t[1]), jnp.maximum(t[2], t[3])))

    def body1(kb, carry):
        pass1(kb, False)
        return carry

    lax.fori_loop(0, qi, body1, 0)
    pass1(qi, True)

    m_scr[...] = jnp.broadcast_to(jnp.max(mp_scr[...], axis=-1, keepdims=True), m_scr.shape)

    def body2(kb, carry):
        k0 = pl.multiple_of(kb * tk, tk)
        v = v_ref[0, pl.ds(k0, tk), :]
        for g in range(4):
            rows = pl.ds(g * tq, tq)
            m = m_scr[rows, :]
            p = jnp.exp2(s_scr[kb, rows, :] - jnp.concatenate([m] * n_lt, axis=1))
            t = lane_tiles(p)
            lp_scr[rows, :] += (t[0] + t[1]) + (t[2] + t[3])
            acc_scr[rows, :] += jnp.dot(p.astype(BF16), v, preferred_element_type=F32)
        return carry

    lax.fori_loop(0, qi + 1, body2, 0)

    lq = lq_ref[...]
    lk = lk_ref[...]
    lam = (jnp.exp(jnp.sum(lq[0:1] * lk[0:1], axis=-1, keepdims=True))
           - jnp.exp(jnp.sum(lq[1:2] * lk[1:2], axis=-1, keepdims=True)) + lam_init)
    inv = 1.0 / jnp.sum(lp_scr[...], axis=-1, keepdims=True)
    outs = []
    for hh in range(2):
        cols = slice(hh * DIFF_V_DIM, (hh + 1) * DIFF_V_DIM)
        r1 = slice((2 * hh) * tq, (2 * hh + 1) * tq)
        r2 = slice((2 * hh + 1) * tq, (2 * hh + 2) * tq)
        o = acc_scr[r1, cols] * inv[r1] - lam * (acc_scr[r2, cols] * inv[r2])
        outs.append(_rms_rows(o, sn_ref[0, hh:hh + 1, :]) * (1.0 - lam_init))
    o_ref[0] = jnp.concatenate(outs, axis=-1).astype(BF16)


def _diff_attn(qd, kd, vd, lq, lk, subnorm, lam_init):
    tq = DIFF_TQ
    kv_spec = pl.BlockSpec((1, SEQ, SEG), lambda b, p, i: (b, 0, p))
    return pl.pallas_call(
        functools.partial(_diff_kernel, lam_init=lam_init),
        out_shape=jax.ShapeDtypeStruct((BATCH, SEQ, DIFF_V_WIDTH), BF16),
        grid=(BATCH, 2, SEQ // tq),
        in_specs=[
            pl.BlockSpec((1, tq, SEG), lambda b, p, i: (b, i, p)),
            kv_spec, kv_spec,
            pl.BlockSpec((2, HEAD_DIM), lambda b, p, i: (0, 0)),
            pl.BlockSpec((2, HEAD_DIM), lambda b, p, i: (0, 0)),
            pl.BlockSpec((1, 2, DIFF_V_DIM), lambda b, p, i: (p, 0, 0)),
        ],
        out_specs=pl.BlockSpec((1, tq, SEG), lambda b, p, i: (b, i, p)),
        scratch_shapes=[pltpu.VMEM((SEQ // DIFF_TK, 4 * tq, DIFF_TK), F32),
                        pltpu.VMEM((4 * tq, LANES), F32),
                        pltpu.VMEM((4 * tq, LANES), F32),
                        pltpu.VMEM((4 * tq, LANES), F32),
                        pltpu.VMEM((4 * tq, SEG), F32)],
        compiler_params=pltpu.CompilerParams(
            dimension_semantics=("arbitrary", "arbitrary", "arbitrary"), vmem_limit_bytes=VMEM_LIMIT_BYTES),
        name="diff_attn",
    )(qd, kd, vd, lq, lk, subnorm)


def _qk_gain_rows(qk_gain_dil, qk_gain_diff):
    scale = HEAD_DIM ** -0.5 * LOG2E
    rows = []
    for g in range(N_DIL_GROUPS):
        rows.append(qk_gain_dil[0, g].reshape(SEG) * scale)
        rows.append(qk_gain_dil[1, g].reshape(SEG))
    qd = qk_gain_diff[0].reshape(2, SEG) * scale
    kd = qk_gain_diff[1].reshape(2, SEG)
    rows += [qd[0], qd[1], kd[0], kd[1]]
    rows += [jnp.zeros((SEG,), F32)] * (16 - len(rows))
    return jnp.stack(rows).astype(F32)


def _cast_stacked(w, block_rows):
    depth, rows, cols = w.shape
    return _cast_bf16(w.reshape(depth * rows, cols), block_rows).reshape(depth, rows, cols)


def kernel(x, ffn1_norm, ffn1_w_in, ffn1_w_out, mix_norm, w_in, qk_gain_dil, qk_gain_diff, lambda_q, lambda_k,
           diff_subnorm, w_branch_dil, w_branch_diff, w_out, ffn2_norm, ffn2_w_in, ffn2_w_out):
    b, s, d = x.shape
    assert (b, s, d) == (BATCH, SEQ, D_MODEL)
    xr = x.reshape(M_ROWS, D_MODEL)
    bd = jnp.asarray(np.kron(np.eye(SEG // HEAD_DIM), np.full((HEAD_DIM, HEAD_DIM), 1.0 / HEAD_DIM)), BF16)
    dil_bias = jnp.asarray(_dil_bias_table())
    f1_in, f2_in = _cast_stacked(ffn1_w_in, CAST_ROWS), _cast_stacked(ffn2_w_in, CAST_ROWS)
    f1_out, f2_out = _cast_stacked(ffn1_w_out, CAST_ROWS), _cast_stacked(ffn2_w_out, CAST_ROWS)
    w_in_p = _cast_permute_w_in(w_in.reshape(DEPTH * D_MODEL, IN_COLS)).reshape(DEPTH, D_MODEL, IN_COLS)
    wa, wb, wo = w_branch_dil.astype(BF16), w_branch_diff.astype(BF16), w_out.astype(BF16)
    for l in range(DEPTH):
        lam_init = 0.8 - 0.6 * math.exp(-0.3 * l)
        xr = _ffn(xr, ffn1_norm[l].reshape(1, D_MODEL), f1_in, f1_out, l)
        g0, g1, g2, qd, kd, vd, gates = _proj(
            xr, mix_norm[l].reshape(1, D_MODEL), w_in_p, _qk_gain_rows(qk_gain_dil[l], qk_gain_diff[l]), bd, l)
        o_dil = _dil_attn(g0.reshape(BATCH, SEQ, DIL_WIDTH), g1.reshape(BATCH, SEQ, DIL_WIDTH),
                          g2.reshape(BATCH, SEQ, DIL_WIDTH), dil_bias)
        o_diff = _diff_attn(qd.reshape(BATCH, SEQ, DIFF_QK_WIDTH), kd.reshape(BATCH, SEQ, DIFF_QK_WIDTH),
                            vd.reshape(BATCH, SEQ, DIFF_V_WIDTH), lambda_q[l], lambda_k[l],
                            diff_subnorm[l].reshape(2, 2, DIFF_V_DIM), lam_init)
        xr = _merge_ffn(xr, o_dil.reshape(M_ROWS, DIL_OUT), o_diff.reshape(M_ROWS, DIFF_V_WIDTH), gates,
                        wa, wb, wo, ffn2_norm[l].reshape(1, D_MODEL), f2_in, f2_out, l)
    return xr.reshape(BATCH, SEQ, D_MODEL)
```

```python
import functools
import math

import jax
import jax.numpy as jnp
import numpy as np
from jax import lax
from jax.experimental import pallas as pl
from jax.experimental.pallas import tpu as pltpu

F32 = jnp.float32
BF16 = jnp.bfloat16

D_MODEL = 1024
BATCH = 8
SEQ = 2048
DEPTH = 2
HEAD_DIM = 64
DIL_CONFIGS = ((128, 1), (512, 4), (2048, 16))
N_DIL_GROUPS = 3
HEADS_PER_GROUP = 4
N_DIL_HEADS = 12
N_DIFF_HEADS = 4
DIFF_V_DIM = 128
N_ALIBI_HEADS = 16
D_FF = 2816
BLOCK = 128
EPS = 1e-6
DIL_WIDTH = 768
DIL_OUT = 256
DIFF_QK_WIDTH = 512
DIFF_V_WIDTH = 512
IN_COLS = 5888
NEG = -1e30

LANES = 128
VMEM_LIMIT_BYTES = 56 * 1024 * 1024

M_ROWS = BATCH * SEQ
SEG = 256
N_SEG = IN_COLS // SEG
FFN_TM = 512
FFN_FC = 256
PROJ_TM = 512
CAST_ROWS = 256
CAST_ROWS_OUT = D_FF // 2
DIL_JS = 4
DIFF_TQ = 512
DIFF_TK = 512
LOG2E = math.log2(math.e)

SLOPES = tuple(float(np.float32(2.0) ** np.float32(-8.0 * i / N_ALIBI_HEADS)) for i in range(1, N_ALIBI_HEADS + 1))


def _rms_rows(x, gain):
    ms = jnp.mean(x * x, axis=-1, keepdims=True)
    return x * lax.rsqrt(ms + EPS) * gain


def _ffn_rows(x, g_ref, win_ref, wout_ref, a_scr):
    h = _rms_rows(x, g_ref[...]).astype(BF16)
    for c in range(D_FF // FFN_FC):
        lo = c * FFN_FC
        gate = jnp.dot(h, win_ref[:, lo:lo + FFN_FC], preferred_element_type=F32)
        up = jnp.dot(h, win_ref[:, D_FF + lo:D_FF + lo + FFN_FC], preferred_element_type=F32)
        a_scr[:, lo:lo + FFN_FC] = (gate * jax.nn.sigmoid(gate) * up).astype(BF16)
    y = jnp.dot(a_scr[...], wout_ref[...], preferred_element_type=F32)
    return x + 0.5 * y


def _ffn_kernel(x_ref, g_ref, win_ref, wout_ref, o_ref, a_scr):
    o_ref[...] = _ffn_rows(x_ref[...], g_ref, win_ref, wout_ref, a_scr)


def _merge_ffn_kernel(x_ref, od_ref, of_ref, gate_ref, wa_ref, wb_ref, wo_ref, g_ref, win_ref, wout_ref,
                      o_ref, a_scr):
    ya = jnp.dot(od_ref[...], wa_ref[...], preferred_element_type=F32)
    yb = jnp.dot(of_ref[...], wb_ref[...], preferred_element_type=F32)
    y = gate_ref[:, 0:D_MODEL].astype(F32) * ya + gate_ref[:, D_MODEL:2 * D_MODEL].astype(F32) * yb
    x = x_ref[...] + jnp.dot(y.astype(BF16), wo_ref[...], preferred_element_type=F32)
    o_ref[...] = _ffn_rows(x, g_ref, win_ref, wout_ref, a_scr)


def _resident(shape):
    return pl.BlockSpec(shape, lambda i: (0,) * len(shape), pipeline_mode=pl.Buffered(1))


def _resident_layer(shape, layer):
    return pl.BlockSpec((None,) + shape, lambda i: (layer,) + (0,) * len(shape), pipeline_mode=pl.Buffered(1))


def _ffn(x, gain, w_in, w_out, layer):
    tm = FFN_TM
    row = lambda i: (i, 0)
    return pl.pallas_call(
        _ffn_kernel,
        out_shape=jax.ShapeDtypeStruct((M_ROWS, D_MODEL), F32),
        grid=(M_ROWS // tm,),
        in_specs=[
            pl.BlockSpec((tm, D_MODEL), row),
            _resident((1, D_MODEL)),
            _resident_layer((D_MODEL, 2 * D_FF), layer),
            _resident_layer((D_FF, D_MODEL), layer),
        ],
        out_specs=pl.BlockSpec((tm, D_MODEL), row),
        scratch_shapes=[pltpu.VMEM((tm, D_FF), BF16)],
        compiler_params=pltpu.CompilerParams(
            dimension_semantics=("arbitrary",), vmem_limit_bytes=VMEM_LIMIT_BYTES),
        name="ffn",
    )(x, gain, w_in, w_out)


def _merge_ffn(x, o_dil, o_diff, gates, wa, wb, wo, gain, w_in, w_out, layer):
    tm = FFN_TM
    row = lambda i: (i, 0)
    return pl.pallas_call(
        _merge_ffn_kernel,
        out_shape=jax.ShapeDtypeStruct((M_ROWS, D_MODEL), F32),
        grid=(M_ROWS // tm,),
        in_specs=[
            pl.BlockSpec((tm, D_MODEL), row),
            pl.BlockSpec((tm, DIL_OUT), row),
            pl.BlockSpec((tm, DIFF_V_WIDTH), row),
            pl.BlockSpec((tm, 2 * D_MODEL), row),
            _resident_layer((DIL_OUT, D_MODEL), layer),
            _resident_layer((DIFF_V_WIDTH, D_MODEL), layer),
            _resident_layer((D_MODEL, D_MODEL), layer),
            _resident((1, D_MODEL)),
            _resident_layer((D_MODEL, 2 * D_FF), layer),
            _resident_layer((D_FF, D_MODEL), layer),
        ],
        out_specs=pl.BlockSpec((tm, D_MODEL), row),
        scratch_shapes=[pltpu.VMEM((tm, D_FF), BF16)],
        compiler_params=pltpu.CompilerParams(
            dimension_semantics=("arbitrary",), vmem_limit_bytes=VMEM_LIMIT_BYTES),
        name="merge_ffn",
    )(x, o_dil, o_diff, gates, wa, wb, wo, gain, w_in, w_out)


def _cast_kernel(w_ref, o_ref):
    o_ref[...] = w_ref[...].astype(BF16)


def _cast_bf16(w, block_rows):
    rows, cols = w.shape
    return pl.pallas_call(
        _cast_kernel,
        out_shape=jax.ShapeDtypeStruct((rows, cols), BF16),
        grid=(rows // block_rows,),
        in_specs=[pl.BlockSpec((block_rows, cols), lambda i: (i, 0))],
        out_specs=pl.BlockSpec((block_rows, cols), lambda i: (i, 0)),
        compiler_params=pltpu.CompilerParams(
            dimension_semantics=("arbitrary",), vmem_limit_bytes=VMEM_LIMIT_BYTES),
        name="cast_bf16",
    )(w)


def _w_in_src_seg(j):
    return jnp.where(j < 3 * N_DIL_GROUPS, (j % 3) * N_DIL_GROUPS + j // 3, j)


def _cast_permute_w_in(w):
    rows = w.shape[0]
    return pl.pallas_call(
        _cast_kernel,
        out_shape=jax.ShapeDtypeStruct((rows, IN_COLS), BF16),
        grid=(N_SEG,),
        in_specs=[pl.BlockSpec((rows, SEG), lambda j: (0, _w_in_src_seg(j)))],
        out_specs=pl.BlockSpec((rows, SEG), lambda j: (0, j)),
        compiler_params=pltpu.CompilerParams(
            dimension_semantics=("arbitrary",), vmem_limit_bytes=VMEM_LIMIT_BYTES),
        name="cast_permute_w_in",
    )(w)


def _proj_kernel(x_ref, g_ref, w_ref, gain_ref,
                 o_g0, o_g1, o_g2, o_qd, o_kd, o_vd, o_gate, scr1, scr2):
    tm = PROJ_TM
    x = x_ref[...]
    h = _rms_rows(x, g_ref[...]).astype(BF16)
    lane_lo = lax.broadcasted_iota(jnp.int32, (1, LANES), 1) < HEAD_DIM

    def seg(i):
        return jnp.dot(h, w_ref[:, i * SEG:(i + 1) * SEG], preferred_element_type=F32)

    def qk_norm(y, gi):
        tiles = []
        for u in range(SEG // LANES):
            yt = y[:, u * LANES:(u + 1) * LANES]
            sq = yt * yt
            tot = jnp.sum(sq, axis=-1, keepdims=True)
            low = jnp.sum(jnp.where(lane_lo, sq, 0.0), axis=-1, keepdims=True)
            ssq = jnp.where(lane_lo, low, tot - low)
            tiles.append(yt * lax.rsqrt(ssq + HEAD_DIM * EPS))
        return jnp.concatenate(tiles, axis=-1) * gain_ref[gi:gi + 1, :]

    o_g0[:, 0:SEG] = qk_norm(seg(0), 0).astype(BF16)
    o_g0[:, SEG:2 * SEG] = qk_norm(seg(1), 1).astype(BF16)
    o_g0[:, 2 * SEG:3 * SEG] = seg(2).astype(BF16)
    for g, (scr, o_g) in ((1, (scr1, o_g1)), (2, (scr2, o_g2))):
        d = DIL_CONFIGS[g][1]
        ys = (qk_norm(seg(3 * g), 2 * g), qk_norm(seg(3 * g + 1), 2 * g + 1), seg(3 * g + 2))
        for t, y in enumerate(ys):
            for u in range(SEG // LANES):
                scr[2 * t + u] = y[:, u * LANES:(u + 1) * LANES]
        for c in range(d):
            for t in range(DIL_WIDTH // LANES):
                o_g[0, c, :, t * LANES:(t + 1) * LANES] = scr[t, pl.ds(c, tm // d, stride=d), :].astype(BF16)
    for t in range(2):
        o_qd[:, t * SEG:(t + 1) * SEG] = qk_norm(seg(9 + t), 6 + t).astype(BF16)
        o_kd[:, t * SEG:(t + 1) * SEG] = qk_norm(seg(11 + t), 8 + t).astype(BF16)
        o_vd[:, t * SEG:(t + 1) * SEG] = seg(13 + t).astype(BF16)
    for t in range(8):
        o_gate[:, t * SEG:(t + 1) * SEG] = jax.nn.sigmoid(seg(15 + t)).astype(BF16)


def _proj(x, gain, w, qk_gain, layer):
    tm = PROJ_TM
    tiles_per_seq = SEQ // tm
    const = lambda i: (0, 0)
    row = lambda i: (i, 0)
    perm = lambda i: (i // tiles_per_seq, 0, i % tiles_per_seq, 0)
    out_shape = (
        jax.ShapeDtypeStruct((M_ROWS, DIL_WIDTH), BF16),
        jax.ShapeDtypeStruct((BATCH, 4, SEQ // 4, DIL_WIDTH), BF16),
        jax.ShapeDtypeStruct((BATCH, 16, SEQ // 16, DIL_WIDTH), BF16),
        jax.ShapeDtypeStruct((M_ROWS, DIFF_QK_WIDTH), BF16),
        jax.ShapeDtypeStruct((M_ROWS, DIFF_QK_WIDTH), BF16),
        jax.ShapeDtypeStruct((M_ROWS, DIFF_V_WIDTH), BF16),
        jax.ShapeDtypeStruct((M_ROWS, 2 * D_MODEL), BF16),
    )
    out_specs = (
        pl.BlockSpec((tm, DIL_WIDTH), row),
        pl.BlockSpec((1, 4, tm // 4, DIL_WIDTH), perm),
        pl.BlockSpec((1, 16, tm // 16, DIL_WIDTH), perm),
        pl.BlockSpec((tm, DIFF_QK_WIDTH), row),
        pl.BlockSpec((tm, DIFF_QK_WIDTH), row),
        pl.BlockSpec((tm, DIFF_V_WIDTH), row),
        pl.BlockSpec((tm, 2 * D_MODEL), row),
    )
    return pl.pallas_call(
        _proj_kernel,
        out_shape=out_shape,
        grid=(M_ROWS // tm,),
        in_specs=[
            pl.BlockSpec((tm, D_MODEL), row),
            pl.BlockSpec((1, D_MODEL), const),
            _resident_layer((D_MODEL, IN_COLS), layer),
            pl.BlockSpec((16, SEG), const),
        ],
        out_specs=out_specs,
        scratch_shapes=[pltpu.VMEM((DIL_WIDTH // LANES, tm, LANES), F32),
                        pltpu.VMEM((DIL_WIDTH // LANES, tm, LANES), F32)],
        compiler_params=pltpu.CompilerParams(
            dimension_semantics=("arbitrary",), vmem_limit_bytes=VMEM_LIMIT_BYTES),
        name="proj",
    )(x, gain, w, qk_gain)


def _dil_block(ref, row_cur, row_prev, bias, lane_head, lane_lo):
    q = ref[0, pl.ds(row_cur, BLOCK), 0:SEG]
    k = ref[0, pl.ds(row_cur, BLOCK), SEG:2 * SEG]
    v = ref[0, pl.ds(row_cur, BLOCK), 2 * SEG:3 * SEG]
    if row_prev is not None:
        k = jnp.concatenate([ref[0, pl.ds(row_prev, BLOCK), SEG:2 * SEG], k], axis=0)
        v = jnp.concatenate([ref[0, pl.ds(row_prev, BLOCK), 2 * SEG:3 * SEG], v], axis=0)
    zero = jnp.zeros_like(q)
    qs = jnp.concatenate([jnp.where(lane_head == j, q, zero) for j in range(HEADS_PER_GROUP)], axis=0)
    s = lax.dot_general(qs, k, (((1,), (1,)), ((), ())), preferred_element_type=F32) + bias
    m = jnp.max(s, axis=-1, keepdims=True)
    p = jnp.exp2(s - m)
    l = jnp.sum(p, axis=-1, keepdims=True)
    pv = jnp.dot(p.astype(BF16), v, preferred_element_type=F32)
    inv = 1.0 / l
    lse_rows = m + jnp.log2(l)
    outs, lses = [], []
    for u in range(SEG // LANES):
        ra, rb = slice(2 * u * BLOCK, (2 * u + 1) * BLOCK), slice((2 * u + 1) * BLOCK, (2 * u + 2) * BLOCK)
        cols = slice(u * LANES, (u + 1) * LANES)
        outs.append(jnp.where(lane_lo, pv[ra, cols], pv[rb, cols]) * jnp.where(lane_lo, inv[ra], inv[rb]))
        lses.append(jnp.where(lane_lo, lse_rows[ra], lse_rows[rb]))
    return outs, lses


def _dil_kernel(g0_ref, g1_ref, g2_ref, bias_ref, o_ref, o_scr, l_scr):
    step = pl.program_id(1)
    lane_head = lax.broadcasted_iota(jnp.int32, (1, SEG), 1) // HEAD_DIM
    lane_lo = lax.broadcasted_iota(jnp.int32, (1, LANES), 1) < HEAD_DIM

    def put(g, rows, outs, lses):
        for u in range(SEG // LANES):
            o_scr[g, u, rows, :] = outs[u]
            l_scr[g, u, rows, :] = lses[u]

    for jj in range(DIL_JS):
        j = step * DIL_JS + jj
        row = pl.multiple_of(j * BLOCK, BLOCK)

        prev = pl.multiple_of(jnp.maximum(j - 1, 0) * BLOCK, BLOCK)
        tab = jnp.where(j == 0, N_DIL_GROUPS, 0) if jj == 0 else 0
        outs, lses = _dil_block(g0_ref, row, prev, bias_ref[tab], lane_head, lane_lo)
        put(0, pl.ds(row, BLOCK), outs, lses)

        c, n = step * (DIL_JS // 4) + jj // 4, jj % 4
        if n == 0:
            outs, lses = _dil_block(g1_ref, row, None, bias_ref[1, :, BLOCK:2 * BLOCK], lane_head, lane_lo)
        else:
            prev = pl.multiple_of(row - BLOCK, BLOCK)
            outs, lses = _dil_block(g1_ref, row, prev, bias_ref[1], lane_head, lane_lo)
        put(1, pl.ds(n * (4 * BLOCK) + c, BLOCK, stride=4), outs, lses)

        outs, lses = _dil_block(g2_ref, row, None, bias_ref[2, :, BLOCK:2 * BLOCK], lane_head, lane_lo)
        put(2, pl.ds(j, BLOCK, stride=16), outs, lses)

    @pl.when(step == pl.num_programs(1) - 1)
    def _():
        rows = 256
        for r in range(SEQ // rows):
            sl = pl.ds(r * rows, rows)
            for u in range(SEG // LANES):
                l0, l1, l2 = l_scr[0, u, sl, :], l_scr[1, u, sl, :], l_scr[2, u, sl, :]
                mx = jnp.maximum(jnp.maximum(l0, l1), l2)
                w0, w1, w2 = jnp.exp2(l0 - mx), jnp.exp2(l1 - mx), jnp.exp2(l2 - mx)
                num = w0 * o_scr[0, u, sl, :] + w1 * o_scr[1, u, sl, :] + w2 * o_scr[2, u, sl, :]
                o_ref[0, sl, u * LANES:(u + 1) * LANES] = (num / (w0 + w1 + w2)).astype(BF16)


def _dil_bias_table():
    qi = np.arange(BLOCK)[:, None]
    ki = np.arange(2 * BLOCK)[None, :]
    dist = BLOCK + qi - ki
    valid = (dist >= 0) & (dist <= BLOCK)
    tab = np.full((N_DIL_GROUPS + 1, HEADS_PER_GROUP * BLOCK, 2 * BLOCK), NEG, np.float32)
    for g, (_, dil) in enumerate(DIL_CONFIGS):
        for h in range(HEADS_PER_GROUP):
            slope = np.float32(SLOPES[g * HEADS_PER_GROUP + h])
            b = np.where(valid, -slope * np.float32(LOG2E) * (dil * dist).astype(np.float32),
                         np.float32(NEG)).astype(np.float32)
            tab[g, h * BLOCK:(h + 1) * BLOCK] = b
            if g == 0:
                tab[N_DIL_GROUPS, h * BLOCK:(h + 1) * BLOCK] = np.where(ki >= BLOCK, b, np.float32(NEG))
    return tab


def _dil_attn(g0, g1, g2, bias):
    n_blk = SEQ // BLOCK
    seq_spec = pl.BlockSpec((1, SEQ, DIL_WIDTH), lambda b, j: (b, 0, 0))
    return pl.pallas_call(
        _dil_kernel,
        out_shape=jax.ShapeDtypeStruct((BATCH, SEQ, DIL_OUT), BF16),
        grid=(BATCH, n_blk // DIL_JS),
        in_specs=[seq_spec, seq_spec, seq_spec,
                  pl.BlockSpec(bias.shape, lambda b, j: (0, 0, 0))],
        out_specs=pl.BlockSpec((1, SEQ, DIL_OUT), lambda b, j: (b, 0, 0)),
        scratch_shapes=[pltpu.VMEM((N_DIL_GROUPS, DIL_OUT // LANES, SEQ, LANES), F32),
                        pltpu.VMEM((N_DIL_GROUPS, DIL_OUT // LANES, SEQ, LANES), F32)],
        compiler_params=pltpu.CompilerParams(
            dimension_semantics=("arbitrary", "arbitrary"), vmem_limit_bytes=VMEM_LIMIT_BYTES),
        name="dil_attn",
    )(g0, g1, g2, bias)


def _diff_kernel(q_ref, k_ref, v_ref, lq_ref, lk_ref, sn_ref, o_ref,
                 s_scr, mp_scr, m_scr, lp_scr, acc_scr, *, lam_init):
    tq, tk = DIFF_TQ, DIFF_TK
    n_lt = tk // LANES
    pair = pl.program_id(1)
    qi = pl.program_id(2)
    q = q_ref[0]
    lane_map = lax.broadcasted_iota(jnp.int32, (1, SEG), 1) // HEAD_DIM
    zero = jnp.zeros_like(q)
    qs = [jnp.where(lane_map == g, q, zero) for g in range(4)]
    slopes = (jnp.where(pair == 0, SLOPES[N_DIL_HEADS] * LOG2E, SLOPES[N_DIL_HEADS + 2] * LOG2E).astype(F32),
              jnp.where(pair == 0, SLOPES[N_DIL_HEADS + 1] * LOG2E, SLOPES[N_DIL_HEADS + 3] * LOG2E).astype(F32))
    lane_k = lax.broadcasted_iota(jnp.int32, (1, tk), 1)

    mp_scr[...] = jnp.full(mp_scr.shape, NEG, F32)
    lp_scr[...] = jnp.zeros(lp_scr.shape, F32)
    acc_scr[...] = jnp.zeros(acc_scr.shape, F32)

    def lane_tiles(x):
        return [x[:, u * LANES:(u + 1) * LANES] for u in range(n_lt)]

    def pass1(kb, masked):
        k0 = pl.multiple_of(kb * tk, tk)
        k = k_ref[0, pl.ds(k0, tk), :]
        kpos = (k0 + lane_k).astype(F32)
        for g in range(4):
            rows = pl.ds(g * tq, tq)
            s = lax.dot_general(qs[g], k, (((1,), (1,)), ((), ())), preferred_element_type=F32)
            s = s + slopes[g // 2] * kpos
            if masked:
                r = lax.broadcasted_iota(jnp.int32, (tq, tk), 0)
                cidx = lax.broadcasted_iota(jnp.int32, (tq, tk), 1)
                s = jnp.where(cidx > r, NEG, s)
            s_scr[kb, rows, :] = s
            t = lane_tiles(s)
            mp_scr[rows, :] = jnp.maximum(mp_scr[rows, :],
                                          jnp.maximum(jnp.maximum(t[0], t[1]), jnp.maximum(t[2], t[3])))

    def body1(kb, carry):
        pass1(kb, False)
        return carry

    lax.fori_loop(0, qi, body1, 0)
    pass1(qi, True)

    m_scr[...] = jnp.broadcast_to(jnp.max(mp_scr[...], axis=-1, keepdims=True), m_scr.shape)

    def body2(kb, carry):
        k0 = pl.multiple_of(kb * tk, tk)
        v = v_ref[0, pl.ds(k0, tk), :]
        for g in range(4):
            rows = pl.ds(g * tq, tq)
            m = m_scr[rows, :]
            p = jnp.exp2(s_scr[kb, rows, :] - jnp.concatenate([m] * n_lt, axis=1))
            t = lane_tiles(p)
            lp_scr[rows, :] += (t[0] + t[1]) + (t[2] + t[3])
            acc_scr[rows, :] += jnp.dot(p.astype(BF16), v, preferred_element_type=F32)
        return carry

    lax.fori_loop(0, qi + 1, body2, 0)

    lq = lq_ref[...]
    lk = lk_ref[...]
    lam = (jnp.exp(jnp.sum(lq[0:1] * lk[0:1], axis=-1, keepdims=True))
           - jnp.exp(jnp.sum(lq[1:2] * lk[1:2], axis=-1, keepdims=True)) + lam_init)
    inv = 1.0 / jnp.sum(lp_scr[...], axis=-1, keepdims=True)
    outs = []
    for hh in range(2):
        cols = slice(hh * DIFF_V_DIM, (hh + 1) * DIFF_V_DIM)
        r1 = slice((2 * hh) * tq, (2 * hh + 1) * tq)
        r2 = slice((2 * hh + 1) * tq, (2 * hh + 2) * tq)
        o = acc_scr[r1, cols] * inv[r1] - lam * (acc_scr[r2, cols] * inv[r2])
        outs.append(_rms_rows(o, sn_ref[0, hh:hh + 1, :]) * (1.0 - lam_init))
    o_ref[0] = jnp.concatenate(outs, axis=-1).astype(BF16)


def _diff_attn(qd, kd, vd, lq, lk, subnorm, lam_init):
    tq = DIFF_TQ
    kv_spec = pl.BlockSpec((1, SEQ, SEG), lambda b, p, i: (b, 0, p))
    return pl.pallas_call(
        functools.partial(_diff_kernel, lam_init=lam_init),
        out_shape=jax.ShapeDtypeStruct((BATCH, SEQ, DIFF_V_WIDTH), BF16),
        grid=(BATCH, 2, SEQ // tq),
        in_specs=[
            pl.BlockSpec((1, tq, SEG), lambda b, p, i: (b, i, p)),
            kv_spec, kv_spec,
            pl.BlockSpec((2, HEAD_DIM), lambda b, p, i: (0, 0)),
            pl.BlockSpec((2, HEAD_DIM), lambda b, p, i: (0, 0)),
            pl.BlockSpec((1, 2, DIFF_V_DIM), lambda b, p, i: (p, 0, 0)),
        ],
        out_specs=pl.BlockSpec((1, tq, SEG), lambda b, p, i: (b, i, p)),
        scratch_shapes=[pltpu.VMEM((SEQ // DIFF_TK, 4 * tq, DIFF_TK), F32),
                        pltpu.VMEM((4 * tq, LANES), F32),
                        pltpu.VMEM((4 * tq, LANES), F32),
                        pltpu.VMEM((4 * tq, LANES), F32),
                        pltpu.VMEM((4 * tq, SEG), F32)],
        compiler_params=pltpu.CompilerParams(
            dimension_semantics=("arbitrary", "arbitrary", "arbitrary"), vmem_limit_bytes=VMEM_LIMIT_BYTES),
        name="diff_attn",
    )(qd, kd, vd, lq, lk, subnorm)


def _qk_gain_rows(qk_gain_dil, qk_gain_diff):
    k_scale = HEAD_DIM ** 0.5
    q_scale = LOG2E
    rows = []
    for g in range(N_DIL_GROUPS):
        rows.append(qk_gain_dil[0, g].reshape(SEG) * q_scale)
        rows.append(qk_gain_dil[1, g].reshape(SEG) * k_scale)
    qd = qk_gain_diff[0].reshape(2, SEG) * q_scale
    kd = qk_gain_diff[1].reshape(2, SEG) * k_scale
    rows += [qd[0], qd[1], kd[0], kd[1]]
    rows += [jnp.zeros((SEG,), F32)] * (16 - len(rows))
    return jnp.stack(rows).astype(F32)


def _cast_stacked(w, block_rows):
    depth, rows, cols = w.shape
    return _cast_bf16(w.reshape(depth * rows, cols), block_rows).reshape(depth, rows, cols)


def kernel(x, ffn1_norm, ffn1_w_in, ffn1_w_out, mix_norm, w_in, qk_gain_dil, qk_gain_diff, lambda_q, lambda_k,
           diff_subnorm, w_branch_dil, w_branch_diff, w_out, ffn2_norm, ffn2_w_in, ffn2_w_out):
    b, s, d = x.shape
    assert (b, s, d) == (BATCH, SEQ, D_MODEL)
    xr = x.reshape(M_ROWS, D_MODEL)
    dil_bias = jnp.asarray(_dil_bias_table())
    f1_in, f2_in = _cast_stacked(ffn1_w_in, CAST_ROWS), _cast_stacked(ffn2_w_in, CAST_ROWS)
    f1_out, f2_out = _cast_stacked(ffn1_w_out, CAST_ROWS_OUT), _cast_stacked(ffn2_w_out, CAST_ROWS_OUT)
    w_in_p = _cast_permute_w_in(w_in.reshape(DEPTH * D_MODEL, IN_COLS)).reshape(DEPTH, D_MODEL, IN_COLS)
    wa, wb, wo = w_branch_dil.astype(BF16), w_branch_diff.astype(BF16), w_out.astype(BF16)
    for l in range(DEPTH):
        lam_init = 0.8 - 0.6 * math.exp(-0.3 * l)
        xr = _ffn(xr, ffn1_norm[l].reshape(1, D_MODEL), f1_in, f1_out, l)
        g0, g1, g2, qd, kd, vd, gates = _proj(
            xr, mix_norm[l].reshape(1, D_MODEL), w_in_p, _qk_gain_rows(qk_gain_dil[l], qk_gain_diff[l]), l)
        o_dil = _dil_attn(g0.reshape(BATCH, SEQ, DIL_WIDTH), g1.reshape(BATCH, SEQ, DIL_WIDTH),
                          g2.reshape(BATCH, SEQ, DIL_WIDTH), dil_bias)
        o_diff = _diff_attn(qd.reshape(BATCH, SEQ, DIFF_QK_WIDTH), kd.reshape(BATCH, SEQ, DIFF_QK_WIDTH),
                            vd.reshape(BATCH, SEQ, DIFF_V_WIDTH), lambda_q[l], lambda_k[l],
                            diff_subnorm[l].reshape(2, 2, DIFF_V_DIM), lam_init)
        xr = _merge_ffn(xr, o_dil.reshape(M_ROWS, DIL_OUT), o_diff.reshape(M_ROWS, DIFF_V_WIDTH), gates,
                        wa, wb, wo, ffn2_norm[l].reshape(1, D_MODEL), f2_in, f2_out, l)
    return xr.reshape(BATCH, SEQ, D_MODEL)
```

```python
import functools
import math

import jax
import jax.numpy as jnp
import numpy as np
from jax import lax
from jax.experimental import pallas as pl
from jax.experimental.pallas import tpu as pltpu

F32 = jnp.float32
BF16 = jnp.bfloat16

D_MODEL = 1024
BATCH = 8
SEQ = 2048
DEPTH = 2
HEAD_DIM = 64
DIL_CONFIGS = ((128, 1), (512, 4), (2048, 16))
N_DIL_GROUPS = 3
HEADS_PER_GROUP = 4
N_DIL_HEADS = 12
N_DIFF_HEADS = 4
DIFF_V_DIM = 128
N_ALIBI_HEADS = 16
D_FF = 2816
BLOCK = 128
EPS = 1e-6
DIL_WIDTH = 768
DIL_OUT = 256
DIFF_QK_WIDTH = 512
DIFF_V_WIDTH = 512
IN_COLS = 5888
NEG = -1e30

LANES = 128
VMEM_LIMIT_BYTES = 56 * 1024 * 1024

M_ROWS = BATCH * SEQ
SEG = 256
N_SEG = IN_COLS // SEG
FFN_TM = 512
FFN_FC = 256
PROJ_TM = 512
CAST_ROWS = 256
CAST_ROWS_OUT = D_FF // 2
DIL_JS = 4
DIFF_T = 512
LOG2E = math.log2(math.e)

SLOPES = tuple(float(np.float32(2.0) ** np.float32(-8.0 * i / N_ALIBI_HEADS)) for i in range(1, N_ALIBI_HEADS + 1))


def _rms_rows(x, gain):
    ms = jnp.mean(x * x, axis=-1, keepdims=True)
    return x * lax.rsqrt(ms + EPS) * gain


def _ffn_rows(x, g_ref, win_ref, wout_ref, a_scr):
    h = _rms_rows(x, g_ref[...]).astype(BF16)
    for c in range(D_FF // FFN_FC):
        lo = c * FFN_FC
        gate = jnp.dot(h, win_ref[:, lo:lo + FFN_FC], preferred_element_type=F32)
        up = jnp.dot(h, win_ref[:, D_FF + lo:D_FF + lo + FFN_FC], preferred_element_type=F32)
        a_scr[:, lo:lo + FFN_FC] = (gate * jax.nn.sigmoid(gate) * up).astype(BF16)
    y = jnp.dot(a_scr[...], wout_ref[...], preferred_element_type=F32)
    return x + 0.5 * y


def _ffn_kernel(x_ref, g_ref, win_ref, wout_ref, o_ref, a_scr):
    o_ref[...] = _ffn_rows(x_ref[...], g_ref, win_ref, wout_ref, a_scr)


def _merge_ffn_kernel(x_ref, od_ref, of_ref, gate_ref, wa_ref, wb_ref, wo_ref, g_ref, win_ref, wout_ref,
                      o_ref, a_scr):
    ya = jnp.dot(od_ref[...], wa_ref[...], preferred_element_type=F32)
    yb = jnp.dot(of_ref[...], wb_ref[...], preferred_element_type=F32)
    y = gate_ref[:, 0:D_MODEL].astype(F32) * ya + gate_ref[:, D_MODEL:2 * D_MODEL].astype(F32) * yb
    x = x_ref[...] + jnp.dot(y.astype(BF16), wo_ref[...], preferred_element_type=F32)
    o_ref[...] = _ffn_rows(x, g_ref, win_ref, wout_ref, a_scr)


def _resident(shape):
    return pl.BlockSpec(shape, lambda i: (0,) * len(shape), pipeline_mode=pl.Buffered(1))


def _resident_layer(shape, layer):
    return pl.BlockSpec((None,) + shape, lambda i: (layer,) + (0,) * len(shape), pipeline_mode=pl.Buffered(1))


def _ffn(x, gain, w_in, w_out, layer):
    tm = FFN_TM
    row = lambda i: (i, 0)
    return pl.pallas_call(
        _ffn_kernel,
        out_shape=jax.ShapeDtypeStruct((M_ROWS, D_MODEL), F32),
        grid=(M_ROWS // tm,),
        in_specs=[
            pl.BlockSpec((tm, D_MODEL), row),
            _resident((1, D_MODEL)),
            _resident_layer((D_MODEL, 2 * D_FF), layer),
            _resident_layer((D_FF, D_MODEL), layer),
        ],
        out_specs=pl.BlockSpec((tm, D_MODEL), row),
        scratch_shapes=[pltpu.VMEM((tm, D_FF), BF16)],
        compiler_params=pltpu.CompilerParams(
            dimension_semantics=("arbitrary",), vmem_limit_bytes=VMEM_LIMIT_BYTES),
        name="ffn",
    )(x, gain, w_in, w_out)


def _merge_ffn(x, o_dil, o_diff, gates, wa, wb, wo, gain, w_in, w_out, layer):
    tm = FFN_TM
    row = lambda i: (i, 0)
    return pl.pallas_call(
        _merge_ffn_kernel,
        out_shape=jax.ShapeDtypeStruct((M_ROWS, D_MODEL), F32),
        grid=(M_ROWS // tm,),
        in_specs=[
            pl.BlockSpec((tm, D_MODEL), row),
            pl.BlockSpec((tm, DIL_OUT), row),
            pl.BlockSpec((tm, DIFF_V_WIDTH), row),
            pl.BlockSpec((tm, 2 * D_MODEL), row),
            _resident_layer((DIL_OUT, D_MODEL), layer),
            _resident_layer((DIFF_V_WIDTH, D_MODEL), layer),
            _resident_layer((D_MODEL, D_MODEL), layer),
            _resident((1, D_MODEL)),
            _resident_layer((D_MODEL, 2 * D_FF), layer),
            _resident_layer((D_FF, D_MODEL), layer),
        ],
        out_specs=pl.BlockSpec((tm, D_MODEL), row),
        scratch_shapes=[pltpu.VMEM((tm, D_FF), BF16)],
        compiler_params=pltpu.CompilerParams(
            dimension_semantics=("arbitrary",), vmem_limit_bytes=VMEM_LIMIT_BYTES),
        name="merge_ffn",
    )(x, o_dil, o_diff, gates, wa, wb, wo, gain, w_in, w_out)


def _cast_kernel(w_ref, o_ref):
    o_ref[...] = w_ref[...].astype(BF16)


def _cast_bf16(w, block_rows):
    rows, cols = w.shape
    return pl.pallas_call(
        _cast_kernel,
        out_shape=jax.ShapeDtypeStruct((rows, cols), BF16),
        grid=(rows // block_rows,),
        in_specs=[pl.BlockSpec((block_rows, cols), lambda i: (i, 0))],
        out_specs=pl.BlockSpec((block_rows, cols), lambda i: (i, 0)),
        compiler_params=pltpu.CompilerParams(
            dimension_semantics=("arbitrary",), vmem_limit_bytes=VMEM_LIMIT_BYTES),
        name="cast_bf16",
    )(w)


def _w_in_src_seg(j):
    return jnp.where(j < 3 * N_DIL_GROUPS, (j % 3) * N_DIL_GROUPS + j // 3, j)


def _cast_permute_w_in(w):
    rows = w.shape[0]
    return pl.pallas_call(
        _cast_kernel,
        out_shape=jax.ShapeDtypeStruct((rows, IN_COLS), BF16),
        grid=(N_SEG,),
        in_specs=[pl.BlockSpec((rows, SEG), lambda j: (0, _w_in_src_seg(j)))],
        out_specs=pl.BlockSpec((rows, SEG), lambda j: (0, j)),
        compiler_params=pltpu.CompilerParams(
            dimension_semantics=("arbitrary",), vmem_limit_bytes=VMEM_LIMIT_BYTES),
        name="cast_permute_w_in",
    )(w)


def _proj_kernel(x_ref, g_ref, w_ref, gain_ref,
                 o_g0, o_g1, o_g2, o_qd, o_kd, o_vd, o_gate, scr1, scr2):
    tm = PROJ_TM
    x = x_ref[...]
    h = _rms_rows(x, g_ref[...]).astype(BF16)
    lane_lo = lax.broadcasted_iota(jnp.int32, (1, LANES), 1) < HEAD_DIM

    def seg(i):
        return jnp.dot(h, w_ref[:, i * SEG:(i + 1) * SEG], preferred_element_type=F32)

    def qk_norm(y, gi):
        tiles = []
        for u in range(SEG // LANES):
            yt = y[:, u * LANES:(u + 1) * LANES]
            sq = yt * yt
            low = jnp.sum(jnp.where(lane_lo, sq, 0.0), axis=-1, keepdims=True)
            high = jnp.sum(jnp.where(lane_lo, 0.0, sq), axis=-1, keepdims=True)
            ssq = jnp.where(lane_lo, low, high)
            tiles.append(yt * lax.rsqrt(ssq + HEAD_DIM * EPS))
        return jnp.concatenate(tiles, axis=-1) * gain_ref[gi:gi + 1, :]

    o_g0[:, 0:SEG] = qk_norm(seg(0), 0).astype(BF16)
    o_g0[:, SEG:2 * SEG] = qk_norm(seg(1), 1).astype(BF16)
    o_g0[:, 2 * SEG:3 * SEG] = seg(2).astype(BF16)
    for g, (scr, o_g) in ((1, (scr1, o_g1)), (2, (scr2, o_g2))):
        d = DIL_CONFIGS[g][1]
        ys = (qk_norm(seg(3 * g), 2 * g), qk_norm(seg(3 * g + 1), 2 * g + 1), seg(3 * g + 2))
        for t, y in enumerate(ys):
            for u in range(SEG // LANES):
                scr[2 * t + u] = y[:, u * LANES:(u + 1) * LANES]
        for c in range(d):
            for t in range(DIL_WIDTH // LANES):
                o_g[0, c, :, t * LANES:(t + 1) * LANES] = scr[t, pl.ds(c, tm // d, stride=d), :].astype(BF16)
    for t in range(2):
        o_qd[:, t * SEG:(t + 1) * SEG] = qk_norm(seg(9 + t), 6 + t).astype(BF16)
        o_kd[:, t * SEG:(t + 1) * SEG] = qk_norm(seg(11 + t), 8 + t).astype(BF16)
        o_vd[:, t * SEG:(t + 1) * SEG] = seg(13 + t).astype(BF16)
    for t in range(8):
        o_gate[:, t * SEG:(t + 1) * SEG] = jax.nn.sigmoid(seg(15 + t)).astype(BF16)


def _proj(x, gain, w, qk_gain, layer):
    tm = PROJ_TM
    tiles_per_seq = SEQ // tm
    const = lambda i: (0, 0)
    row = lambda i: (i, 0)
    perm = lambda i: (i // tiles_per_seq, 0, i % tiles_per_seq, 0)
    out_shape = (
        jax.ShapeDtypeStruct((M_ROWS, DIL_WIDTH), BF16),
        jax.ShapeDtypeStruct((BATCH, 4, SEQ // 4, DIL_WIDTH), BF16),
        jax.ShapeDtypeStruct((BATCH, 16, SEQ // 16, DIL_WIDTH), BF16),
        jax.ShapeDtypeStruct((M_ROWS, DIFF_QK_WIDTH), BF16),
        jax.ShapeDtypeStruct((M_ROWS, DIFF_QK_WIDTH), BF16),
        jax.ShapeDtypeStruct((M_ROWS, DIFF_V_WIDTH), BF16),
        jax.ShapeDtypeStruct((M_ROWS, 2 * D_MODEL), BF16),
    )
    out_specs = (
        pl.BlockSpec((tm, DIL_WIDTH), row),
        pl.BlockSpec((1, 4, tm // 4, DIL_WIDTH), perm),
        pl.BlockSpec((1, 16, tm // 16, DIL_WIDTH), perm),
        pl.BlockSpec((tm, DIFF_QK_WIDTH), row),
        pl.BlockSpec((tm, DIFF_QK_WIDTH), row),
        pl.BlockSpec((tm, DIFF_V_WIDTH), row),
        pl.BlockSpec((tm, 2 * D_MODEL), row),
    )
    return pl.pallas_call(
        _proj_kernel,
        out_shape=out_shape,
        grid=(M_ROWS // tm,),
        in_specs=[
            pl.BlockSpec((tm, D_MODEL), row),
            pl.BlockSpec((1, D_MODEL), const),
            _resident_layer((D_MODEL, IN_COLS), layer),
            pl.BlockSpec((16, SEG), const),
        ],
        out_specs=out_specs,
        scratch_shapes=[pltpu.VMEM((DIL_WIDTH // LANES, tm, LANES), F32),
                        pltpu.VMEM((DIL_WIDTH // LANES, tm, LANES), F32)],
        compiler_params=pltpu.CompilerParams(
            dimension_semantics=("arbitrary",), vmem_limit_bytes=VMEM_LIMIT_BYTES),
        name="proj",
    )(x, gain, w, qk_gain)


def _dil_block(ref, row_cur, row_prev, bias, lane_head, lane_lo):
    q = ref[0, pl.ds(row_cur, BLOCK), 0:SEG]
    k = ref[0, pl.ds(row_cur, BLOCK), SEG:2 * SEG]
    v = ref[0, pl.ds(row_cur, BLOCK), 2 * SEG:3 * SEG]
    if row_prev is not None:
        k = jnp.concatenate([ref[0, pl.ds(row_prev, BLOCK), SEG:2 * SEG], k], axis=0)
        v = jnp.concatenate([ref[0, pl.ds(row_prev, BLOCK), 2 * SEG:3 * SEG], v], axis=0)
    zero = jnp.zeros_like(q)
    qs = jnp.concatenate([jnp.where(lane_head == j, q, zero) for j in range(HEADS_PER_GROUP)], axis=0)
    s = lax.dot_general(qs, k, (((1,), (1,)), ((), ())), preferred_element_type=F32) + bias
    m = jnp.max(s, axis=-1, keepdims=True)
    p = jnp.exp2(s - m)
    l = jnp.sum(p, axis=-1, keepdims=True)
    pv = jnp.dot(p.astype(BF16), v, preferred_element_type=F32)
    inv = 1.0 / l
    lse_rows = m + jnp.log2(l)
    outs, lses = [], []
    for u in range(SEG // LANES):
        ra, rb = slice(2 * u * BLOCK, (2 * u + 1) * BLOCK), slice((2 * u + 1) * BLOCK, (2 * u + 2) * BLOCK)
        cols = slice(u * LANES, (u + 1) * LANES)
        outs.append(jnp.where(lane_lo, pv[ra, cols], pv[rb, cols]) * jnp.where(lane_lo, inv[ra], inv[rb]))
        lses.append(jnp.where(lane_lo, lse_rows[ra], lse_rows[rb]))
    return outs, lses


def _dil_kernel(g0_ref, g1_ref, g2_ref, bias_ref, o_ref, o_scr, l_scr):
    step = pl.program_id(1)
    lane_head = lax.broadcasted_iota(jnp.int32, (1, SEG), 1) // HEAD_DIM
    lane_lo = lax.broadcasted_iota(jnp.int32, (1, LANES), 1) < HEAD_DIM

    def put(g, rows, outs, lses):
        for u in range(SEG // LANES):
            o_scr[g, u, rows, :] = outs[u]
            l_scr[g, u, rows, :] = lses[u]

    for jj in range(DIL_JS):
        j = step * DIL_JS + jj
        row = pl.multiple_of(j * BLOCK, BLOCK)

        prev = pl.multiple_of(jnp.maximum(j - 1, 0) * BLOCK, BLOCK)
        tab = jnp.where(j == 0, N_DIL_GROUPS, 0) if jj == 0 else 0
        outs, lses = _dil_block(g0_ref, row, prev, bias_ref[tab], lane_head, lane_lo)
        put(0, pl.ds(row, BLOCK), outs, lses)

        c, n = step * (DIL_JS // 4) + jj // 4, jj % 4
        if n == 0:
            outs, lses = _dil_block(g1_ref, row, None, bias_ref[1, :, BLOCK:2 * BLOCK], lane_head, lane_lo)
        else:
            prev = pl.multiple_of(row - BLOCK, BLOCK)
            outs, lses = _dil_block(g1_ref, row, prev, bias_ref[1], lane_head, lane_lo)
        put(1, pl.ds(n * (4 * BLOCK) + c, BLOCK, stride=4), outs, lses)

        outs, lses = _dil_block(g2_ref, row, None, bias_ref[2, :, BLOCK:2 * BLOCK], lane_head, lane_lo)
        put(2, pl.ds(j, BLOCK, stride=16), outs, lses)

    @pl.when(step == pl.num_programs(1) - 1)
    def _():
        rows = 256
        for r in range(SEQ // rows):
            sl = pl.ds(r * rows, rows)
            for u in range(SEG // LANES):
                l0, l1, l2 = l_scr[0, u, sl, :], l_scr[1, u, sl, :], l_scr[2, u, sl, :]
                mx = jnp.maximum(jnp.maximum(l0, l1), l2)
                w0, w1, w2 = jnp.exp2(l0 - mx), jnp.exp2(l1 - mx), jnp.exp2(l2 - mx)
                num = w0 * o_scr[0, u, sl, :] + w1 * o_scr[1, u, sl, :] + w2 * o_scr[2, u, sl, :]
                o_ref[0, sl, u * LANES:(u + 1) * LANES] = (num / (w0 + w1 + w2)).astype(BF16)


def _dil_bias_table():
    qi = np.arange(BLOCK)[:, None]
    ki = np.arange(2 * BLOCK)[None, :]
    dist = BLOCK + qi - ki
    valid = (dist >= 0) & (dist <= BLOCK)
    tab = np.full((N_DIL_GROUPS + 1, HEADS_PER_GROUP * BLOCK, 2 * BLOCK), NEG, np.float32)
    for g, (_, dil) in enumerate(DIL_CONFIGS):
        for h in range(HEADS_PER_GROUP):
            slope = np.float32(SLOPES[g * HEADS_PER_GROUP + h])
            b = np.where(valid, -slope * np.float32(LOG2E) * (dil * dist).astype(np.float32),
                         np.float32(NEG)).astype(np.float32)
            tab[g, h * BLOCK:(h + 1) * BLOCK] = b
            if g == 0:
                tab[N_DIL_GROUPS, h * BLOCK:(h + 1) * BLOCK] = np.where(ki >= BLOCK, b, np.float32(NEG))
    return tab


def _dil_attn(g0, g1, g2, bias):
    n_blk = SEQ // BLOCK
    seq_spec = pl.BlockSpec((1, SEQ, DIL_WIDTH), lambda b, j: (b, 0, 0))
    return pl.pallas_call(
        _dil_kernel,
        out_shape=jax.ShapeDtypeStruct((BATCH, SEQ, DIL_OUT), BF16),
        grid=(BATCH, n_blk // DIL_JS),
        in_specs=[seq_spec, seq_spec, seq_spec,
                  pl.BlockSpec(bias.shape, lambda b, j: (0, 0, 0))],
        out_specs=pl.BlockSpec((1, SEQ, DIL_OUT), lambda b, j: (b, 0, 0)),
        scratch_shapes=[pltpu.VMEM((N_DIL_GROUPS, DIL_OUT // LANES, SEQ, LANES), F32),
                        pltpu.VMEM((N_DIL_GROUPS, DIL_OUT // LANES, SEQ, LANES), F32)],
        compiler_params=pltpu.CompilerParams(
            dimension_semantics=("arbitrary", "arbitrary"), vmem_limit_bytes=VMEM_LIMIT_BYTES),
        name="dil_attn",
    )(g0, g1, g2, bias)


def _diff_kernel(q_ref, k_ref, v_ref, lq_ref, lk_ref, sn_ref, o_ref,
                 s_scr, mp_scr, m_scr, acc_scr, *, lam_init):
    t = DIFF_T
    n_lt = t // LANES
    pair = pl.program_id(1)
    qi = pl.program_id(2)
    q = q_ref[0]
    lane_map = lax.broadcasted_iota(jnp.int32, (1, SEG), 1) // HEAD_DIM
    zero = jnp.zeros_like(q)
    qs = [jnp.where(lane_map == g, q, zero) for g in range(4)]
    slopes = (jnp.where(pair == 0, SLOPES[N_DIL_HEADS] * LOG2E, SLOPES[N_DIL_HEADS + 2] * LOG2E).astype(F32),
              jnp.where(pair == 0, SLOPES[N_DIL_HEADS + 1] * LOG2E, SLOPES[N_DIL_HEADS + 3] * LOG2E).astype(F32))
    lane_k = lax.broadcasted_iota(jnp.int32, (1, t), 1)
    lane_v0 = lax.broadcasted_iota(jnp.int32, (1, SEG), 1) < DIFF_V_DIM

    def lane_tiles(x):
        return [x[:, u * LANES:(u + 1) * LANES] for u in range(n_lt)]

    def pass1(kb, diag, init):
        k = k_ref[0, kb * t:(kb + 1) * t, :]
        kpos = (kb * t + lane_k).astype(F32)
        for g in range(4):
            rows = pl.ds(g * t, t)
            s = lax.dot_general(qs[g], k, (((1,), (1,)), ((), ())), preferred_element_type=F32)
            s = s + slopes[g // 2] * kpos
            if diag:
                r = lax.broadcasted_iota(jnp.int32, (t, t), 0)
                cidx = lax.broadcasted_iota(jnp.int32, (t, t), 1)
                s = jnp.where(cidx > r, NEG, s)
            s_scr[kb, rows, :] = s
            tl = lane_tiles(s)
            tmax = jnp.maximum(jnp.maximum(tl[0], tl[1]), jnp.maximum(tl[2], tl[3]))
            if not init:
                tmax = jnp.maximum(mp_scr[rows, :], tmax)
            if diag:
                m_scr[rows, :] = jnp.broadcast_to(jnp.max(tmax, axis=-1, keepdims=True), (t, LANES))
            else:
                mp_scr[rows, :] = tmax

    lq = lq_ref[...]
    lk = lk_ref[...]
    lam = (jnp.exp(jnp.sum(lq[0:1] * lk[0:1], axis=-1, keepdims=True))
           - jnp.exp(jnp.sum(lq[1:2] * lk[1:2], axis=-1, keepdims=True)) + lam_init)

    def pass2(kb, init, last):
        v = v_ref[0, kb * t:(kb + 1) * t, :]
        one = jnp.ones_like(v)
        vh = (jnp.where(lane_v0, v, one), jnp.where(lane_v0, one, v))
        done = []
        for g in range(4):
            rows = pl.ds(g * t, t)
            m = m_scr[rows, :]
            p = jnp.exp2(s_scr[kb, rows, :] - jnp.concatenate([m] * n_lt, axis=1))
            pv = jnp.dot(p.astype(BF16), vh[g // 2], preferred_element_type=F32)
            acc = pv if init else acc_scr[rows, :] + pv
            if not last:
                acc_scr[rows, :] = acc
                continue
            done.append(acc)
            if g % 2 == 1:
                hh = g // 2
                cols = slice(hh * DIFF_V_DIM, (hh + 1) * DIFF_V_DIM)
                sums = slice((1 - hh) * DIFF_V_DIM, (2 - hh) * DIFF_V_DIM)
                a1, a2 = done[g - 1], done[g]
                o = a1[:, cols] / a1[:, sums] - lam * (a2[:, cols] / a2[:, sums])
                o_ref[0, :, cols] = (_rms_rows(o, sn_ref[0, hh:hh + 1, :]) * (1.0 - lam_init)).astype(BF16)

    for nq in range(SEQ // t):
        @pl.when(qi == nq)
        def _(nq=nq):
            for kb in range(nq):
                pass1(kb, False, kb == 0)
            pass1(nq, True, nq == 0)

    for nq in range(SEQ // t):
        @pl.when(qi == nq)
        def _(nq=nq):
            for kb in range(nq):
                pass2(kb, kb == 0, False)
            pass2(nq, nq == 0, True)


def _diff_attn(qd, kd, vd, lq, lk, subnorm, lam_init):
    t = DIFF_T
    assert SEQ // t == 4
    kv_spec = pl.BlockSpec((1, SEQ, SEG), lambda b, p, i: (b, 0, p))
    q_spec = pl.BlockSpec((1, t, SEG), lambda b, p, i: (b, i, p))
    return pl.pallas_call(
        functools.partial(_diff_kernel, lam_init=lam_init),
        out_shape=jax.ShapeDtypeStruct((BATCH, SEQ, DIFF_V_WIDTH), BF16),
        grid=(BATCH, 2, SEQ // t),
        in_specs=[
            q_spec, kv_spec, kv_spec,
            pl.BlockSpec((2, HEAD_DIM), lambda b, p, i: (0, 0)),
            pl.BlockSpec((2, HEAD_DIM), lambda b, p, i: (0, 0)),
            pl.BlockSpec((1, 2, DIFF_V_DIM), lambda b, p, i: (p, 0, 0)),
        ],
        out_specs=q_spec,
        scratch_shapes=[pltpu.VMEM((SEQ // t, 4 * t, t), F32),
                        pltpu.VMEM((4 * t, LANES), F32),
                        pltpu.VMEM((4 * t, LANES), F32),
                        pltpu.VMEM((4 * t, SEG), F32)],
        compiler_params=pltpu.CompilerParams(
            dimension_semantics=("arbitrary", "arbitrary", "arbitrary"), vmem_limit_bytes=VMEM_LIMIT_BYTES),
        name="diff_attn",
    )(qd, kd, vd, lq, lk, subnorm)


def _qk_gain_rows(qk_gain_dil, qk_gain_diff):
    k_scale = HEAD_DIM ** 0.5
    q_scale = LOG2E
    rows = []
    for g in range(N_DIL_GROUPS):
        rows.append(qk_gain_dil[0, g].reshape(SEG) * q_scale)
        rows.append(qk_gain_dil[1, g].reshape(SEG) * k_scale)
    qd = qk_gain_diff[0].reshape(2, SEG) * q_scale
    kd = qk_gain_diff[1].reshape(2, SEG) * k_scale
    rows += [qd[0], qd[1], kd[0], kd[1]]
    rows += [jnp.zeros((SEG,), F32)] * (16 - len(rows))
    return jnp.stack(rows).astype(F32)


def _cast_stacked(w, block_rows):
    depth, rows, cols = w.shape
    return _cast_bf16(w.reshape(depth * rows, cols), block_rows).reshape(depth, rows, cols)


def kernel(x, ffn1_norm, ffn1_w_in, ffn1_w_out, mix_norm, w_in, qk_gain_dil, qk_gain_diff, lambda_q, lambda_k,
           diff_subnorm, w_branch_dil, w_branch_diff, w_out, ffn2_norm, ffn2_w_in, ffn2_w_out):
    b, s, d = x.shape
    assert (b, s, d) == (BATCH, SEQ, D_MODEL)
    xr = x.reshape(M_ROWS, D_MODEL)
    dil_bias = jnp.asarray(_dil_bias_table())
    f1_in, f2_in = _cast_stacked(ffn1_w_in, CAST_ROWS), _cast_stacked(ffn2_w_in, CAST_ROWS)
    f1_out, f2_out = _cast_stacked(ffn1_w_out, CAST_ROWS_OUT), _cast_stacked(ffn2_w_out, CAST_ROWS_OUT)
    w_in_p = _cast_permute_w_in(w_in.reshape(DEPTH * D_MODEL, IN_COLS)).reshape(DEPTH, D_MODEL, IN_COLS)
    wa, wb, wo = w_branch_dil.astype(BF16), w_branch_diff.astype(BF16), w_out.astype(BF16)
    for l in range(DEPTH):
        lam_init = 0.8 - 0.6 * math.exp(-0.3 * l)
        xr = _ffn(xr, ffn1_norm[l].reshape(1, D_MODEL), f1_in, f1_out, l)
        g0, g1, g2, qd, kd, vd, gates = _proj(
            xr, mix_norm[l].reshape(1, D_MODEL), w_in_p, _qk_gain_rows(qk_gain_dil[l], qk_gain_diff[l]), l)
        o_dil = _dil_attn(g0.reshape(BATCH, SEQ, DIL_WIDTH), g1.reshape(BATCH, SEQ, DIL_WIDTH),
                          g2.reshape(BATCH, SEQ, DIL_WIDTH), dil_bias)
        o_diff = _diff_attn(qd.reshape(BATCH, SEQ, DIFF_QK_WIDTH), kd.reshape(BATCH, SEQ, DIFF_QK_WIDTH),
                            vd.reshape(BATCH, SEQ, DIFF_V_WIDTH), lambda_q[l], lambda_k[l],
                            diff_subnorm[l].reshape(2, 2, DIFF_V_DIM), lam_init)
        xr = _merge_ffn(xr, o_dil.reshape(M_ROWS, DIL_OUT), o_diff.reshape(M_ROWS, DIFF_V_WIDTH), gates,
                        wa, wb, wo, ffn2_norm[l].reshape(1, D_MODEL), f2_in, f2_out, l)
    return xr.reshape(BATCH, SEQ, D_MODEL)
```

```python
import functools
import math

import jax
import jax.numpy as jnp
import numpy as np
from jax import lax
from jax.experimental import pallas as pl
from jax.experimental.pallas import tpu as pltpu

F32 = jnp.float32
BF16 = jnp.bfloat16

D_MODEL = 1024
BATCH = 8
SEQ = 2048
DEPTH = 2
HEAD_DIM = 64
DIL_CONFIGS = ((128, 1), (512, 4), (2048, 16))
N_DIL_GROUPS = 3
HEADS_PER_GROUP = 4
N_DIL_HEADS = 12
N_DIFF_HEADS = 4
DIFF_V_DIM = 128
N_ALIBI_HEADS = 16
D_FF = 2816
BLOCK = 128
EPS = 1e-6
DIL_WIDTH = 768
DIL_OUT = 256
DIFF_QK_WIDTH = 512
DIFF_V_WIDTH = 512
IN_COLS = 5888
NEG = -1e30

LANES = 128
VMEM_LIMIT_BYTES = 56 * 1024 * 1024

M_ROWS = BATCH * SEQ
SEG = 256
N_SEG = IN_COLS // SEG
FFN_TM = 512
FFN_FC = 256
PROJ_TM = 512
CAST_ROWS = 256
CAST_ROWS_OUT = D_FF // 2
DIL_JS = 4
DIFF_T = 512
LOG2E = math.log2(math.e)

SLOPES = tuple(float(np.float32(2.0) ** np.float32(-8.0 * i / N_ALIBI_HEADS)) for i in range(1, N_ALIBI_HEADS + 1))


def _rms_rows(x, gain):
    ms = jnp.mean(x * x, axis=-1, keepdims=True)
    return x * lax.rsqrt(ms + EPS) * gain


def _row_rsqrt(x):
    return lax.rsqrt(jnp.mean(x * x, axis=-1, keepdims=True) + EPS)


def _ffn_rows(x, win_ref, wout_ref, a_scr):
    h = x.astype(BF16)
    r = _row_rsqrt(x)
    for c in range(D_FF // FFN_FC):
        lo = c * FFN_FC
        gate = jnp.dot(h, win_ref[:, lo:lo + FFN_FC], preferred_element_type=F32) * r
        up = jnp.dot(h, win_ref[:, D_FF + lo:D_FF + lo + FFN_FC], preferred_element_type=F32) * r
        a_scr[:, lo:lo + FFN_FC] = (gate * jax.nn.sigmoid(gate) * up).astype(BF16)
    y = jnp.dot(a_scr[...], wout_ref[...], preferred_element_type=F32)
    return x + 0.5 * y


def _ffn_kernel(x_ref, win_ref, wout_ref, o_ref, a_scr):
    o_ref[...] = _ffn_rows(x_ref[...], win_ref, wout_ref, a_scr)


def _merge_ffn_kernel(x_ref, od_ref, of_ref, gate_ref, wa_ref, wb_ref, wo_ref, win_ref, wout_ref,
                      o_ref, a_scr):
    ya = jnp.dot(od_ref[...], wa_ref[...], preferred_element_type=F32)
    yb = jnp.dot(of_ref[...], wb_ref[...], preferred_element_type=F32)
    y = gate_ref[:, 0:D_MODEL].astype(F32) * ya + gate_ref[:, D_MODEL:2 * D_MODEL].astype(F32) * yb
    x = x_ref[...] + jnp.dot(y.astype(BF16), wo_ref[...], preferred_element_type=F32)
    o_ref[...] = _ffn_rows(x, win_ref, wout_ref, a_scr)


def _resident(shape):
    return pl.BlockSpec(shape, lambda i: (0,) * len(shape), pipeline_mode=pl.Buffered(1))


def _resident_layer(shape, layer):
    return pl.BlockSpec((None,) + shape, lambda i: (layer,) + (0,) * len(shape), pipeline_mode=pl.Buffered(1))


def _ffn(x, w_in, w_out, layer):
    tm = FFN_TM
    row = lambda i: (i, 0)
    return pl.pallas_call(
        _ffn_kernel,
        out_shape=jax.ShapeDtypeStruct((M_ROWS, D_MODEL), F32),
        grid=(M_ROWS // tm,),
        in_specs=[
            pl.BlockSpec((tm, D_MODEL), row),
            _resident_layer((D_MODEL, 2 * D_FF), layer),
            _resident_layer((D_FF, D_MODEL), layer),
        ],
        out_specs=pl.BlockSpec((tm, D_MODEL), row),
        scratch_shapes=[pltpu.VMEM((tm, D_FF), BF16)],
        compiler_params=pltpu.CompilerParams(
            dimension_semantics=("arbitrary",), vmem_limit_bytes=VMEM_LIMIT_BYTES),
        name="ffn",
    )(x, w_in, w_out)


def _merge_ffn(x, o_dil, o_diff, gates, wa, wb, wo, w_in, w_out, layer):
    tm = FFN_TM
    row = lambda i: (i, 0)
    return pl.pallas_call(
        _merge_ffn_kernel,
        out_shape=jax.ShapeDtypeStruct((M_ROWS, D_MODEL), F32),
        grid=(M_ROWS // tm,),
        in_specs=[
            pl.BlockSpec((tm, D_MODEL), row),
            pl.BlockSpec((tm, DIL_OUT), row),
            pl.BlockSpec((tm, DIFF_V_WIDTH), row),
            pl.BlockSpec((tm, 2 * D_MODEL), row),
            _resident_layer((DIL_OUT, D_MODEL), layer),
            _resident_layer((DIFF_V_WIDTH, D_MODEL), layer),
            _resident_layer((D_MODEL, D_MODEL), layer),
            _resident_layer((D_MODEL, 2 * D_FF), layer),
            _resident_layer((D_FF, D_MODEL), layer),
        ],
        out_specs=pl.BlockSpec((tm, D_MODEL), row),
        scratch_shapes=[pltpu.VMEM((tm, D_FF), BF16)],
        compiler_params=pltpu.CompilerParams(
            dimension_semantics=("arbitrary",), vmem_limit_bytes=VMEM_LIMIT_BYTES),
        name="merge_ffn",
    )(x, o_dil, o_diff, gates, wa, wb, wo, w_in, w_out)


def _cast_kernel(w_ref, o_ref):
    o_ref[...] = w_ref[...].astype(BF16)


def _scaled_cast_kernel(w_ref, g_ref, o_ref):
    o_ref[...] = (w_ref[...] * g_ref[...]).astype(BF16)


def _cast_bf16(w, block_rows, row_gain=None):
    rows, cols = w.shape
    blk = pl.BlockSpec((block_rows, cols), lambda i: (i, 0))
    if row_gain is None:
        body, in_specs, args = _cast_kernel, [blk], (w,)
    else:
        body, in_specs, args = _scaled_cast_kernel, [blk, pl.BlockSpec((block_rows, 1), lambda i: (i, 0))], (w, row_gain)
    return pl.pallas_call(
        body,
        out_shape=jax.ShapeDtypeStruct((rows, cols), BF16),
        grid=(rows // block_rows,),
        in_specs=in_specs,
        out_specs=blk,
        compiler_params=pltpu.CompilerParams(
            dimension_semantics=("arbitrary",), vmem_limit_bytes=VMEM_LIMIT_BYTES),
        name="cast_bf16",
    )(*args)


def _w_in_src_seg(j):
    return jnp.where(j < 3 * N_DIL_GROUPS, (j % 3) * N_DIL_GROUPS + j // 3, j)


def _cast_permute_w_in(w, row_gain):
    rows = w.shape[0]
    return pl.pallas_call(
        _scaled_cast_kernel,
        out_shape=jax.ShapeDtypeStruct((rows, IN_COLS), BF16),
        grid=(N_SEG,),
        in_specs=[pl.BlockSpec((rows, SEG), lambda j: (0, _w_in_src_seg(j))),
                  pl.BlockSpec((rows, 1), lambda j: (0, 0))],
        out_specs=pl.BlockSpec((rows, SEG), lambda j: (0, j)),
        compiler_params=pltpu.CompilerParams(
            dimension_semantics=("arbitrary",), vmem_limit_bytes=VMEM_LIMIT_BYTES),
        name="cast_permute_w_in",
    )(w, row_gain)


def _proj_kernel(x_ref, w_ref, gain_ref,
                 o_g0, o_g1, o_g2, o_qd, o_kd, o_vd, o_gate, scr1, scr2, scr3):
    tm = PROJ_TM
    x = x_ref[...]
    h = x.astype(BF16)
    r = _row_rsqrt(x)
    lane_lo = lax.broadcasted_iota(jnp.int32, (1, LANES), 1) < HEAD_DIM

    def seg(i):
        return jnp.dot(h, w_ref[:, i * SEG:(i + 1) * SEG], preferred_element_type=F32) * r

    def qk_norm(y, gi):
        tiles = []
        for u in range(SEG // LANES):
            yt = y[:, u * LANES:(u + 1) * LANES]
            sq = yt * yt
            low = jnp.sum(jnp.where(lane_lo, sq, 0.0), axis=-1, keepdims=True)
            high = jnp.sum(jnp.where(lane_lo, 0.0, sq), axis=-1, keepdims=True)
            ssq = jnp.where(lane_lo, low, high)
            tiles.append(yt * lax.rsqrt(ssq + HEAD_DIM * EPS))
        return jnp.concatenate(tiles, axis=-1) * gain_ref[gi:gi + 1, :]

    o_g0[:, 0:SEG] = qk_norm(seg(0), 0).astype(BF16)
    o_g0[:, SEG:2 * SEG] = qk_norm(seg(1), 1).astype(BF16)
    o_g0[:, 2 * SEG:3 * SEG] = seg(2).astype(BF16)
    n_t = DIL_WIDTH // LANES
    quarter = tm // 4
    for g, (scr, o_g) in ((1, (scr1, o_g1)), (2, (scr2, o_g2))):
        ys = (qk_norm(seg(3 * g), 2 * g), qk_norm(seg(3 * g + 1), 2 * g + 1), seg(3 * g + 2))
        for t, y in enumerate(ys):
            for u in range(SEG // LANES):
                scr[2 * t + u] = y[:, u * LANES:(u + 1) * LANES]
        if g == 1:
            for c in range(4):
                for t in range(n_t):
                    o_g[0, c, :, t * LANES:(t + 1) * LANES] = scr[t, pl.ds(c, quarter, stride=4), :].astype(BF16)
        else:
            for c1 in range(4):
                for t in range(n_t):
                    scr3[t, c1 * quarter:(c1 + 1) * quarter, :] = scr[t, pl.ds(c1, quarter, stride=4), :]
            for c in range(16):
                c1, c2 = c % 4, c // 4
                for t in range(n_t):
                    o_g[0, c, :, t * LANES:(t + 1) * LANES] = (
                        scr3[t, pl.ds(c1 * quarter + c2, tm // 16, stride=4), :].astype(BF16))
    for t in range(2):
        o_qd[:, t * SEG:(t + 1) * SEG] = qk_norm(seg(9 + t), 6 + t).astype(BF16)
        o_kd[:, t * SEG:(t + 1) * SEG] = qk_norm(seg(11 + t), 8 + t).astype(BF16)
        o_vd[:, t * SEG:(t + 1) * SEG] = seg(13 + t).astype(BF16)
    for t in range(8):
        o_gate[:, t * SEG:(t + 1) * SEG] = jax.nn.sigmoid(seg(15 + t)).astype(BF16)


def _proj(x, w, qk_gain, layer):
    tm = PROJ_TM
    tiles_per_seq = SEQ // tm
    const = lambda i: (0, 0)
    row = lambda i: (i, 0)
    perm = lambda i: (i // tiles_per_seq, 0, i % tiles_per_seq, 0)
    out_shape = (
        jax.ShapeDtypeStruct((M_ROWS, DIL_WIDTH), BF16),
        jax.ShapeDtypeStruct((BATCH, 4, SEQ // 4, DIL_WIDTH), BF16),
        jax.ShapeDtypeStruct((BATCH, 16, SEQ // 16, DIL_WIDTH), BF16),
        jax.ShapeDtypeStruct((M_ROWS, DIFF_QK_WIDTH), BF16),
        jax.ShapeDtypeStruct((M_ROWS, DIFF_QK_WIDTH), BF16),
        jax.ShapeDtypeStruct((M_ROWS, DIFF_V_WIDTH), BF16),
        jax.ShapeDtypeStruct((M_ROWS, 2 * D_MODEL), BF16),
    )
    out_specs = (
        pl.BlockSpec((tm, DIL_WIDTH), row),
        pl.BlockSpec((1, 4, tm // 4, DIL_WIDTH), perm),
        pl.BlockSpec((1, 16, tm // 16, DIL_WIDTH), perm),
        pl.BlockSpec((tm, DIFF_QK_WIDTH), row),
        pl.BlockSpec((tm, DIFF_QK_WIDTH), row),
        pl.BlockSpec((tm, DIFF_V_WIDTH), row),
        pl.BlockSpec((tm, 2 * D_MODEL), row),
    )
    return pl.pallas_call(
        _proj_kernel,
        out_shape=out_shape,
        grid=(M_ROWS // tm,),
        in_specs=[
            pl.BlockSpec((tm, D_MODEL), row),
            _resident_layer((D_MODEL, IN_COLS), layer),
            pl.BlockSpec((16, SEG), const),
        ],
        out_specs=out_specs,
        scratch_shapes=[pltpu.VMEM((DIL_WIDTH // LANES, tm, LANES), F32)] * 3,
        compiler_params=pltpu.CompilerParams(
            dimension_semantics=("arbitrary",), vmem_limit_bytes=VMEM_LIMIT_BYTES),
        name="proj",
    )(x, w, qk_gain)


def _dil_block(ref, row_cur, row_prev, bias, lane_lo):
    q = ref[0, pl.ds(row_cur, BLOCK), 0:SEG]
    k = ref[0, pl.ds(row_cur, BLOCK), SEG:2 * SEG]
    v = ref[0, pl.ds(row_cur, BLOCK), 2 * SEG:3 * SEG]
    if row_prev is not None:
        k = jnp.concatenate([ref[0, pl.ds(row_prev, BLOCK), SEG:2 * SEG], k], axis=0)
        v = jnp.concatenate([ref[0, pl.ds(row_prev, BLOCK), 2 * SEG:3 * SEG], v], axis=0)
    zero = jnp.zeros((BLOCK, LANES), BF16)
    copies = []
    for j in range(HEADS_PER_GROUP):
        qt = q[:, (j // 2) * LANES:(j // 2 + 1) * LANES]
        kept = jnp.where(lane_lo, qt, zero) if j % 2 == 0 else jnp.where(lane_lo, zero, qt)
        copies.append(jnp.concatenate([kept, zero] if j < 2 else [zero, kept], axis=1))
    qs = jnp.concatenate(copies, axis=0)
    s = lax.dot_general(qs, k, (((1,), (1,)), ((), ())), preferred_element_type=F32) + bias
    m = jnp.max(s, axis=-1, keepdims=True)
    p = jnp.exp2(s - m)
    l = jnp.sum(p, axis=-1, keepdims=True)
    pv = jnp.dot(p.astype(BF16), v, preferred_element_type=F32)
    stats = []
    for u in range(SEG // LANES):
        ra, rb = slice(2 * u * BLOCK, (2 * u + 1) * BLOCK), slice((2 * u + 1) * BLOCK, (2 * u + 2) * BLOCK)
        cols = slice(u * LANES, (u + 1) * LANES)
        stats.append((jnp.where(lane_lo, pv[ra, cols], pv[rb, cols]),
                      jnp.where(lane_lo, m[ra], m[rb]), jnp.where(lane_lo, l[ra], l[rb])))
    return stats


def _dil_kernel(g0_ref, g1_ref, g2_ref, bias_ref, o_ref, o_scr, m_scr, l_scr):
    step = pl.program_id(1)
    lane_lo = lax.broadcasted_iota(jnp.int32, (1, LANES), 1) < HEAD_DIM

    def put(g, rows, stats):
        for u, (o, m, l) in enumerate(stats):
            o_scr[g, u, rows, :] = o
            m_scr[g, u, rows, :] = m
            l_scr[g, u, rows, :] = l

    for jj in range(DIL_JS):
        j = step * DIL_JS + jj
        row = pl.multiple_of(j * BLOCK, BLOCK)

        prev = pl.multiple_of(jnp.maximum(j - 1, 0) * BLOCK, BLOCK)
        tab = jnp.where(j == 0, N_DIL_GROUPS, 0) if jj == 0 else 0
        put(0, pl.ds(row, BLOCK), _dil_block(g0_ref, row, prev, bias_ref[tab], lane_lo))

        c, n = step * (DIL_JS // 4) + jj // 4, jj % 4
        if n == 0:
            stats = _dil_block(g1_ref, row, None, bias_ref[1, :, BLOCK:2 * BLOCK], lane_lo)
        else:
            prev = pl.multiple_of(row - BLOCK, BLOCK)
            stats = _dil_block(g1_ref, row, prev, bias_ref[1], lane_lo)
        put(1, pl.ds(n * (4 * BLOCK) + c, BLOCK, stride=4), stats)

        put(2, pl.ds(j, BLOCK, stride=16),
            _dil_block(g2_ref, row, None, bias_ref[2, :, BLOCK:2 * BLOCK], lane_lo))

    @pl.when(step == pl.num_programs(1) - 1)
    def _():
        rows = 256
        for r in range(SEQ // rows):
            sl = pl.ds(r * rows, rows)
            for u in range(SEG // LANES):
                m0, m1, m2 = m_scr[0, u, sl, :], m_scr[1, u, sl, :], m_scr[2, u, sl, :]
                mx = jnp.maximum(jnp.maximum(m0, m1), m2)
                w0, w1, w2 = jnp.exp2(m0 - mx), jnp.exp2(m1 - mx), jnp.exp2(m2 - mx)
                num = w0 * o_scr[0, u, sl, :] + w1 * o_scr[1, u, sl, :] + w2 * o_scr[2, u, sl, :]
                den = w0 * l_scr[0, u, sl, :] + w1 * l_scr[1, u, sl, :] + w2 * l_scr[2, u, sl, :]
                o_ref[0, sl, u * LANES:(u + 1) * LANES] = (num / den).astype(BF16)


def _dil_bias_table():
    qi = np.arange(BLOCK)[:, None]
    ki = np.arange(2 * BLOCK)[None, :]
    dist = BLOCK + qi - ki
    valid = (dist >= 0) & (dist <= BLOCK)
    tab = np.full((N_DIL_GROUPS + 1, HEADS_PER_GROUP * BLOCK, 2 * BLOCK), NEG, np.float32)
    for g, (_, dil) in enumerate(DIL_CONFIGS):
        for h in range(HEADS_PER_GROUP):
            slope = np.float32(SLOPES[g * HEADS_PER_GROUP + h])
            b = np.where(valid, -slope * np.float32(LOG2E) * (dil * dist).astype(np.float32),
                         np.float32(NEG)).astype(np.float32)
            tab[g, h * BLOCK:(h + 1) * BLOCK] = b
            if g == 0:
                tab[N_DIL_GROUPS, h * BLOCK:(h + 1) * BLOCK] = np.where(ki >= BLOCK, b, np.float32(NEG))
    return tab


def _dil_attn(g0, g1, g2, bias):
    n_blk = SEQ // BLOCK
    seq_spec = pl.BlockSpec((1, SEQ, DIL_WIDTH), lambda b, j: (b, 0, 0))
    return pl.pallas_call(
        _dil_kernel,
        out_shape=jax.ShapeDtypeStruct((BATCH, SEQ, DIL_OUT), BF16),
        grid=(BATCH, n_blk // DIL_JS),
        in_specs=[seq_spec, seq_spec, seq_spec,
                  pl.BlockSpec(bias.shape, lambda b, j: (0, 0, 0))],
        out_specs=pl.BlockSpec((1, SEQ, DIL_OUT), lambda b, j: (b, 0, 0)),
        scratch_shapes=[pltpu.VMEM((N_DIL_GROUPS, DIL_OUT // LANES, SEQ, LANES), F32)] * 3,
        compiler_params=pltpu.CompilerParams(
            dimension_semantics=("arbitrary", "arbitrary"), vmem_limit_bytes=VMEM_LIMIT_BYTES),
        name="dil_attn",
    )(g0, g1, g2, bias)


def _diff_kernel(q_ref, k_ref, v_ref, lq_ref, lk_ref, sn_ref, o_ref,
                 s_scr, mp_scr, m_scr, acc_scr, *, lam_init):
    t = DIFF_T
    n_lt = t // LANES
    pair = pl.program_id(1)
    qi = pl.program_id(2)
    q = q_ref[0]
    lane_map = lax.broadcasted_iota(jnp.int32, (1, SEG), 1) // HEAD_DIM
    zero = jnp.zeros_like(q)
    qs = [jnp.where(lane_map == g, q, zero) for g in range(4)]
    slopes = (jnp.where(pair == 0, SLOPES[N_DIL_HEADS] * LOG2E, SLOPES[N_DIL_HEADS + 2] * LOG2E).astype(F32),
              jnp.where(pair == 0, SLOPES[N_DIL_HEADS + 1] * LOG2E, SLOPES[N_DIL_HEADS + 3] * LOG2E).astype(F32))
    lane_k = lax.broadcasted_iota(jnp.int32, (1, t), 1)
    lane_v0 = lax.broadcasted_iota(jnp.int32, (1, SEG), 1) < DIFF_V_DIM

    def lane_tiles(x):
        return [x[:, u * LANES:(u + 1) * LANES] for u in range(x.shape[1] // LANES)]

    half = t // 2
    above = (lax.broadcasted_iota(jnp.int32, (half, half), 1)
             > lax.broadcasted_iota(jnp.int32, (half, half), 0))
    nt = (((1,), (1,)), ((), ()))

    def pass1(kb, diag, init):
        k = k_ref[0, kb * t:(kb + 1) * t, :]
        bias = [sl * (kb * t + lane_k).astype(F32) for sl in slopes]
        for g in range(4):
            rows = pl.ds(g * t, t)
            if diag:
                sa = lax.dot_general(qs[g][:half], k[:half], nt, preferred_element_type=F32)
                sa = jnp.where(above, NEG, sa + bias[g // 2][:, :half])
                sb = lax.dot_general(qs[g][half:], k, nt, preferred_element_type=F32) + bias[g // 2]
                sb = jnp.concatenate([sb[:, :half], jnp.where(above, NEG, sb[:, half:])], axis=1)
                s_scr[kb, pl.ds(g * t, half), 0:half] = sa
                s_scr[kb, pl.ds(g * t + half, half), :] = sb
                ta, tb = lane_tiles(sa), lane_tiles(sb)
                tmax = jnp.concatenate([jnp.maximum(ta[0], ta[1]),
                                        jnp.maximum(jnp.maximum(tb[0], tb[1]), jnp.maximum(tb[2], tb[3]))], axis=0)
            else:
                s = lax.dot_general(qs[g], k, nt, preferred_element_type=F32) + bias[g // 2]
                s_scr[kb, rows, :] = s
                tl = lane_tiles(s)
                tmax = jnp.maximum(jnp.maximum(tl[0], tl[1]), jnp.maximum(tl[2], tl[3]))
            if not init:
                tmax = jnp.maximum(mp_scr[rows, :], tmax)
            if diag:
                m_scr[rows, :] = jnp.broadcast_to(jnp.max(tmax, axis=-1, keepdims=True), (t, LANES))
            else:
                mp_scr[rows, :] = tmax

    lq = lq_ref[...]
    lk = lk_ref[...]
    lam = (jnp.exp(jnp.sum(lq[0:1] * lk[0:1], axis=-1, keepdims=True))
           - jnp.exp(jnp.sum(lq[1:2] * lk[1:2], axis=-1, keepdims=True)) + lam_init)

    def pass2(kb, init, last):
        v = v_ref[0, kb * t:(kb + 1) * t, :]
        one = jnp.ones_like(v)
        vh = (jnp.where(lane_v0, v, one), jnp.where(lane_v0, one, v))
        done = []
        for g in range(4):
            rows = pl.ds(g * t, t)
            m = m_scr[rows, :]
            if last:
                pa = jnp.exp2(s_scr[kb, pl.ds(g * t, half), 0:half]
                              - jnp.concatenate([m[:half]] * (half // LANES), axis=1))
                pb = jnp.exp2(s_scr[kb, pl.ds(g * t + half, half), :] - jnp.concatenate([m[half:]] * n_lt, axis=1))
                pv = jnp.concatenate(
                    [jnp.dot(pa.astype(BF16), vh[g // 2][:half], preferred_element_type=F32),
                     jnp.dot(pb.astype(BF16), vh[g // 2], preferred_element_type=F32)], axis=0)
            else:
                p = jnp.exp2(s_scr[kb, rows, :] - jnp.concatenate([m] * n_lt, axis=1))
                pv = jnp.dot(p.astype(BF16), vh[g // 2], preferred_element_type=F32)
            acc = pv if init else acc_scr[rows, :] + pv
            if not last:
                acc_scr[rows, :] = acc
                continue
            done.append(acc)
            if g % 2 == 1:
                hh = g // 2
                cols = slice(hh * DIFF_V_DIM, (hh + 1) * DIFF_V_DIM)
                sums = slice((1 - hh) * DIFF_V_DIM, (2 - hh) * DIFF_V_DIM)
                a1, a2 = done[g - 1], done[g]
                o = a1[:, cols] / a1[:, sums] - lam * (a2[:, cols] / a2[:, sums])
                o_ref[0, :, cols] = (_rms_rows(o, sn_ref[0, hh:hh + 1, :]) * (1.0 - lam_init)).astype(BF16)

    for nq in range(SEQ // t):
        @pl.when(qi == nq)
        def _(nq=nq):
            for kb in range(nq):
                pass1(kb, False, kb == 0)
            pass1(nq, True, nq == 0)

    for nq in range(SEQ // t):
        @pl.when(qi == nq)
        def _(nq=nq):
            for kb in range(nq):
                pass2(kb, kb == 0, False)
            pass2(nq, nq == 0, True)


def _diff_attn(qd, kd, vd, lq, lk, subnorm, lam_init):
    t = DIFF_T
    assert SEQ // t == 4
    kv_spec = pl.BlockSpec((1, SEQ, SEG), lambda b, p, i: (b, 0, p))
    q_spec = pl.BlockSpec((1, t, SEG), lambda b, p, i: (b, i, p))
    return pl.pallas_call(
        functools.partial(_diff_kernel, lam_init=lam_init),
        out_shape=jax.ShapeDtypeStruct((BATCH, SEQ, DIFF_V_WIDTH), BF16),
        grid=(BATCH, 2, SEQ // t),
        in_specs=[
            q_spec, kv_spec, kv_spec,
            pl.BlockSpec((2, HEAD_DIM), lambda b, p, i: (0, 0)),
            pl.BlockSpec((2, HEAD_DIM), lambda b, p, i: (0, 0)),
            pl.BlockSpec((1, 2, DIFF_V_DIM), lambda b, p, i: (p, 0, 0)),
        ],
        out_specs=q_spec,
        scratch_shapes=[pltpu.VMEM((SEQ // t, 4 * t, t), F32),
                        pltpu.VMEM((4 * t, LANES), F32),
                        pltpu.VMEM((4 * t, LANES), F32),
                        pltpu.VMEM((4 * t, SEG), F32)],
        compiler_params=pltpu.CompilerParams(
            dimension_semantics=("arbitrary", "arbitrary", "arbitrary"), vmem_limit_bytes=VMEM_LIMIT_BYTES),
        name="diff_attn",
    )(qd, kd, vd, lq, lk, subnorm)


def _qk_gain_rows(qk_gain_dil, qk_gain_diff):
    k_scale = HEAD_DIM ** 0.5
    q_scale = LOG2E
    rows = []
    for g in range(N_DIL_GROUPS):
        rows.append(qk_gain_dil[0, g].reshape(SEG) * q_scale)
        rows.append(qk_gain_dil[1, g].reshape(SEG) * k_scale)
    qd = qk_gain_diff[0].reshape(2, SEG) * q_scale
    kd = qk_gain_diff[1].reshape(2, SEG) * k_scale
    rows += [qd[0], qd[1], kd[0], kd[1]]
    rows += [jnp.zeros((SEG,), F32)] * (16 - len(rows))
    return jnp.stack(rows).astype(F32)


def _cast_stacked(w, block_rows, norm_gain=None):
    depth, rows, cols = w.shape
    row_gain = None if norm_gain is None else norm_gain.reshape(depth * rows, 1)
    return _cast_bf16(w.reshape(depth * rows, cols), block_rows, row_gain).reshape(depth, rows, cols)


def kernel(x, ffn1_norm, ffn1_w_in, ffn1_w_out, mix_norm, w_in, qk_gain_dil, qk_gain_diff, lambda_q, lambda_k,
           diff_subnorm, w_branch_dil, w_branch_diff, w_out, ffn2_norm, ffn2_w_in, ffn2_w_out):
    b, s, d = x.shape
    assert (b, s, d) == (BATCH, SEQ, D_MODEL)
    xr = x.reshape(M_ROWS, D_MODEL)
    dil_bias = jnp.asarray(_dil_bias_table())
    f1_in = _cast_stacked(ffn1_w_in, CAST_ROWS, ffn1_norm)
    f2_in = _cast_stacked(ffn2_w_in, CAST_ROWS, ffn2_norm)
    f1_out, f2_out = _cast_stacked(ffn1_w_out, CAST_ROWS_OUT), _cast_stacked(ffn2_w_out, CAST_ROWS_OUT)
    w_in_p = _cast_permute_w_in(w_in.reshape(DEPTH * D_MODEL, IN_COLS),
                                mix_norm.reshape(DEPTH * D_MODEL, 1)).reshape(DEPTH, D_MODEL, IN_COLS)
    wa, wb, wo = w_branch_dil.astype(BF16), w_branch_diff.astype(BF16), w_out.astype(BF16)
    for l in range(DEPTH):
        lam_init = 0.8 - 0.6 * math.exp(-0.3 * l)
        xr = _ffn(xr, f1_in, f1_out, l)
        g0, g1, g2, qd, kd, vd, gates = _proj(xr, w_in_p, _qk_gain_rows(qk_gain_dil[l], qk_gain_diff[l]), l)
        o_dil = _dil_attn(g0.reshape(BATCH, SEQ, DIL_WIDTH), g1.reshape(BATCH, SEQ, DIL_WIDTH),
                          g2.reshape(BATCH, SEQ, DIL_WIDTH), dil_bias)
        o_diff = _diff_attn(qd.reshape(BATCH, SEQ, DIFF_QK_WIDTH), kd.reshape(BATCH, SEQ, DIFF_QK_WIDTH),
                            vd.reshape(BATCH, SEQ, DIFF_V_WIDTH), lambda_q[l], lambda_k[l],
                            diff_subnorm[l].reshape(2, 2, DIFF_V_DIM), lam_init)
        xr = _merge_ffn(xr, o_dil.reshape(M_ROWS, DIL_OUT), o_diff.reshape(M_ROWS, DIFF_V_WIDTH), gates,
                        wa, wb, wo, f2_in, f2_out, l)
    return xr.reshape(BATCH, SEQ, D_MODEL)
```

```python
import functools
import math

import jax
import jax.numpy as jnp
import numpy as np
from jax import lax
from jax.experimental import pallas as pl
from jax.experimental.pallas import tpu as pltpu

F32 = jnp.float32
BF16 = jnp.bfloat16

D_MODEL = 1024
BATCH = 8
SEQ = 2048
DEPTH = 2
HEAD_DIM = 64
DIL_CONFIGS = ((128, 1), (512, 4), (2048, 16))
N_DIL_GROUPS = 3
HEADS_PER_GROUP = 4
N_DIL_HEADS = 12
N_DIFF_HEADS = 4
DIFF_V_DIM = 128
N_ALIBI_HEADS = 16
D_FF = 2816
BLOCK = 128
EPS = 1e-6
DIL_WIDTH = 768
DIL_OUT = 256
DIFF_QK_WIDTH = 512
DIFF_V_WIDTH = 512
IN_COLS = 5888
NEG = -1e30

LANES = 128
VMEM_LIMIT_BYTES = 56 * 1024 * 1024

M_ROWS = BATCH * SEQ
SEG = 256
N_SEG = IN_COLS // SEG
FFN_TM = 512
FFN_FC = 256
PROJ_TM = 512
CAST_ROWS = 256
CAST_ROWS_OUT = D_FF // 2
DIL_JS = 4
DIFF_T = 512
LOG2E = math.log2(math.e)

SLOPES = tuple(float(np.float32(2.0) ** np.float32(-8.0 * i / N_ALIBI_HEADS)) for i in range(1, N_ALIBI_HEADS + 1))


def _rms_rows(x, gain):
    ms = jnp.mean(x * x, axis=-1, keepdims=True)
    return x * lax.rsqrt(ms + EPS) * gain


def _row_rsqrt(x):
    return lax.rsqrt(jnp.mean(x * x, axis=-1, keepdims=True) + EPS)


class _FfnWeightStager:
    def __init__(self, layer, gain_ref, win_hbm, wout_hbm, win_bf, wout_bf, stage_in, stage_out, sems):
        self.layer, self.gain_ref = layer, gain_ref
        self.win_hbm, self.wout_hbm, self.win_bf, self.wout_bf = win_hbm, wout_hbm, win_bf, wout_bf
        self.stage_in, self.stage_out, self.sems = stage_in, stage_out, sems

    def _copies(self, c):
        slot, lo = c % 2, c * FFN_FC
        return (
            pltpu.make_async_copy(self.win_hbm.at[self.layer, :, pl.ds(lo, FFN_FC)],
                                  self.stage_in.at[slot, 0], self.sems.at[slot, 0]),
            pltpu.make_async_copy(self.win_hbm.at[self.layer, :, pl.ds(D_FF + lo, FFN_FC)],
                                  self.stage_in.at[slot, 1], self.sems.at[slot, 1]),
            pltpu.make_async_copy(self.wout_hbm.at[self.layer, pl.ds(lo, FFN_FC), :],
                                  self.stage_out.at[slot], self.sems.at[slot, 2]),
        )

    def start(self, c):
        for cp in self._copies(c):
            cp.start()

    def finish(self, c):
        slot, lo = c % 2, c * FFN_FC
        for cp in self._copies(c):
            cp.wait()
        gain = jnp.concatenate([self.gain_ref[...]] * (FFN_FC // LANES), axis=1)
        self.win_bf[:, lo:lo + FFN_FC] = (self.stage_in[slot, 0] * gain).astype(BF16)
        self.win_bf[:, D_FF + lo:D_FF + lo + FFN_FC] = (self.stage_in[slot, 1] * gain).astype(BF16)
        self.wout_bf[lo:lo + FFN_FC, :] = self.stage_out[slot].astype(BF16)


def _ffn_rows(x, win_ref, wout_ref, a_scr, stager=None):
    h = x.astype(BF16)
    r = _row_rsqrt(x)
    n_chunks = D_FF // FFN_FC
    if stager is not None:
        stager.start(0)
    for c in range(n_chunks):
        lo = c * FFN_FC
        if stager is not None:
            if c + 1 < n_chunks:
                stager.start(c + 1)
            stager.finish(c)
        gate = jnp.dot(h, win_ref[:, lo:lo + FFN_FC], preferred_element_type=F32) * r
        up = jnp.dot(h, win_ref[:, D_FF + lo:D_FF + lo + FFN_FC], preferred_element_type=F32) * r
        a_scr[:, lo:lo + FFN_FC] = (gate * jax.nn.sigmoid(gate) * up).astype(BF16)
    y = jnp.dot(a_scr[...], wout_ref[...], preferred_element_type=F32)
    return x + 0.5 * y


def _ffn_tile(x_fn, o_ref, a_scr, stager):
    first = pl.program_id(0) == 0

    @pl.when(first)
    def _():
        o_ref[...] = _ffn_rows(x_fn(), stager.win_bf, stager.wout_bf, a_scr, stager)

    @pl.when(jnp.logical_not(first))
    def _():
        o_ref[...] = _ffn_rows(x_fn(), stager.win_bf, stager.wout_bf, a_scr)


def _ffn_kernel(x_ref, gain_ref, win_hbm, wout_hbm, o_ref, a_scr, *stage_refs, layer):
    _ffn_tile(lambda: x_ref[...], o_ref, a_scr, _FfnWeightStager(layer, gain_ref, win_hbm, wout_hbm, *stage_refs))


def _merge_ffn_kernel(x_ref, od_ref, of_ref, gate_ref, wa_ref, wb_ref, wo_ref, gain_ref, win_hbm, wout_hbm,
                      o_ref, a_scr, *stage_refs, layer):
    def merged():
        ya = jnp.dot(od_ref[...], wa_ref[...], preferred_element_type=F32)
        yb = jnp.dot(of_ref[...], wb_ref[...], preferred_element_type=F32)
        y = gate_ref[:, 0:D_MODEL].astype(F32) * ya + gate_ref[:, D_MODEL:2 * D_MODEL].astype(F32) * yb
        return x_ref[...] + jnp.dot(y.astype(BF16), wo_ref[...], preferred_element_type=F32)

    _ffn_tile(merged, o_ref, a_scr, _FfnWeightStager(layer, gain_ref, win_hbm, wout_hbm, *stage_refs))


def _resident(shape):
    return pl.BlockSpec(shape, lambda i: (0,) * len(shape), pipeline_mode=pl.Buffered(1))


def _resident_layer(shape, layer):
    return pl.BlockSpec((None,) + shape, lambda i: (layer,) + (0,) * len(shape), pipeline_mode=pl.Buffered(1))


def _ffn_weight_specs():
    return [_resident((D_MODEL, LANES)), pl.BlockSpec(memory_space=pl.ANY), pl.BlockSpec(memory_space=pl.ANY)]


def _ffn_scratch(tm):
    return [pltpu.VMEM((tm, D_FF), BF16),
            pltpu.VMEM((D_MODEL, 2 * D_FF), BF16), pltpu.VMEM((D_FF, D_MODEL), BF16),
            pltpu.VMEM((2, 2, D_MODEL, FFN_FC), F32), pltpu.VMEM((2, FFN_FC, D_MODEL), F32),
            pltpu.SemaphoreType.DMA((2, 3))]


def _lane_gain(norm_gain):
    return jnp.broadcast_to(norm_gain.reshape(D_MODEL, 1), (D_MODEL, LANES))


def _ffn(x, norm_gain, w_in, w_out, layer):
    tm = FFN_TM
    row = lambda i: (i, 0)
    return pl.pallas_call(
        functools.partial(_ffn_kernel, layer=layer),
        out_shape=jax.ShapeDtypeStruct((M_ROWS, D_MODEL), F32),
        grid=(M_ROWS // tm,),
        in_specs=[pl.BlockSpec((tm, D_MODEL), row)] + _ffn_weight_specs(),
        out_specs=pl.BlockSpec((tm, D_MODEL), row),
        scratch_shapes=_ffn_scratch(tm),
        compiler_params=pltpu.CompilerParams(
            dimension_semantics=("arbitrary",), vmem_limit_bytes=VMEM_LIMIT_BYTES),
        name="ffn",
    )(x, _lane_gain(norm_gain), w_in, w_out)


def _merge_ffn(x, o_dil, o_diff, gates, wa, wb, wo, norm_gain, w_in, w_out, layer):
    tm = FFN_TM
    row = lambda i: (i, 0)
    return pl.pallas_call(
        functools.partial(_merge_ffn_kernel, layer=layer),
        out_shape=jax.ShapeDtypeStruct((M_ROWS, D_MODEL), F32),
        grid=(M_ROWS // tm,),
        in_specs=[
            pl.BlockSpec((tm, D_MODEL), row),
            pl.BlockSpec((tm, DIL_OUT), row),
            pl.BlockSpec((tm, DIFF_V_WIDTH), row),
            pl.BlockSpec((tm, 2 * D_MODEL), row),
            _resident_layer((DIL_OUT, D_MODEL), layer),
            _resident_layer((DIFF_V_WIDTH, D_MODEL), layer),
            _resident_layer((D_MODEL, D_MODEL), layer),
        ] + _ffn_weight_specs(),
        out_specs=pl.BlockSpec((tm, D_MODEL), row),
        scratch_shapes=_ffn_scratch(tm),
        compiler_params=pltpu.CompilerParams(
            dimension_semantics=("arbitrary",), vmem_limit_bytes=VMEM_LIMIT_BYTES),
        name="merge_ffn",
    )(x, o_dil, o_diff, gates, wa, wb, wo, _lane_gain(norm_gain), w_in, w_out)


def _cast_kernel(w_ref, o_ref):
    o_ref[...] = w_ref[...].astype(BF16)


def _scaled_cast_kernel(w_ref, g_ref, o_ref):
    o_ref[...] = (w_ref[...] * g_ref[...]).astype(BF16)


def _cast_bf16(w, block_rows, row_gain=None):
    rows, cols = w.shape
    blk = pl.BlockSpec((block_rows, cols), lambda i: (i, 0))
    if row_gain is None:
        body, in_specs, args = _cast_kernel, [blk], (w,)
    else:
        body, in_specs, args = _scaled_cast_kernel, [blk, pl.BlockSpec((block_rows, 1), lambda i: (i, 0))], (w, row_gain)
    return pl.pallas_call(
        body,
        out_shape=jax.ShapeDtypeStruct((rows, cols), BF16),
        grid=(rows // block_rows,),
        in_specs=in_specs,
        out_specs=blk,
        compiler_params=pltpu.CompilerParams(
            dimension_semantics=("arbitrary",), vmem_limit_bytes=VMEM_LIMIT_BYTES),
        name="cast_bf16",
    )(*args)


def _w_in_src_seg(j):
    return jnp.where(j < 3 * N_DIL_GROUPS, (j % 3) * N_DIL_GROUPS + j // 3, j)


def _cast_permute_w_in(w, row_gain):
    rows = w.shape[0]
    return pl.pallas_call(
        _scaled_cast_kernel,
        out_shape=jax.ShapeDtypeStruct((rows, IN_COLS), BF16),
        grid=(N_SEG,),
        in_specs=[pl.BlockSpec((rows, SEG), lambda j: (0, _w_in_src_seg(j))),
                  pl.BlockSpec((rows, 1), lambda j: (0, 0))],
        out_specs=pl.BlockSpec((rows, SEG), lambda j: (0, j)),
        compiler_params=pltpu.CompilerParams(
            dimension_semantics=("arbitrary",), vmem_limit_bytes=VMEM_LIMIT_BYTES),
        name="cast_permute_w_in",
    )(w, row_gain)


def _proj_kernel(x_ref, w_ref, gain_ref,
                 o_g0, o_g1, o_g2, o_qd, o_kd, o_vd, o_gate, scr1, scr2, scr3):
    tm = PROJ_TM
    x = x_ref[...]
    h = x.astype(BF16)
    r = _row_rsqrt(x)
    lane_lo = lax.broadcasted_iota(jnp.int32, (1, LANES), 1) < HEAD_DIM

    def seg(i):
        return jnp.dot(h, w_ref[:, i * SEG:(i + 1) * SEG], preferred_element_type=F32) * r

    def qk_norm(y, gi):
        tiles = []
        for u in range(SEG // LANES):
            yt = y[:, u * LANES:(u + 1) * LANES]
            sq = yt * yt
            low = jnp.sum(jnp.where(lane_lo, sq, 0.0), axis=-1, keepdims=True)
            high = jnp.sum(jnp.where(lane_lo, 0.0, sq), axis=-1, keepdims=True)
            ssq = jnp.where(lane_lo, low, high)
            tiles.append(yt * lax.rsqrt(ssq + HEAD_DIM * EPS))
        return jnp.concatenate(tiles, axis=-1) * gain_ref[gi:gi + 1, :]

    o_g0[:, 0:SEG] = qk_norm(seg(0), 0).astype(BF16)
    o_g0[:, SEG:2 * SEG] = qk_norm(seg(1), 1).astype(BF16)
    o_g0[:, 2 * SEG:3 * SEG] = seg(2).astype(BF16)
    n_t = DIL_WIDTH // LANES
    quarter = tm // 4
    for g, (scr, o_g) in ((1, (scr1, o_g1)), (2, (scr2, o_g2))):
        ys = (qk_norm(seg(3 * g), 2 * g), qk_norm(seg(3 * g + 1), 2 * g + 1), seg(3 * g + 2))
        for t, y in enumerate(ys):
            for u in range(SEG // LANES):
                scr[2 * t + u] = y[:, u * LANES:(u + 1) * LANES]
        if g == 1:
            for c in range(4):
                for t in range(n_t):
                    o_g[0, c, :, t * LANES:(t + 1) * LANES] = scr[t, pl.ds(c, quarter, stride=4), :].astype(BF16)
        else:
            for c1 in range(4):
                for t in range(n_t):
                    scr3[t, c1 * quarter:(c1 + 1) * quarter, :] = scr[t, pl.ds(c1, quarter, stride=4), :]
            for c in range(16):
                c1, c2 = c % 4, c // 4
                for t in range(n_t):
                    o_g[0, c, :, t * LANES:(t + 1) * LANES] = (
                        scr3[t, pl.ds(c1 * quarter + c2, tm // 16, stride=4), :].astype(BF16))
    for t in range(2):
        o_qd[:, t * SEG:(t + 1) * SEG] = qk_norm(seg(9 + t), 6 + t).astype(BF16)
        o_kd[:, t * SEG:(t + 1) * SEG] = qk_norm(seg(11 + t), 8 + t).astype(BF16)
        o_vd[:, t * SEG:(t + 1) * SEG] = seg(13 + t).astype(BF16)
    for t in range(8):
        o_gate[:, t * SEG:(t + 1) * SEG] = jax.nn.sigmoid(seg(15 + t)).astype(BF16)


def _proj(x, w, qk_gain, layer):
    tm = PROJ_TM
    tiles_per_seq = SEQ // tm
    const = lambda i: (0, 0)
    row = lambda i: (i, 0)
    perm = lambda i: (i // tiles_per_seq, 0, i % tiles_per_seq, 0)
    out_shape = (
        jax.ShapeDtypeStruct((M_ROWS, DIL_WIDTH), BF16),
        jax.ShapeDtypeStruct((BATCH, 4, SEQ // 4, DIL_WIDTH), BF16),
        jax.ShapeDtypeStruct((BATCH, 16, SEQ // 16, DIL_WIDTH), BF16),
        jax.ShapeDtypeStruct((M_ROWS, DIFF_QK_WIDTH), BF16),
        jax.ShapeDtypeStruct((M_ROWS, DIFF_QK_WIDTH), BF16),
        jax.ShapeDtypeStruct((M_ROWS, DIFF_V_WIDTH), BF16),
        jax.ShapeDtypeStruct((M_ROWS, 2 * D_MODEL), BF16),
    )
    out_specs = (
        pl.BlockSpec((tm, DIL_WIDTH), row),
        pl.BlockSpec((1, 4, tm // 4, DIL_WIDTH), perm),
        pl.BlockSpec((1, 16, tm // 16, DIL_WIDTH), perm),
        pl.BlockSpec((tm, DIFF_QK_WIDTH), row),
        pl.BlockSpec((tm, DIFF_QK_WIDTH), row),
        pl.BlockSpec((tm, DIFF_V_WIDTH), row),
        pl.BlockSpec((tm, 2 * D_MODEL), row),
    )
    return pl.pallas_call(
        _proj_kernel,
        out_shape=out_shape,
        grid=(M_ROWS // tm,),
        in_specs=[
            pl.BlockSpec((tm, D_MODEL), row),
            _resident_layer((D_MODEL, IN_COLS), layer),
            pl.BlockSpec((16, SEG), const),
        ],
        out_specs=out_specs,
        scratch_shapes=[pltpu.VMEM((DIL_WIDTH // LANES, tm, LANES), F32)] * 3,
        compiler_params=pltpu.CompilerParams(
            dimension_semantics=("arbitrary",), vmem_limit_bytes=VMEM_LIMIT_BYTES),
        name="proj",
    )(x, w, qk_gain)


def _dil_block(ref, row_cur, row_prev, bias, lane_lo):
    q = ref[0, pl.ds(row_cur, BLOCK), 0:SEG]
    k = ref[0, pl.ds(row_cur, BLOCK), SEG:2 * SEG]
    v = ref[0, pl.ds(row_cur, BLOCK), 2 * SEG:3 * SEG]
    if row_prev is not None:
        k = jnp.concatenate([ref[0, pl.ds(row_prev, BLOCK), SEG:2 * SEG], k], axis=0)
        v = jnp.concatenate([ref[0, pl.ds(row_prev, BLOCK), 2 * SEG:3 * SEG], v], axis=0)
    zero = jnp.zeros((BLOCK, LANES), BF16)
    copies = []
    for j in range(HEADS_PER_GROUP):
        qt = q[:, (j // 2) * LANES:(j // 2 + 1) * LANES]
        kept = jnp.where(lane_lo, qt, zero) if j % 2 == 0 else jnp.where(lane_lo, zero, qt)
        copies.append(jnp.concatenate([kept, zero] if j < 2 else [zero, kept], axis=1))
    qs = jnp.concatenate(copies, axis=0)
    s = lax.dot_general(qs, k, (((1,), (1,)), ((), ())), preferred_element_type=F32) + bias
    m = jnp.max(s, axis=-1, keepdims=True)
    p = jnp.exp2(s - m)
    l = jnp.sum(p, axis=-1, keepdims=True)
    pv = jnp.dot(p.astype(BF16), v, preferred_element_type=F32)
    stats = []
    for u in range(SEG // LANES):
        ra, rb = slice(2 * u * BLOCK, (2 * u + 1) * BLOCK), slice((2 * u + 1) * BLOCK, (2 * u + 2) * BLOCK)
        cols = slice(u * LANES, (u + 1) * LANES)
        stats.append((jnp.where(lane_lo, pv[ra, cols], pv[rb, cols]),
                      jnp.where(lane_lo, m[ra], m[rb]), jnp.where(lane_lo, l[ra], l[rb])))
    return stats


def _dil_kernel(g0_ref, g1_ref, g2_ref, bias_ref, o_ref, o_scr, m_scr, l_scr):
    step = pl.program_id(1)
    lane_lo = lax.broadcasted_iota(jnp.int32, (1, LANES), 1) < HEAD_DIM

    def put(g, rows, stats):
        for u, (o, m, l) in enumerate(stats):
            o_scr[g, u, rows, :] = o
            m_scr[g, u, rows, :] = m
            l_scr[g, u, rows, :] = l

    for jj in range(DIL_JS):
        j = step * DIL_JS + jj
        row = pl.multiple_of(j * BLOCK, BLOCK)

        prev = pl.multiple_of(jnp.maximum(j - 1, 0) * BLOCK, BLOCK)
        tab = jnp.where(j == 0, N_DIL_GROUPS, 0) if jj == 0 else 0
        put(0, pl.ds(row, BLOCK), _dil_block(g0_ref, row, prev, bias_ref[tab], lane_lo))

        c, n = step * (DIL_JS // 4) + jj // 4, jj % 4
        if n == 0:
            stats = _dil_block(g1_ref, row, None, bias_ref[1, :, BLOCK:2 * BLOCK], lane_lo)
        else:
            prev = pl.multiple_of(row - BLOCK, BLOCK)
            stats = _dil_block(g1_ref, row, prev, bias_ref[1], lane_lo)
        put(1, pl.ds(n * (4 * BLOCK) + c, BLOCK, stride=4), stats)

        put(2, pl.ds(j, BLOCK, stride=16),
            _dil_block(g2_ref, row, None, bias_ref[2, :, BLOCK:2 * BLOCK], lane_lo))

    @pl.when(step == pl.num_programs(1) - 1)
    def _():
        rows = 256
        for r in range(SEQ // rows):
            sl = pl.ds(r * rows, rows)
            for u in range(SEG // LANES):
                m0, m1, m2 = m_scr[0, u, sl, :], m_scr[1, u, sl, :], m_scr[2, u, sl, :]
                mx = jnp.maximum(jnp.maximum(m0, m1), m2)
                w0, w1, w2 = jnp.exp2(m0 - mx), jnp.exp2(m1 - mx), jnp.exp2(m2 - mx)
                num = w0 * o_scr[0, u, sl, :] + w1 * o_scr[1, u, sl, :] + w2 * o_scr[2, u, sl, :]
                den = w0 * l_scr[0, u, sl, :] + w1 * l_scr[1, u, sl, :] + w2 * l_scr[2, u, sl, :]
                o_ref[0, sl, u * LANES:(u + 1) * LANES] = (num / den).astype(BF16)


def _dil_bias_table():
    qi = np.arange(BLOCK)[:, None]
    ki = np.arange(2 * BLOCK)[None, :]
    dist = BLOCK + qi - ki
    valid = (dist >= 0) & (dist <= BLOCK)
    tab = np.full((N_DIL_GROUPS + 1, HEADS_PER_GROUP * BLOCK, 2 * BLOCK), NEG, np.float32)
    for g, (_, dil) in enumerate(DIL_CONFIGS):
        for h in range(HEADS_PER_GROUP):
            slope = np.float32(SLOPES[g * HEADS_PER_GROUP + h])
            b = np.where(valid, -slope * np.float32(LOG2E) * (dil * dist).astype(np.float32),
                         np.float32(NEG)).astype(np.float32)
            tab[g, h * BLOCK:(h + 1) * BLOCK] = b
            if g == 0:
                tab[N_DIL_GROUPS, h * BLOCK:(h + 1) * BLOCK] = np.where(ki >= BLOCK, b, np.float32(NEG))
    return tab


def _dil_attn(g0, g1, g2, bias):
    n_blk = SEQ // BLOCK
    seq_spec = pl.BlockSpec((1, SEQ, DIL_WIDTH), lambda b, j: (b, 0, 0))
    return pl.pallas_call(
        _dil_kernel,
        out_shape=jax.ShapeDtypeStruct((BATCH, SEQ, DIL_OUT), BF16),
        grid=(BATCH, n_blk // DIL_JS),
        in_specs=[seq_spec, seq_spec, seq_spec,
                  pl.BlockSpec(bias.shape, lambda b, j: (0, 0, 0))],
        out_specs=pl.BlockSpec((1, SEQ, DIL_OUT), lambda b, j: (b, 0, 0)),
        scratch_shapes=[pltpu.VMEM((N_DIL_GROUPS, DIL_OUT // LANES, SEQ, LANES), F32)] * 3,
        compiler_params=pltpu.CompilerParams(
            dimension_semantics=("arbitrary", "arbitrary"), vmem_limit_bytes=VMEM_LIMIT_BYTES),
        name="dil_attn",
    )(g0, g1, g2, bias)


def _diff_kernel(q_ref, k_ref, v_ref, lq_ref, lk_ref, sn_ref, o_ref,
                 s_scr, mp_scr, m_scr, acc_scr, *, lam_init):
    t = DIFF_T
    n_lt = t // LANES
    pair = pl.program_id(1)
    qi = pl.program_id(2)
    q = q_ref[0]
    lane_map = lax.broadcasted_iota(jnp.int32, (1, SEG), 1) // HEAD_DIM
    zero = jnp.zeros_like(q)
    qs = [jnp.where(lane_map == g, q, zero) for g in range(4)]
    slopes = (jnp.where(pair == 0, SLOPES[N_DIL_HEADS] * LOG2E, SLOPES[N_DIL_HEADS + 2] * LOG2E).astype(F32),
              jnp.where(pair == 0, SLOPES[N_DIL_HEADS + 1] * LOG2E, SLOPES[N_DIL_HEADS + 3] * LOG2E).astype(F32))
    lane_k = lax.broadcasted_iota(jnp.int32, (1, t), 1)
    lane_v0 = lax.broadcasted_iota(jnp.int32, (1, SEG), 1) < DIFF_V_DIM

    def lane_tiles(x):
        return [x[:, u * LANES:(u + 1) * LANES] for u in range(x.shape[1] // LANES)]

    half = t // 2
    above = (lax.broadcasted_iota(jnp.int32, (half, half), 1)
             > lax.broadcasted_iota(jnp.int32, (half, half), 0))
    nt = (((1,), (1,)), ((), ()))

    def pass1(kb, diag, init):
        k = k_ref[0, kb * t:(kb + 1) * t, :]
        bias = [sl * (kb * t + lane_k).astype(F32) for sl in slopes]
        for g in range(4):
            rows = pl.ds(g * t, t)
            if diag:
                sa = lax.dot_general(qs[g][:half], k[:half], nt, preferred_element_type=F32)
                sa = jnp.where(above, NEG, sa + bias[g // 2][:, :half])
                sb = lax.dot_general(qs[g][half:], k, nt, preferred_element_type=F32) + bias[g // 2]
                sb = jnp.concatenate([sb[:, :half], jnp.where(above, NEG, sb[:, half:])], axis=1)
                s_scr[kb, pl.ds(g * t, half), 0:half] = sa
                s_scr[kb, pl.ds(g * t + half, half), :] = sb
                ta, tb = lane_tiles(sa), lane_tiles(sb)
                tmax = jnp.concatenate([jnp.maximum(ta[0], ta[1]),
                                        jnp.maximum(jnp.maximum(tb[0], tb[1]), jnp.maximum(tb[2], tb[3]))], axis=0)
            else:
                s = lax.dot_general(qs[g], k, nt, preferred_element_type=F32) + bias[g // 2]
                s_scr[kb, rows, :] = s
                tl = lane_tiles(s)
                tmax = jnp.maximum(jnp.maximum(tl[0], tl[1]), jnp.maximum(tl[2], tl[3]))
            if not init:
                tmax = jnp.maximum(mp_scr[rows, :], tmax)
            if diag:
                m_scr[rows, :] = jnp.broadcast_to(jnp.max(tmax, axis=-1, keepdims=True), (t, LANES))
            else:
                mp_scr[rows, :] = tmax

    lq = lq_ref[...]
    lk = lk_ref[...]
    lam = (jnp.exp(jnp.sum(lq[0:1] * lk[0:1], axis=-1, keepdims=True))
           - jnp.exp(jnp.sum(lq[1:2] * lk[1:2], axis=-1, keepdims=True)) + lam_init)

    def pass2(kb, init, last):
        v = v_ref[0, kb * t:(kb + 1) * t, :]
        one = jnp.ones_like(v)
        vh = (jnp.where(lane_v0, v, one), jnp.where(lane_v0, one, v))
        done = []
        for g in range(4):
            rows = pl.ds(g * t, t)
            m = m_scr[rows, :]
            if last:
                pa = jnp.exp2(s_scr[kb, pl.ds(g * t, half), 0:half]
                              - jnp.concatenate([m[:half]] * (half // LANES), axis=1))
                pb = jnp.exp2(s_scr[kb, pl.ds(g * t + half, half), :] - jnp.concatenate([m[half:]] * n_lt, axis=1))
                pv = jnp.concatenate(
                    [jnp.dot(pa.astype(BF16), vh[g // 2][:half], preferred_element_type=F32),
                     jnp.dot(pb.astype(BF16), vh[g // 2], preferred_element_type=F32)], axis=0)
            else:
                p = jnp.exp2(s_scr[kb, rows, :] - jnp.concatenate([m] * n_lt, axis=1))
                pv = jnp.dot(p.astype(BF16), vh[g // 2], preferred_element_type=F32)
            acc = pv if init else acc_scr[rows, :] + pv
            if not last:
                acc_scr[rows, :] = acc
                continue
            done.append(acc)
            if g % 2 == 1:
                hh = g // 2
                cols = slice(hh * DIFF_V_DIM, (hh + 1) * DIFF_V_DIM)
                sums = slice((1 - hh) * DIFF_V_DIM, (2 - hh) * DIFF_V_DIM)
                a1, a2 = done[g - 1], done[g]
                o = a1[:, cols] / a1[:, sums] - lam * (a2[:, cols] / a2[:, sums])
                o_ref[0, :, cols] = (_rms_rows(o, sn_ref[0, hh:hh + 1, :]) * (1.0 - lam_init)).astype(BF16)

    for nq in range(SEQ // t):
        @pl.when(qi == nq)
        def _(nq=nq):
            for kb in range(nq):
                pass1(kb, False, kb == 0)
            pass1(nq, True, nq == 0)

    for nq in range(SEQ // t):
        @pl.when(qi == nq)
        def _(nq=nq):
            for kb in range(nq):
                pass2(kb, kb == 0, False)
            pass2(nq, nq == 0, True)


def _diff_attn(qd, kd, vd, lq, lk, subnorm, lam_init):
    t = DIFF_T
    assert SEQ // t == 4
    kv_spec = pl.BlockSpec((1, SEQ, SEG), lambda b, p, i: (b, 0, p))
    q_spec = pl.BlockSpec((1, t, SEG), lambda b, p, i: (b, i, p))
    return pl.pallas_call(
        functools.partial(_diff_kernel, lam_init=lam_init),
        out_shape=jax.ShapeDtypeStruct((BATCH, SEQ, DIFF_V_WIDTH), BF16),
        grid=(BATCH, 2, SEQ // t),
        in_specs=[
            q_spec, kv_spec, kv_spec,
            pl.BlockSpec((2, HEAD_DIM), lambda b, p, i: (0, 0)),
            pl.BlockSpec((2, HEAD_DIM), lambda b, p, i: (0, 0)),
            pl.BlockSpec((1, 2, DIFF_V_DIM), lambda b, p, i: (p, 0, 0)),
        ],
        out_specs=q_spec,
        scratch_shapes=[pltpu.VMEM((SEQ // t, 4 * t, t), F32),
                        pltpu.VMEM((4 * t, LANES), F32),
                        pltpu.VMEM((4 * t, LANES), F32),
                        pltpu.VMEM((4 * t, SEG), F32)],
        compiler_params=pltpu.CompilerParams(
            dimension_semantics=("arbitrary", "arbitrary", "arbitrary"), vmem_limit_bytes=VMEM_LIMIT_BYTES),
        name="diff_attn",
    )(qd, kd, vd, lq, lk, subnorm)


def _qk_gain_rows(qk_gain_dil, qk_gain_diff):
    k_scale = HEAD_DIM ** 0.5
    q_scale = LOG2E
    rows = []
    for g in range(N_DIL_GROUPS):
        rows.append(qk_gain_dil[0, g].reshape(SEG) * q_scale)
        rows.append(qk_gain_dil[1, g].reshape(SEG) * k_scale)
    qd = qk_gain_diff[0].reshape(2, SEG) * q_scale
    kd = qk_gain_diff[1].reshape(2, SEG) * k_scale
    rows += [qd[0], qd[1], kd[0], kd[1]]
    rows += [jnp.zeros((SEG,), F32)] * (16 - len(rows))
    return jnp.stack(rows).astype(F32)


def kernel(x, ffn1_norm, ffn1_w_in, ffn1_w_out, mix_norm, w_in, qk_gain_dil, qk_gain_diff, lambda_q, lambda_k,
           diff_subnorm, w_branch_dil, w_branch_diff, w_out, ffn2_norm, ffn2_w_in, ffn2_w_out):
    b, s, d = x.shape
    assert (b, s, d) == (BATCH, SEQ, D_MODEL)
    xr = x.reshape(M_ROWS, D_MODEL)
    dil_bias = jnp.asarray(_dil_bias_table())
    w_in_p = _cast_permute_w_in(w_in.reshape(DEPTH * D_MODEL, IN_COLS),
                                mix_norm.reshape(DEPTH * D_MODEL, 1)).reshape(DEPTH, D_MODEL, IN_COLS)
    wa, wb, wo = w_branch_dil.astype(BF16), w_branch_diff.astype(BF16), w_out.astype(BF16)
    for l in range(DEPTH):
        lam_init = 0.8 - 0.6 * math.exp(-0.3 * l)
        xr = _ffn(xr, ffn1_norm[l], ffn1_w_in, ffn1_w_out, l)
        g0, g1, g2, qd, kd, vd, gates = _proj(xr, w_in_p, _qk_gain_rows(qk_gain_dil[l], qk_gain_diff[l]), l)
        o_dil = _dil_attn(g0.reshape(BATCH, SEQ, DIL_WIDTH), g1.reshape(BATCH, SEQ, DIL_WIDTH),
                          g2.reshape(BATCH, SEQ, DIL_WIDTH), dil_bias)
        o_diff = _diff_attn(qd.reshape(BATCH, SEQ, DIFF_QK_WIDTH), kd.reshape(BATCH, SEQ, DIFF_QK_WIDTH),
                            vd.reshape(BATCH, SEQ, DIFF_V_WIDTH), lambda_q[l], lambda_k[l],
                            diff_subnorm[l].reshape(2, 2, DIFF_V_DIM), lam_init)
        xr = _merge_ffn(xr, o_dil.reshape(M_ROWS, DIL_OUT), o_diff.reshape(M_ROWS, DIFF_V_WIDTH), gates,
                        wa, wb, wo, ffn2_norm[l], ffn2_w_in, ffn2_w_out, l)
    return xr.reshape(BATCH, SEQ, D_MODEL)
```

```python
import functools
import math

import jax
import jax.numpy as jnp
import numpy as np
from jax import lax
from jax.experimental import pallas as pl
from jax.experimental.pallas import tpu as pltpu

F32 = jnp.float32
BF16 = jnp.bfloat16

D_MODEL = 1024
BATCH = 8
SEQ = 2048
DEPTH = 2
HEAD_DIM = 64
DIL_CONFIGS = ((128, 1), (512, 4), (2048, 16))
N_DIL_GROUPS = 3
HEADS_PER_GROUP = 4
N_DIL_HEADS = 12
N_DIFF_HEADS = 4
DIFF_V_DIM = 128
N_ALIBI_HEADS = 16
D_FF = 2816
BLOCK = 128
EPS = 1e-6
DIL_WIDTH = 768
DIL_OUT = 256
DIFF_QK_WIDTH = 512
DIFF_V_WIDTH = 512
IN_COLS = 5888
NEG = -1e30

LANES = 128
VMEM_LIMIT_BYTES = 56 * 1024 * 1024

M_ROWS = BATCH * SEQ
SEG = 256
N_SEG = IN_COLS // SEG
FFN_TM = 512
FFN_FC = 256
PROJ_TM = 512
DIL_JS = 4
DIFF_T = 512
LOG2E = math.log2(math.e)

SLOPES = tuple(float(np.float32(2.0) ** np.float32(-8.0 * i / N_ALIBI_HEADS)) for i in range(1, N_ALIBI_HEADS + 1))


def _rms_rows(x, gain):
    ms = jnp.mean(x * x, axis=-1, keepdims=True)
    return x * lax.rsqrt(ms + EPS) * gain


def _row_rsqrt(x):
    return lax.rsqrt(jnp.mean(x * x, axis=-1, keepdims=True) + EPS)


class _FfnWeightStager:
    def __init__(self, layer, gain_ref, win_hbm, wout_hbm, win_bf, wout_bf, stage_in, stage_out, sems):
        self.layer, self.gain_ref = layer, gain_ref
        self.win_hbm, self.wout_hbm, self.win_bf, self.wout_bf = win_hbm, wout_hbm, win_bf, wout_bf
        self.stage_in, self.stage_out, self.sems = stage_in, stage_out, sems

    def _copies(self, c):
        slot, lo = c % 2, c * FFN_FC
        return (
            pltpu.make_async_copy(self.win_hbm.at[self.layer, :, pl.ds(lo, FFN_FC)],
                                  self.stage_in.at[slot, 0], self.sems.at[slot, 0]),
            pltpu.make_async_copy(self.win_hbm.at[self.layer, :, pl.ds(D_FF + lo, FFN_FC)],
                                  self.stage_in.at[slot, 1], self.sems.at[slot, 1]),
            pltpu.make_async_copy(self.wout_hbm.at[self.layer, pl.ds(lo, FFN_FC), :],
                                  self.stage_out.at[slot], self.sems.at[slot, 2]),
        )

    def start(self, c):
        for cp in self._copies(c):
            cp.start()

    def finish(self, c):
        slot, lo = c % 2, c * FFN_FC
        for cp in self._copies(c):
            cp.wait()
        gain = jnp.concatenate([self.gain_ref[...]] * (FFN_FC // LANES), axis=1)
        self.win_bf[:, lo:lo + FFN_FC] = (self.stage_in[slot, 0] * gain).astype(BF16)
        self.win_bf[:, D_FF + lo:D_FF + lo + FFN_FC] = (self.stage_in[slot, 1] * gain).astype(BF16)
        self.wout_bf[lo:lo + FFN_FC, :] = self.stage_out[slot].astype(BF16)


def _ffn_rows(x, win_ref, wout_ref, a_scr, stager=None):
    h = x.astype(BF16)
    r = _row_rsqrt(x)
    n_chunks = D_FF // FFN_FC
    if stager is not None:
        stager.start(0)
    for c in range(n_chunks):
        lo = c * FFN_FC
        if stager is not None:
            if c + 1 < n_chunks:
                stager.start(c + 1)
            stager.finish(c)
        gate = jnp.dot(h, win_ref[:, lo:lo + FFN_FC], preferred_element_type=F32) * r
        up = jnp.dot(h, win_ref[:, D_FF + lo:D_FF + lo + FFN_FC], preferred_element_type=F32) * r
        a_scr[:, lo:lo + FFN_FC] = (gate * jax.nn.sigmoid(gate) * up).astype(BF16)
    y = jnp.dot(a_scr[...], wout_ref[...], preferred_element_type=F32)
    return x + 0.5 * y


def _ffn_tile(x_fn, o_ref, a_scr, stager):
    first = pl.program_id(0) == 0

    @pl.when(first)
    def _():
        o_ref[...] = _ffn_rows(x_fn(), stager.win_bf, stager.wout_bf, a_scr, stager)

    @pl.when(jnp.logical_not(first))
    def _():
        o_ref[...] = _ffn_rows(x_fn(), stager.win_bf, stager.wout_bf, a_scr)


def _proj_src_col(i):
    if i < 3 * N_DIL_GROUPS:
        return (i % 3) * DIL_WIDTH + (i // 3) * SEG
    return i * SEG


def _ffn_kernel(x_ref, gain_ref, win_hbm, wout_hbm, pw_ref, pgain_ref, o_ref, pw_bf_ref, a_scr, *stage_refs, layer):
    pgain = jnp.concatenate([pgain_ref[...]] * (SEG // LANES), axis=1)
    for j in range(N_SEG):
        src = _proj_src_col(j)
        pw_bf_ref[:, j * SEG:(j + 1) * SEG] = (pw_ref[:, src:src + SEG] * pgain).astype(BF16)
    _ffn_tile(lambda: x_ref[...], o_ref, a_scr, _FfnWeightStager(layer, gain_ref, win_hbm, wout_hbm, *stage_refs))


def _merge_ffn_kernel(x_ref, od_ref, of_ref, gate_ref, wa_ref, wb_ref, wo_ref, gain_ref, win_hbm, wout_hbm,
                      o_ref, a_scr, *stage_refs, layer):
    def merged():
        ya = jnp.dot(od_ref[...], wa_ref[...], preferred_element_type=F32)
        yb = jnp.dot(of_ref[...], wb_ref[...], preferred_element_type=F32)
        y = gate_ref[:, 0:D_MODEL].astype(F32) * ya + gate_ref[:, D_MODEL:2 * D_MODEL].astype(F32) * yb
        return x_ref[...] + jnp.dot(y.astype(BF16), wo_ref[...], preferred_element_type=F32)

    _ffn_tile(merged, o_ref, a_scr, _FfnWeightStager(layer, gain_ref, win_hbm, wout_hbm, *stage_refs))


def _resident(shape):
    return pl.BlockSpec(shape, lambda i: (0,) * len(shape), pipeline_mode=pl.Buffered(1))


def _resident_layer(shape, layer):
    return pl.BlockSpec((None,) + shape, lambda i: (layer,) + (0,) * len(shape), pipeline_mode=pl.Buffered(1))


def _ffn_weight_specs():
    return [_resident((D_MODEL, LANES)), pl.BlockSpec(memory_space=pl.ANY), pl.BlockSpec(memory_space=pl.ANY)]


def _ffn_scratch(tm):
    return [pltpu.VMEM((tm, D_FF), BF16),
            pltpu.VMEM((D_MODEL, 2 * D_FF), BF16), pltpu.VMEM((D_FF, D_MODEL), BF16),
            pltpu.VMEM((2, 2, D_MODEL, FFN_FC), F32), pltpu.VMEM((2, FFN_FC, D_MODEL), F32),
            pltpu.SemaphoreType.DMA((2, 3))]


def _lane_gain(norm_gain):
    return jnp.broadcast_to(norm_gain.reshape(D_MODEL, 1), (D_MODEL, LANES))


def _ffn(x, norm_gain, w_in, w_out, proj_w, proj_norm_gain, layer):
    tm = FFN_TM
    n_steps = M_ROWS // tm
    pw_rows = D_MODEL // n_steps
    row = lambda i: (i, 0)
    return pl.pallas_call(
        functools.partial(_ffn_kernel, layer=layer),
        out_shape=(jax.ShapeDtypeStruct((M_ROWS, D_MODEL), F32), jax.ShapeDtypeStruct((D_MODEL, IN_COLS), BF16)),
        grid=(n_steps,),
        in_specs=[pl.BlockSpec((tm, D_MODEL), row)] + _ffn_weight_specs() + [
            pl.BlockSpec((None, pw_rows, IN_COLS), lambda i: (layer, i, 0)),
            pl.BlockSpec((pw_rows, LANES), row)],
        out_specs=(pl.BlockSpec((tm, D_MODEL), row), pl.BlockSpec((pw_rows, IN_COLS), row)),
        scratch_shapes=_ffn_scratch(tm),
        compiler_params=pltpu.CompilerParams(
            dimension_semantics=("arbitrary",), vmem_limit_bytes=VMEM_LIMIT_BYTES),
        name="ffn",
    )(x, _lane_gain(norm_gain), w_in, w_out, proj_w, _lane_gain(proj_norm_gain))


def _merge_ffn(x, o_dil, o_diff, gates, wa, wb, wo, norm_gain, w_in, w_out, layer):
    tm = FFN_TM
    row = lambda i: (i, 0)
    return pl.pallas_call(
        functools.partial(_merge_ffn_kernel, layer=layer),
        out_shape=jax.ShapeDtypeStruct((M_ROWS, D_MODEL), F32),
        grid=(M_ROWS // tm,),
        in_specs=[
            pl.BlockSpec((tm, D_MODEL), row),
            pl.BlockSpec((tm, DIL_OUT), row),
            pl.BlockSpec((tm, DIFF_V_WIDTH), row),
            pl.BlockSpec((tm, 2 * D_MODEL), row),
            _resident_layer((DIL_OUT, D_MODEL), layer),
            _resident_layer((DIFF_V_WIDTH, D_MODEL), layer),
            _resident_layer((D_MODEL, D_MODEL), layer),
        ] + _ffn_weight_specs(),
        out_specs=pl.BlockSpec((tm, D_MODEL), row),
        scratch_shapes=_ffn_scratch(tm),
        compiler_params=pltpu.CompilerParams(
            dimension_semantics=("arbitrary",), vmem_limit_bytes=VMEM_LIMIT_BYTES),
        name="merge_ffn",
    )(x, o_dil, o_diff, gates, wa, wb, wo, _lane_gain(norm_gain), w_in, w_out)


def _proj_kernel(x_ref, w_ref, gain_ref,
                 o_g0, o_g1, o_g2, o_qd, o_kd, o_vd, o_gate, scr1, scr2, scr3):
    tm = PROJ_TM
    x = x_ref[...]
    h = x.astype(BF16)
    r = _row_rsqrt(x)
    lane_lo = lax.broadcasted_iota(jnp.int32, (1, LANES), 1) < HEAD_DIM

    def seg(i):
        return jnp.dot(h, w_ref[:, i * SEG:(i + 1) * SEG], preferred_element_type=F32) * r

    def qk_norm(y, gi):
        tiles = []
        for u in range(SEG // LANES):
            yt = y[:, u * LANES:(u + 1) * LANES]
            sq = yt * yt
            low = jnp.sum(jnp.where(lane_lo, sq, 0.0), axis=-1, keepdims=True)
            high = jnp.sum(jnp.where(lane_lo, 0.0, sq), axis=-1, keepdims=True)
            ssq = jnp.where(lane_lo, low, high)
            tiles.append(yt * lax.rsqrt(ssq + HEAD_DIM * EPS))
        return jnp.concatenate(tiles, axis=-1) * gain_ref[gi:gi + 1, :]

    o_g0[:, 0:SEG] = qk_norm(seg(0), 0).astype(BF16)
    o_g0[:, SEG:2 * SEG] = qk_norm(seg(1), 1).astype(BF16)
    o_g0[:, 2 * SEG:3 * SEG] = seg(2).astype(BF16)
    n_t = DIL_WIDTH // LANES
    quarter = tm // 4
    for g, (scr, o_g) in ((1, (scr1, o_g1)), (2, (scr2, o_g2))):
        ys = (qk_norm(seg(3 * g), 2 * g), qk_norm(seg(3 * g + 1), 2 * g + 1), seg(3 * g + 2))
        for t, y in enumerate(ys):
            for u in range(SEG // LANES):
                scr[2 * t + u] = y[:, u * LANES:(u + 1) * LANES]
        if g == 1:
            for c in range(4):
                for t in range(n_t):
                    o_g[0, c, :, t * LANES:(t + 1) * LANES] = scr[t, pl.ds(c, quarter, stride=4), :].astype(BF16)
        else:
            for c1 in range(4):
                for t in range(n_t):
                    scr3[t, c1 * quarter:(c1 + 1) * quarter, :] = scr[t, pl.ds(c1, quarter, stride=4), :]
            for c in range(16):
                c1, c2 = c % 4, c // 4
                for t in range(n_t):
                    o_g[0, c, :, t * LANES:(t + 1) * LANES] = (
                        scr3[t, pl.ds(c1 * quarter + c2, tm // 16, stride=4), :].astype(BF16))
    for t in range(2):
        o_qd[:, t * SEG:(t + 1) * SEG] = qk_norm(seg(9 + t), 6 + t).astype(BF16)
        o_kd[:, t * SEG:(t + 1) * SEG] = qk_norm(seg(11 + t), 8 + t).astype(BF16)
        o_vd[:, t * SEG:(t + 1) * SEG] = seg(13 + t).astype(BF16)
    for t in range(8):
        o_gate[:, t * SEG:(t + 1) * SEG] = jax.nn.sigmoid(seg(15 + t)).astype(BF16)


def _proj(x, w, qk_gain):
    tm = PROJ_TM
    tiles_per_seq = SEQ // tm
    const = lambda i: (0, 0)
    row = lambda i: (i, 0)
    perm = lambda i: (i // tiles_per_seq, 0, i % tiles_per_seq, 0)
    out_shape = (
        jax.ShapeDtypeStruct((M_ROWS, DIL_WIDTH), BF16),
        jax.ShapeDtypeStruct((BATCH, 4, SEQ // 4, DIL_WIDTH), BF16),
        jax.ShapeDtypeStruct((BATCH, 16, SEQ // 16, DIL_WIDTH), BF16),
        jax.ShapeDtypeStruct((M_ROWS, DIFF_QK_WIDTH), BF16),
        jax.ShapeDtypeStruct((M_ROWS, DIFF_QK_WIDTH), BF16),
        jax.ShapeDtypeStruct((M_ROWS, DIFF_V_WIDTH), BF16),
        jax.ShapeDtypeStruct((M_ROWS, 2 * D_MODEL), BF16),
    )
    out_specs = (
        pl.BlockSpec((tm, DIL_WIDTH), row),
        pl.BlockSpec((1, 4, tm // 4, DIL_WIDTH), perm),
        pl.BlockSpec((1, 16, tm // 16, DIL_WIDTH), perm),
        pl.BlockSpec((tm, DIFF_QK_WIDTH), row),
        pl.BlockSpec((tm, DIFF_QK_WIDTH), row),
        pl.BlockSpec((tm, DIFF_V_WIDTH), row),
        pl.BlockSpec((tm, 2 * D_MODEL), row),
    )
    return pl.pallas_call(
        _proj_kernel,
        out_shape=out_shape,
        grid=(M_ROWS // tm,),
        in_specs=[
            pl.BlockSpec((tm, D_MODEL), row),
            _resident((D_MODEL, IN_COLS)),
            pl.BlockSpec((16, SEG), const),
        ],
        out_specs=out_specs,
        scratch_shapes=[pltpu.VMEM((DIL_WIDTH // LANES, tm, LANES), F32)] * 3,
        compiler_params=pltpu.CompilerParams(
            dimension_semantics=("arbitrary",), vmem_limit_bytes=VMEM_LIMIT_BYTES),
        name="proj",
    )(x, w, qk_gain)


def _dil_block(ref, row_cur, row_prev, bias, lane_lo):
    q = ref[0, pl.ds(row_cur, BLOCK), 0:SEG]
    k = ref[0, pl.ds(row_cur, BLOCK), SEG:2 * SEG]
    v = ref[0, pl.ds(row_cur, BLOCK), 2 * SEG:3 * SEG]
    if row_prev is not None:
        k = jnp.concatenate([ref[0, pl.ds(row_prev, BLOCK), SEG:2 * SEG], k], axis=0)
        v = jnp.concatenate([ref[0, pl.ds(row_prev, BLOCK), 2 * SEG:3 * SEG], v], axis=0)
    zero = jnp.zeros((BLOCK, LANES), BF16)
    copies = []
    for j in range(HEADS_PER_GROUP):
        qt = q[:, (j // 2) * LANES:(j // 2 + 1) * LANES]
        kept = jnp.where(lane_lo, qt, zero) if j % 2 == 0 else jnp.where(lane_lo, zero, qt)
        copies.append(jnp.concatenate([kept, zero] if j < 2 else [zero, kept], axis=1))
    qs = jnp.concatenate(copies, axis=0)
    s = lax.dot_general(qs, k, (((1,), (1,)), ((), ())), preferred_element_type=F32) + bias
    m = jnp.max(s, axis=-1, keepdims=True)
    p = jnp.exp2(s - m)
    l = jnp.sum(p, axis=-1, keepdims=True)
    pv = jnp.dot(p.astype(BF16), v, preferred_element_type=F32)
    stats = []
    for u in range(SEG // LANES):
        ra, rb = slice(2 * u * BLOCK, (2 * u + 1) * BLOCK), slice((2 * u + 1) * BLOCK, (2 * u + 2) * BLOCK)
        cols = slice(u * LANES, (u + 1) * LANES)
        stats.append((jnp.where(lane_lo, pv[ra, cols], pv[rb, cols]),
                      jnp.where(lane_lo, m[ra], m[rb]), jnp.where(lane_lo, l[ra], l[rb])))
    return stats


def _dil_kernel(g0_ref, g1_ref, g2_ref, bias_ref, o_ref, o_scr, m_scr, l_scr):
    step = pl.program_id(1)
    lane_lo = lax.broadcasted_iota(jnp.int32, (1, LANES), 1) < HEAD_DIM

    def put(g, rows, stats):
        for u, (o, m, l) in enumerate(stats):
            o_scr[g, u, rows, :] = o
            m_scr[g, u, rows, :] = m
            l_scr[g, u, rows, :] = l

    for jj in range(DIL_JS):
        j = step * DIL_JS + jj
        row = pl.multiple_of(j * BLOCK, BLOCK)

        prev = pl.multiple_of(jnp.maximum(j - 1, 0) * BLOCK, BLOCK)
        tab = jnp.where(j == 0, N_DIL_GROUPS, 0) if jj == 0 else 0
        put(0, pl.ds(row, BLOCK), _dil_block(g0_ref, row, prev, bias_ref[tab], lane_lo))

        c, n = step * (DIL_JS // 4) + jj // 4, jj % 4
        if n == 0:
            stats = _dil_block(g1_ref, row, None, bias_ref[1, :, BLOCK:2 * BLOCK], lane_lo)
        else:
            prev = pl.multiple_of(row - BLOCK, BLOCK)
            stats = _dil_block(g1_ref, row, prev, bias_ref[1], lane_lo)
        put(1, pl.ds(n * (4 * BLOCK) + c, BLOCK, stride=4), stats)

        put(2, pl.ds(j, BLOCK, stride=16),
            _dil_block(g2_ref, row, None, bias_ref[2, :, BLOCK:2 * BLOCK], lane_lo))

    @pl.when(step == pl.num_programs(1) - 1)
    def _():
        rows = 256
        for r in range(SEQ // rows):
            sl = pl.ds(r * rows, rows)
            for u in range(SEG // LANES):
                m0, m1, m2 = m_scr[0, u, sl, :], m_scr[1, u, sl, :], m_scr[2, u, sl, :]
                mx = jnp.maximum(jnp.maximum(m0, m1), m2)
                w0, w1, w2 = jnp.exp2(m0 - mx), jnp.exp2(m1 - mx), jnp.exp2(m2 - mx)
                num = w0 * o_scr[0, u, sl, :] + w1 * o_scr[1, u, sl, :] + w2 * o_scr[2, u, sl, :]
                den = w0 * l_scr[0, u, sl, :] + w1 * l_scr[1, u, sl, :] + w2 * l_scr[2, u, sl, :]
                o_ref[0, sl, u * LANES:(u + 1) * LANES] = (num / den).astype(BF16)


def _dil_bias_table():
    qi = np.arange(BLOCK)[:, None]
    ki = np.arange(2 * BLOCK)[None, :]
    dist = BLOCK + qi - ki
    valid = (dist >= 0) & (dist <= BLOCK)
    tab = np.full((N_DIL_GROUPS + 1, HEADS_PER_GROUP * BLOCK, 2 * BLOCK), NEG, np.float32)
    for g, (_, dil) in enumerate(DIL_CONFIGS):
        for h in range(HEADS_PER_GROUP):
            slope = np.float32(SLOPES[g * HEADS_PER_GROUP + h])
            b = np.where(valid, -slope * np.float32(LOG2E) * (dil * dist).astype(np.float32),
                         np.float32(NEG)).astype(np.float32)
            tab[g, h * BLOCK:(h + 1) * BLOCK] = b
            if g == 0:
                tab[N_DIL_GROUPS, h * BLOCK:(h + 1) * BLOCK] = np.where(ki >= BLOCK, b, np.float32(NEG))
    return tab


def _dil_attn(g0, g1, g2, bias):
    n_blk = SEQ // BLOCK
    seq_spec = pl.BlockSpec((1, SEQ, DIL_WIDTH), lambda b, j: (b, 0, 0))
    return pl.pallas_call(
        _dil_kernel,
        out_shape=jax.ShapeDtypeStruct((BATCH, SEQ, DIL_OUT), BF16),
        grid=(BATCH, n_blk // DIL_JS),
        in_specs=[seq_spec, seq_spec, seq_spec,
                  pl.BlockSpec(bias.shape, lambda b, j: (0, 0, 0))],
        out_specs=pl.BlockSpec((1, SEQ, DIL_OUT), lambda b, j: (b, 0, 0)),
        scratch_shapes=[pltpu.VMEM((N_DIL_GROUPS, DIL_OUT // LANES, SEQ, LANES), F32)] * 3,
        compiler_params=pltpu.CompilerParams(
            dimension_semantics=("arbitrary", "arbitrary"), vmem_limit_bytes=VMEM_LIMIT_BYTES),
        name="dil_attn",
    )(g0, g1, g2, bias)


def _diff_kernel(q_ref, k_ref, v_ref, lq_ref, lk_ref, sn_ref, o_ref,
                 s_scr, mp_scr, m_scr, acc_scr, *, lam_init):
    t = DIFF_T
    n_lt = t // LANES
    pair = pl.program_id(1)
    qi = pl.program_id(2)
    q = q_ref[0]
    lane_map = lax.broadcasted_iota(jnp.int32, (1, SEG), 1) // HEAD_DIM
    zero = jnp.zeros_like(q)
    qs = [jnp.where(lane_map == g, q, zero) for g in range(4)]
    slopes = (jnp.where(pair == 0, SLOPES[N_DIL_HEADS] * LOG2E, SLOPES[N_DIL_HEADS + 2] * LOG2E).astype(F32),
              jnp.where(pair == 0, SLOPES[N_DIL_HEADS + 1] * LOG2E, SLOPES[N_DIL_HEADS + 3] * LOG2E).astype(F32))
    lane_k = lax.broadcasted_iota(jnp.int32, (1, t), 1)
    lane_v0 = lax.broadcasted_iota(jnp.int32, (1, SEG), 1) < DIFF_V_DIM

    def lane_tiles(x):
        return [x[:, u * LANES:(u + 1) * LANES] for u in range(x.shape[1] // LANES)]

    half = t // 2
    above = (lax.broadcasted_iota(jnp.int32, (half, half), 1)
             > lax.broadcasted_iota(jnp.int32, (half, half), 0))
    nt = (((1,), (1,)), ((), ()))

    def pass1(kb, diag, init):
        k = k_ref[0, kb * t:(kb + 1) * t, :]
        bias = [sl * (kb * t + lane_k).astype(F32) for sl in slopes]
        for g in range(4):
            rows = pl.ds(g * t, t)
            if diag:
                sa = lax.dot_general(qs[g][:half], k[:half], nt, preferred_element_type=F32)
                sa = jnp.where(above, NEG, sa + bias[g // 2][:, :half])
                sb = lax.dot_general(qs[g][half:], k, nt, preferred_element_type=F32) + bias[g // 2]
                sb = jnp.concatenate([sb[:, :half], jnp.where(above, NEG, sb[:, half:])], axis=1)
                s_scr[kb, pl.ds(g * t, half), 0:half] = sa
                s_scr[kb, pl.ds(g * t + half, half), :] = sb
                ta, tb = lane_tiles(sa), lane_tiles(sb)
                tmax = jnp.concatenate([jnp.maximum(ta[0], ta[1]),
                                        jnp.maximum(jnp.maximum(tb[0], tb[1]), jnp.maximum(tb[2], tb[3]))], axis=0)
            else:
                s = lax.dot_general(qs[g], k, nt, preferred_element_type=F32) + bias[g // 2]
                s_scr[kb, rows, :] = s
                tl = lane_tiles(s)
                tmax = jnp.maximum(jnp.maximum(tl[0], tl[1]), jnp.maximum(tl[2], tl[3]))
            if not init:
                tmax = jnp.maximum(mp_scr[rows, :], tmax)
            if diag:
                m_scr[rows, :] = jnp.broadcast_to(jnp.max(tmax, axis=-1, keepdims=True), (t, LANES))
            else:
                mp_scr[rows, :] = tmax

    lq = lq_ref[...]
    lk = lk_ref[...]
    lam = (jnp.exp(jnp.sum(lq[0:1] * lk[0:1], axis=-1, keepdims=True))
           - jnp.exp(jnp.sum(lq[1:2] * lk[1:2], axis=-1, keepdims=True)) + lam_init)

    def pass2(kb, init, last):
        v = v_ref[0, kb * t:(kb + 1) * t, :]
        one = jnp.ones_like(v)
        vh = (jnp.where(lane_v0, v, one), jnp.where(lane_v0, one, v))
        done = []
        for g in range(4):
            rows = pl.ds(g * t, t)
            m = m_scr[rows, :]
            if last:
                pa = jnp.exp2(s_scr[kb, pl.ds(g * t, half), 0:half]
                              - jnp.concatenate([m[:half]] * (half // LANES), axis=1))
                pb = jnp.exp2(s_scr[kb, pl.ds(g * t + half, half), :] - jnp.concatenate([m[half:]] * n_lt, axis=1))
                pv = jnp.concatenate(
                    [jnp.dot(pa.astype(BF16), vh[g // 2][:half], preferred_element_type=F32),
                     jnp.dot(pb.astype(BF16), vh[g // 2], preferred_element_type=F32)], axis=0)
            else:
                p = jnp.exp2(s_scr[kb, rows, :] - jnp.concatenate([m] * n_lt, axis=1))
                pv = jnp.dot(p.astype(BF16), vh[g // 2], preferred_element_type=F32)
            acc = pv if init else acc_scr[rows, :] + pv
            if not last:
                acc_scr[rows, :] = acc
                continue
            done.append(acc)
            if g % 2 == 1:
                hh = g // 2
                cols = slice(hh * DIFF_V_DIM, (hh + 1) * DIFF_V_DIM)
                sums = slice((1 - hh) * DIFF_V_DIM, (2 - hh) * DIFF_V_DIM)
                a1, a2 = done[g - 1], done[g]
                o = a1[:, cols] / a1[:, sums] - lam * (a2[:, cols] / a2[:, sums])
                o_ref[0, :, cols] = (_rms_rows(o, sn_ref[0, hh:hh + 1, :]) * (1.0 - lam_init)).astype(BF16)

    for nq in range(SEQ // t):
        @pl.when(qi == nq)
        def _(nq=nq):
            for kb in range(nq):
                pass1(kb, False, kb == 0)
            pass1(nq, True, nq == 0)

    for nq in range(SEQ // t):
        @pl.when(qi == nq)
        def _(nq=nq):
            for kb in range(nq):
                pass2(kb, kb == 0, False)
            pass2(nq, nq == 0, True)


def _diff_attn(qd, kd, vd, lq, lk, subnorm, lam_init):
    t = DIFF_T
    assert SEQ // t == 4
    kv_spec = pl.BlockSpec((1, SEQ, SEG), lambda b, p, i: (b, 0, p))
    q_spec = pl.BlockSpec((1, t, SEG), lambda b, p, i: (b, i, p))
    return pl.pallas_call(
        functools.partial(_diff_kernel, lam_init=lam_init),
        out_shape=jax.ShapeDtypeStruct((BATCH, SEQ, DIFF_V_WIDTH), BF16),
        grid=(BATCH, 2, SEQ // t),
        in_specs=[
            q_spec, kv_spec, kv_spec,
            pl.BlockSpec((2, HEAD_DIM), lambda b, p, i: (0, 0)),
            pl.BlockSpec((2, HEAD_DIM), lambda b, p, i: (0, 0)),
            pl.BlockSpec((1, 2, DIFF_V_DIM), lambda b, p, i: (p, 0, 0)),
        ],
        out_specs=q_spec,
        scratch_shapes=[pltpu.VMEM((SEQ // t, 4 * t, t), F32),
                        pltpu.VMEM((4 * t, LANES), F32),
                        pltpu.VMEM((4 * t, LANES), F32),
                        pltpu.VMEM((4 * t, SEG), F32)],
        compiler_params=pltpu.CompilerParams(
            dimension_semantics=("arbitrary", "arbitrary", "arbitrary"), vmem_limit_bytes=VMEM_LIMIT_BYTES),
        name="diff_attn",
    )(qd, kd, vd, lq, lk, subnorm)


def _qk_gain_rows(qk_gain_dil, qk_gain_diff):
    k_scale = HEAD_DIM ** 0.5
    q_scale = LOG2E
    rows = []
    for g in range(N_DIL_GROUPS):
        rows.append(qk_gain_dil[0, g].reshape(SEG) * q_scale)
        rows.append(qk_gain_dil[1, g].reshape(SEG) * k_scale)
    qd = qk_gain_diff[0].reshape(2, SEG) * q_scale
    kd = qk_gain_diff[1].reshape(2, SEG) * k_scale
    rows += [qd[0], qd[1], kd[0], kd[1]]
    rows += [jnp.zeros((SEG,), F32)] * (16 - len(rows))
    return jnp.stack(rows).astype(F32)


def kernel(x, ffn1_norm, ffn1_w_in, ffn1_w_out, mix_norm, w_in, qk_gain_dil, qk_gain_diff, lambda_q, lambda_k,
           diff_subnorm, w_branch_dil, w_branch_diff, w_out, ffn2_norm, ffn2_w_in, ffn2_w_out):
    b, s, d = x.shape
    assert (b, s, d) == (BATCH, SEQ, D_MODEL)
    xr = x.reshape(M_ROWS, D_MODEL)
    dil_bias = jnp.asarray(_dil_bias_table())
    wa, wb, wo = w_branch_dil.astype(BF16), w_branch_diff.astype(BF16), w_out.astype(BF16)
    for l in range(DEPTH):
        lam_init = 0.8 - 0.6 * math.exp(-0.3 * l)
        xr, proj_w = _ffn(xr, ffn1_norm[l], ffn1_w_in, ffn1_w_out, w_in, mix_norm[l], l)
        g0, g1, g2, qd, kd, vd, gates = _proj(xr, proj_w, _qk_gain_rows(qk_gain_dil[l], qk_gain_diff[l]))
        o_dil = _dil_attn(g0.reshape(BATCH, SEQ, DIL_WIDTH), g1.reshape(BATCH, SEQ, DIL_WIDTH),
                          g2.reshape(BATCH, SEQ, DIL_WIDTH), dil_bias)
        o_diff = _diff_attn(qd.reshape(BATCH, SEQ, DIFF_QK_WIDTH), kd.reshape(BATCH, SEQ, DIFF_QK_WIDTH),
                            vd.reshape(BATCH, SEQ, DIFF_V_WIDTH), lambda_q[l], lambda_k[l],
                            diff_subnorm[l].reshape(2, 2, DIFF_V_DIM), lam_init)
        xr = _merge_ffn(xr, o_dil.reshape(M_ROWS, DIL_OUT), o_diff.reshape(M_ROWS, DIFF_V_WIDTH), gates,
                        wa, wb, wo, ffn2_norm[l], ffn2_w_in, ffn2_w_out, l)
    return xr.reshape(BATCH, SEQ, D_MODEL)
```

```python
import functools
import math

import jax
import jax.numpy as jnp
import numpy as np
from jax import lax
from jax.experimental import pallas as pl
from jax.experimental.pallas import tpu as pltpu

F32 = jnp.float32
BF16 = jnp.bfloat16

D_MODEL = 1024
BATCH = 8
SEQ = 2048
DEPTH = 2
HEAD_DIM = 64
DIL_CONFIGS = ((128, 1), (512, 4), (2048, 16))
N_DIL_GROUPS = 3
HEADS_PER_GROUP = 4
N_DIL_HEADS = 12
N_DIFF_HEADS = 4
DIFF_V_DIM = 128
N_ALIBI_HEADS = 16
D_FF = 2816
BLOCK = 128
EPS = 1e-6
DIL_WIDTH = 768
DIL_OUT = 256
DIFF_QK_WIDTH = 512
DIFF_V_WIDTH = 512
IN_COLS = 5888
NEG = -1e30

LANES = 128
VMEM_LIMIT_BYTES = 60 * 1024 * 1024

M_ROWS = BATCH * SEQ
SEG = 256
N_SEG = IN_COLS // SEG
FFN_TM = 1024
MERGE_TM = 512
FFN_FC = 256
PROJ_TM = 512
DIL_JS = 8
DIFF_T = 512
LOG2E = math.log2(math.e)

SLOPES = tuple(float(np.float32(2.0) ** np.float32(-8.0 * i / N_ALIBI_HEADS)) for i in range(1, N_ALIBI_HEADS + 1))


def _rms_rows(x, gain):
    ms = jnp.mean(x * x, axis=-1, keepdims=True)
    return x * lax.rsqrt(ms + EPS) * gain


def _row_rsqrt(x):
    return lax.rsqrt(jnp.mean(x * x, axis=-1, keepdims=True) + EPS)


class _FfnWeightStager:
    def __init__(self, layer, gain_ref, win_hbm, wout_hbm, win_bf, wout_bf, stage_in, stage_out, sems):
        self.layer, self.gain_ref = layer, gain_ref
        self.win_hbm, self.wout_hbm, self.win_bf, self.wout_bf = win_hbm, wout_hbm, win_bf, wout_bf
        self.stage_in, self.stage_out, self.sems = stage_in, stage_out, sems

    def _copies(self, c):
        slot, lo = c % 2, c * FFN_FC
        return (
            pltpu.make_async_copy(self.win_hbm.at[self.layer, :, pl.ds(lo, FFN_FC)],
                                  self.stage_in.at[slot, 0], self.sems.at[slot, 0]),
            pltpu.make_async_copy(self.win_hbm.at[self.layer, :, pl.ds(D_FF + lo, FFN_FC)],
                                  self.stage_in.at[slot, 1], self.sems.at[slot, 1]),
            pltpu.make_async_copy(self.wout_hbm.at[self.layer, pl.ds(lo, FFN_FC), :],
                                  self.stage_out.at[slot], self.sems.at[slot, 2]),
        )

    def start(self, c):
        for cp in self._copies(c):
            cp.start()

    def finish(self, c):
        slot, lo = c % 2, c * FFN_FC
        for cp in self._copies(c):
            cp.wait()
        gain = jnp.concatenate([self.gain_ref[...]] * (FFN_FC // LANES), axis=1)
        self.win_bf[:, lo:lo + FFN_FC] = (self.stage_in[slot, 0] * gain).astype(BF16)
        self.win_bf[:, D_FF + lo:D_FF + lo + FFN_FC] = (self.stage_in[slot, 1] * gain).astype(BF16)
        self.wout_bf[lo:lo + FFN_FC, :] = self.stage_out[slot].astype(BF16)


def _ffn_rows(x, win_ref, wout_ref, a_scr, stager=None):
    h = x.astype(BF16)
    r = _row_rsqrt(x)
    n_chunks = D_FF // FFN_FC
    if stager is not None:
        stager.start(0)
    for c in range(n_chunks):
        lo = c * FFN_FC
        if stager is not None:
            if c + 1 < n_chunks:
                stager.start(c + 1)
            stager.finish(c)
        gate = jnp.dot(h, win_ref[:, lo:lo + FFN_FC], preferred_element_type=F32) * r
        up = jnp.dot(h, win_ref[:, D_FF + lo:D_FF + lo + FFN_FC], preferred_element_type=F32) * r
        a_scr[:, lo:lo + FFN_FC] = (gate * jax.nn.sigmoid(gate) * up).astype(BF16)
    y = jnp.dot(a_scr[...], wout_ref[...], preferred_element_type=F32)
    return x + 0.5 * y


def _ffn_tile(x_fn, o_ref, a_scr, stager):
    first = pl.program_id(0) == 0

    @pl.when(first)
    def _():
        o_ref[...] = _ffn_rows(x_fn(), stager.win_bf, stager.wout_bf, a_scr, stager)

    @pl.when(jnp.logical_not(first))
    def _():
        o_ref[...] = _ffn_rows(x_fn(), stager.win_bf, stager.wout_bf, a_scr)


def _proj_src_col(i):
    if i < 3 * N_DIL_GROUPS:
        return (i % 3) * DIL_WIDTH + (i // 3) * SEG
    return i * SEG


def _ffn_kernel(x_ref, gain_ref, win_hbm, wout_hbm, pw_ref, pgain_ref, o_ref, pw_bf_ref, a_scr, *stage_refs, layer):
    pgain = jnp.concatenate([pgain_ref[...]] * (SEG // LANES), axis=1)
    for j in range(N_SEG):
        src = _proj_src_col(j)
        pw_bf_ref[:, j * SEG:(j + 1) * SEG] = (pw_ref[:, src:src + SEG] * pgain).astype(BF16)
    _ffn_tile(lambda: x_ref[...], o_ref, a_scr, _FfnWeightStager(layer, gain_ref, win_hbm, wout_hbm, *stage_refs))


def _merge_ffn_kernel(x_ref, od_ref, of_ref, gate_ref, wa_ref, wb_ref, wo_ref, gain_ref, win_hbm, wout_hbm,
                      o_ref, a_scr, *stage_refs, layer):
    def merged():
        ya = jnp.dot(od_ref[...], wa_ref[...], preferred_element_type=F32)
        yb = jnp.dot(of_ref[...], wb_ref[...], preferred_element_type=F32)
        y = gate_ref[:, 0:D_MODEL].astype(F32) * ya + gate_ref[:, D_MODEL:2 * D_MODEL].astype(F32) * yb
        return x_ref[...] + jnp.dot(y.astype(BF16), wo_ref[...], preferred_element_type=F32)

    _ffn_tile(merged, o_ref, a_scr, _FfnWeightStager(layer, gain_ref, win_hbm, wout_hbm, *stage_refs))


def _resident(shape):
    return pl.BlockSpec(shape, lambda i: (0,) * len(shape), pipeline_mode=pl.Buffered(1))


def _resident_layer(shape, layer):
    return pl.BlockSpec((None,) + shape, lambda i: (layer,) + (0,) * len(shape), pipeline_mode=pl.Buffered(1))


def _ffn_weight_specs():
    return [_resident((D_MODEL, LANES)), pl.BlockSpec(memory_space=pl.ANY), pl.BlockSpec(memory_space=pl.ANY)]


def _ffn_scratch(tm):
    return [pltpu.VMEM((tm, D_FF), BF16),
            pltpu.VMEM((D_MODEL, 2 * D_FF), BF16), pltpu.VMEM((D_FF, D_MODEL), BF16),
            pltpu.VMEM((2, 2, D_MODEL, FFN_FC), F32), pltpu.VMEM((2, FFN_FC, D_MODEL), F32),
            pltpu.SemaphoreType.DMA((2, 3))]


def _lane_gain(norm_gain):
    return jnp.broadcast_to(norm_gain.reshape(D_MODEL, 1), (D_MODEL, LANES))


def _ffn(x, norm_gain, w_in, w_out, proj_w, proj_norm_gain, layer):
    tm = FFN_TM
    n_steps = M_ROWS // tm
    pw_rows = D_MODEL // n_steps
    row = lambda i: (i, 0)
    return pl.pallas_call(
        functools.partial(_ffn_kernel, layer=layer),
        out_shape=(jax.ShapeDtypeStruct((M_ROWS, D_MODEL), F32), jax.ShapeDtypeStruct((D_MODEL, IN_COLS), BF16)),
        grid=(n_steps,),
        in_specs=[pl.BlockSpec((tm, D_MODEL), row)] + _ffn_weight_specs() + [
            pl.BlockSpec((None, pw_rows, IN_COLS), lambda i: (layer, i, 0)),
            pl.BlockSpec((pw_rows, LANES), row)],
        out_specs=(pl.BlockSpec((tm, D_MODEL), row), pl.BlockSpec((pw_rows, IN_COLS), row)),
        scratch_shapes=_ffn_scratch(tm),
        compiler_params=pltpu.CompilerParams(
            dimension_semantics=("arbitrary",), vmem_limit_bytes=VMEM_LIMIT_BYTES),
        name="ffn",
    )(x, _lane_gain(norm_gain), w_in, w_out, proj_w, _lane_gain(proj_norm_gain))


def _merge_ffn(x, o_dil, o_diff, gates, wa, wb, wo, norm_gain, w_in, w_out, layer):
    tm = MERGE_TM
    row = lambda i: (i, 0)
    return pl.pallas_call(
        functools.partial(_merge_ffn_kernel, layer=layer),
        out_shape=jax.ShapeDtypeStruct((M_ROWS, D_MODEL), F32),
        grid=(M_ROWS // tm,),
        in_specs=[
            pl.BlockSpec((tm, D_MODEL), row),
            pl.BlockSpec((tm, DIL_OUT), row),
            pl.BlockSpec((tm, DIFF_V_WIDTH), row),
            pl.BlockSpec((tm, 2 * D_MODEL), row),
            _resident_layer((DIL_OUT, D_MODEL), layer),
            _resident_layer((DIFF_V_WIDTH, D_MODEL), layer),
            _resident_layer((D_MODEL, D_MODEL), layer),
        ] + _ffn_weight_specs(),
        out_specs=pl.BlockSpec((tm, D_MODEL), row),
        scratch_shapes=_ffn_scratch(tm),
        compiler_params=pltpu.CompilerParams(
            dimension_semantics=("arbitrary",), vmem_limit_bytes=VMEM_LIMIT_BYTES),
        name="merge_ffn",
    )(x, o_dil, o_diff, gates, wa, wb, wo, _lane_gain(norm_gain), w_in, w_out)


def _proj_kernel(x_ref, w_ref, gain_ref,
                 o_g0, o_g1, o_g2, o_qd, o_kd, o_vd, o_gate, scr1, scr2, scr3):
    tm = PROJ_TM
    x = x_ref[...]
    h = x.astype(BF16)
    r = _row_rsqrt(x)
    lane_lo = lax.broadcasted_iota(jnp.int32, (1, LANES), 1) < HEAD_DIM

    def seg(i):
        return jnp.dot(h, w_ref[:, i * SEG:(i + 1) * SEG], preferred_element_type=F32) * r

    def qk_norm(y, gi):
        tiles = []
        for u in range(SEG // LANES):
            yt = y[:, u * LANES:(u + 1) * LANES]
            sq = yt * yt
            low = jnp.sum(jnp.where(lane_lo, sq, 0.0), axis=-1, keepdims=True)
            high = jnp.sum(jnp.where(lane_lo, 0.0, sq), axis=-1, keepdims=True)
            ssq = jnp.where(lane_lo, low, high)
            tiles.append(yt * lax.rsqrt(ssq + HEAD_DIM * EPS))
        return jnp.concatenate(tiles, axis=-1) * gain_ref[gi:gi + 1, :]

    o_g0[:, 0:SEG] = qk_norm(seg(0), 0).astype(BF16)
    o_g0[:, SEG:2 * SEG] = qk_norm(seg(1), 1).astype(BF16)
    o_g0[:, 2 * SEG:3 * SEG] = seg(2).astype(BF16)
    n_t = DIL_WIDTH // LANES
    quarter = tm // 4
    for g, (scr, o_g) in ((1, (scr1, o_g1)), (2, (scr2, o_g2))):
        ys = (qk_norm(seg(3 * g), 2 * g), qk_norm(seg(3 * g + 1), 2 * g + 1), seg(3 * g + 2))
        for t, y in enumerate(ys):
            for u in range(SEG // LANES):
                scr[2 * t + u] = y[:, u * LANES:(u + 1) * LANES]
        if g == 1:
            for c in range(4):
                for t in range(n_t):
                    o_g[0, c, :, t * LANES:(t + 1) * LANES] = scr[t, pl.ds(c, quarter, stride=4), :].astype(BF16)
        else:
            for c1 in range(4):
                for t in range(n_t):
                    scr3[t, c1 * quarter:(c1 + 1) * quarter, :] = scr[t, pl.ds(c1, quarter, stride=4), :]
            for c in range(16):
                c1, c2 = c % 4, c // 4
                for t in range(n_t):
                    o_g[0, c, :, t * LANES:(t + 1) * LANES] = (
                        scr3[t, pl.ds(c1 * quarter + c2, tm // 16, stride=4), :].astype(BF16))
    for t in range(2):
        o_qd[:, t * SEG:(t + 1) * SEG] = qk_norm(seg(9 + t), 6 + t).astype(BF16)
        o_kd[:, t * SEG:(t + 1) * SEG] = qk_norm(seg(11 + t), 8 + t).astype(BF16)
        o_vd[:, t * SEG:(t + 1) * SEG] = seg(13 + t).astype(BF16)
    for t in range(8):
        o_gate[:, t * SEG:(t + 1) * SEG] = jax.nn.sigmoid(seg(15 + t)).astype(BF16)


def _proj(x, w, qk_gain):
    tm = PROJ_TM
    tiles_per_seq = SEQ // tm
    const = lambda i: (0, 0)
    row = lambda i: (i, 0)
    perm = lambda i: (i // tiles_per_seq, 0, i % tiles_per_seq, 0)
    out_shape = (
        jax.ShapeDtypeStruct((M_ROWS, DIL_WIDTH), BF16),
        jax.ShapeDtypeStruct((BATCH, 4, SEQ // 4, DIL_WIDTH), BF16),
        jax.ShapeDtypeStruct((BATCH, 16, SEQ // 16, DIL_WIDTH), BF16),
        jax.ShapeDtypeStruct((M_ROWS, DIFF_QK_WIDTH), BF16),
        jax.ShapeDtypeStruct((M_ROWS, DIFF_QK_WIDTH), BF16),
        jax.ShapeDtypeStruct((M_ROWS, DIFF_V_WIDTH), BF16),
        jax.ShapeDtypeStruct((M_ROWS, 2 * D_MODEL), BF16),
    )
    out_specs = (
        pl.BlockSpec((tm, DIL_WIDTH), row),
        pl.BlockSpec((1, 4, tm // 4, DIL_WIDTH), perm),
        pl.BlockSpec((1, 16, tm // 16, DIL_WIDTH), perm),
        pl.BlockSpec((tm, DIFF_QK_WIDTH), row),
        pl.BlockSpec((tm, DIFF_QK_WIDTH), row),
        pl.BlockSpec((tm, DIFF_V_WIDTH), row),
        pl.BlockSpec((tm, 2 * D_MODEL), row),
    )
    return pl.pallas_call(
        _proj_kernel,
        out_shape=out_shape,
        grid=(M_ROWS // tm,),
        in_specs=[
            pl.BlockSpec((tm, D_MODEL), row),
            _resident((D_MODEL, IN_COLS)),
            pl.BlockSpec((16, SEG), const),
        ],
        out_specs=out_specs,
        scratch_shapes=[pltpu.VMEM((DIL_WIDTH // LANES, tm, LANES), F32)] * 3,
        compiler_params=pltpu.CompilerParams(
            dimension_semantics=("arbitrary",), vmem_limit_bytes=VMEM_LIMIT_BYTES),
        name="proj",
    )(x, w, qk_gain)


def _dil_block(ref, row_cur, row_prev, bias, lane_lo):
    q = ref[0, pl.ds(row_cur, BLOCK), 0:SEG]
    k = ref[0, pl.ds(row_cur, BLOCK), SEG:2 * SEG]
    v = ref[0, pl.ds(row_cur, BLOCK), 2 * SEG:3 * SEG]
    if row_prev is not None:
        k = jnp.concatenate([ref[0, pl.ds(row_prev, BLOCK), SEG:2 * SEG], k], axis=0)
        v = jnp.concatenate([ref[0, pl.ds(row_prev, BLOCK), 2 * SEG:3 * SEG], v], axis=0)
    zero = jnp.zeros((BLOCK, LANES), BF16)
    copies = []
    for j in range(HEADS_PER_GROUP):
        qt = q[:, (j // 2) * LANES:(j // 2 + 1) * LANES]
        kept = jnp.where(lane_lo, qt, zero) if j % 2 == 0 else jnp.where(lane_lo, zero, qt)
        copies.append(jnp.concatenate([kept, zero] if j < 2 else [zero, kept], axis=1))
    qs = jnp.concatenate(copies, axis=0)
    s = lax.dot_general(qs, k, (((1,), (1,)), ((), ())), preferred_element_type=F32) + bias
    m = jnp.max(s, axis=-1, keepdims=True)
    p = jnp.exp2(s - m)
    l = jnp.sum(p, axis=-1, keepdims=True)
    pv = jnp.dot(p.astype(BF16), v, preferred_element_type=F32)
    stats = []
    for u in range(SEG // LANES):
        ra, rb = slice(2 * u * BLOCK, (2 * u + 1) * BLOCK), slice((2 * u + 1) * BLOCK, (2 * u + 2) * BLOCK)
        cols = slice(u * LANES, (u + 1) * LANES)
        stats.append((jnp.where(lane_lo, pv[ra, cols], pv[rb, cols]),
                      jnp.where(lane_lo, m[ra], m[rb]), jnp.where(lane_lo, l[ra], l[rb])))
    return stats


def _dil_kernel(g0_ref, g1_ref, g2_ref, bias_ref, o_ref, o_scr, m_scr, l_scr):
    step = pl.program_id(1)
    lane_lo = lax.broadcasted_iota(jnp.int32, (1, LANES), 1) < HEAD_DIM

    def put(g, rows, stats):
        for u, (o, m, l) in enumerate(stats):
            o_scr[g, u, rows, :] = o
            m_scr[g, u, rows, :] = m
            l_scr[g, u, rows, :] = l

    for jj in range(DIL_JS):
        j = step * DIL_JS + jj
        row = pl.multiple_of(j * BLOCK, BLOCK)

        prev = pl.multiple_of(jnp.maximum(j - 1, 0) * BLOCK, BLOCK)
        tab = jnp.where(j == 0, N_DIL_GROUPS, 0) if jj == 0 else 0
        put(0, pl.ds(row, BLOCK), _dil_block(g0_ref, row, prev, bias_ref[tab], lane_lo))

        c, n = step * (DIL_JS // 4) + jj // 4, jj % 4
        if n == 0:
            stats = _dil_block(g1_ref, row, None, bias_ref[1, :, BLOCK:2 * BLOCK], lane_lo)
        else:
            prev = pl.multiple_of(row - BLOCK, BLOCK)
            stats = _dil_block(g1_ref, row, prev, bias_ref[1], lane_lo)
        put(1, pl.ds(n * (4 * BLOCK) + c, BLOCK, stride=4), stats)

        put(2, pl.ds(j, BLOCK, stride=16),
            _dil_block(g2_ref, row, None, bias_ref[2, :, BLOCK:2 * BLOCK], lane_lo))

    @pl.when(step == pl.num_programs(1) - 1)
    def _():
        rows = 256
        for r in range(SEQ // rows):
            sl = pl.ds(r * rows, rows)
            for u in range(SEG // LANES):
                m0, m1, m2 = m_scr[0, u, sl, :], m_scr[1, u, sl, :], m_scr[2, u, sl, :]
                mx = jnp.maximum(jnp.maximum(m0, m1), m2)
                w0, w1, w2 = jnp.exp2(m0 - mx), jnp.exp2(m1 - mx), jnp.exp2(m2 - mx)
                num = w0 * o_scr[0, u, sl, :] + w1 * o_scr[1, u, sl, :] + w2 * o_scr[2, u, sl, :]
                den = w0 * l_scr[0, u, sl, :] + w1 * l_scr[1, u, sl, :] + w2 * l_scr[2, u, sl, :]
                o_ref[0, sl, u * LANES:(u + 1) * LANES] = (num / den).astype(BF16)


def _dil_bias_table():
    qi = np.arange(BLOCK)[:, None]
    ki = np.arange(2 * BLOCK)[None, :]
    dist = BLOCK + qi - ki
    valid = (dist >= 0) & (dist <= BLOCK)
    tab = np.full((N_DIL_GROUPS + 1, HEADS_PER_GROUP * BLOCK, 2 * BLOCK), NEG, np.float32)
    for g, (_, dil) in enumerate(DIL_CONFIGS):
        for h in range(HEADS_PER_GROUP):
            slope = np.float32(SLOPES[g * HEADS_PER_GROUP + h])
            b = np.where(valid, -slope * np.float32(LOG2E) * (dil * dist).astype(np.float32),
                         np.float32(NEG)).astype(np.float32)
            tab[g, h * BLOCK:(h + 1) * BLOCK] = b
            if g == 0:
                tab[N_DIL_GROUPS, h * BLOCK:(h + 1) * BLOCK] = np.where(ki >= BLOCK, b, np.float32(NEG))
    return tab


def _dil_attn(g0, g1, g2, bias):
    n_blk = SEQ // BLOCK
    seq_spec = pl.BlockSpec((1, SEQ, DIL_WIDTH), lambda b, j: (b, 0, 0))
    return pl.pallas_call(
        _dil_kernel,
        out_shape=jax.ShapeDtypeStruct((BATCH, SEQ, DIL_OUT), BF16),
        grid=(BATCH, n_blk // DIL_JS),
        in_specs=[seq_spec, seq_spec, seq_spec,
                  pl.BlockSpec(bias.shape, lambda b, j: (0, 0, 0))],
        out_specs=pl.BlockSpec((1, SEQ, DIL_OUT), lambda b, j: (b, 0, 0)),
        scratch_shapes=[pltpu.VMEM((N_DIL_GROUPS, DIL_OUT // LANES, SEQ, LANES), F32)] * 3,
        compiler_params=pltpu.CompilerParams(
            dimension_semantics=("arbitrary", "arbitrary"), vmem_limit_bytes=VMEM_LIMIT_BYTES),
        name="dil_attn",
    )(g0, g1, g2, bias)


def _diff_kernel(q_ref, k_ref, v_ref, lq_ref, lk_ref, sn_ref, o_ref,
                 s_scr, mp_scr, m_scr, acc_scr, *, lam_init):
    t = DIFF_T
    n_lt = t // LANES
    n_q = SEQ // t
    pair = pl.program_id(1)
    qp = pl.program_id(2)
    lane_map = lax.broadcasted_iota(jnp.int32, (1, SEG), 1) // HEAD_DIM

    def stacked_q(qi):
        q = q_ref[0, qi * t:(qi + 1) * t, :]
        zero = jnp.zeros_like(q)
        return [jnp.where(lane_map == g, q, zero) for g in range(4)]
    slopes = (jnp.where(pair == 0, SLOPES[N_DIL_HEADS] * LOG2E, SLOPES[N_DIL_HEADS + 2] * LOG2E).astype(F32),
              jnp.where(pair == 0, SLOPES[N_DIL_HEADS + 1] * LOG2E, SLOPES[N_DIL_HEADS + 3] * LOG2E).astype(F32))
    lane_k = lax.broadcasted_iota(jnp.int32, (1, t), 1)
    lane_v0 = lax.broadcasted_iota(jnp.int32, (1, SEG), 1) < DIFF_V_DIM

    def lane_tiles(x):
        return [x[:, u * LANES:(u + 1) * LANES] for u in range(x.shape[1] // LANES)]

    half = t // 2
    above = (lax.broadcasted_iota(jnp.int32, (half, half), 1)
             > lax.broadcasted_iota(jnp.int32, (half, half), 0))
    nt = (((1,), (1,)), ((), ()))

    def pass1(qs, kb, diag, init):
        k = k_ref[0, kb * t:(kb + 1) * t, :]
        bias = [sl * (kb * t + lane_k).astype(F32) for sl in slopes]
        for g in range(4):
            rows = pl.ds(g * t, t)
            if diag:
                sa = lax.dot_general(qs[g][:half], k[:half], nt, preferred_element_type=F32)
                sa = jnp.where(above, NEG, sa + bias[g // 2][:, :half])
                sb = lax.dot_general(qs[g][half:], k, nt, preferred_element_type=F32) + bias[g // 2]
                sb = jnp.concatenate([sb[:, :half], jnp.where(above, NEG, sb[:, half:])], axis=1)
                s_scr[kb, pl.ds(g * t, half), 0:half] = sa
                s_scr[kb, pl.ds(g * t + half, half), :] = sb
                ta, tb = lane_tiles(sa), lane_tiles(sb)
                tmax = jnp.concatenate([jnp.maximum(ta[0], ta[1]),
                                        jnp.maximum(jnp.maximum(tb[0], tb[1]), jnp.maximum(tb[2], tb[3]))], axis=0)
            else:
                s = lax.dot_general(qs[g], k, nt, preferred_element_type=F32) + bias[g // 2]
                s_scr[kb, rows, :] = s
                tl = lane_tiles(s)
                tmax = jnp.maximum(jnp.maximum(tl[0], tl[1]), jnp.maximum(tl[2], tl[3]))
            if not init:
                tmax = jnp.maximum(mp_scr[rows, :], tmax)
            if diag:
                m_scr[rows, :] = jnp.broadcast_to(jnp.max(tmax, axis=-1, keepdims=True), (t, LANES))
            else:
                mp_scr[rows, :] = tmax

    lq = lq_ref[...]
    lk = lk_ref[...]
    lam = (jnp.exp(jnp.sum(lq[0:1] * lk[0:1], axis=-1, keepdims=True))
           - jnp.exp(jnp.sum(lq[1:2] * lk[1:2], axis=-1, keepdims=True)) + lam_init)

    def pass2(qi, kb, init, last):
        v = v_ref[0, kb * t:(kb + 1) * t, :]
        one = jnp.ones_like(v)
        vh = (jnp.where(lane_v0, v, one), jnp.where(lane_v0, one, v))
        done = []
        for g in range(4):
            rows = pl.ds(g * t, t)
            m = m_scr[rows, :]
            if last:
                pa = jnp.exp2(s_scr[kb, pl.ds(g * t, half), 0:half]
                              - jnp.concatenate([m[:half]] * (half // LANES), axis=1))
                pb = jnp.exp2(s_scr[kb, pl.ds(g * t + half, half), :] - jnp.concatenate([m[half:]] * n_lt, axis=1))
                pv = jnp.concatenate(
                    [jnp.dot(pa.astype(BF16), vh[g // 2][:half], preferred_element_type=F32),
                     jnp.dot(pb.astype(BF16), vh[g // 2], preferred_element_type=F32)], axis=0)
            else:
                p = jnp.exp2(s_scr[kb, rows, :] - jnp.concatenate([m] * n_lt, axis=1))
                pv = jnp.dot(p.astype(BF16), vh[g // 2], preferred_element_type=F32)
            acc = pv if init else acc_scr[rows, :] + pv
            if not last:
                acc_scr[rows, :] = acc
                continue
            done.append(acc)
            if g % 2 == 1:
                hh = g // 2
                cols = slice(hh * DIFF_V_DIM, (hh + 1) * DIFF_V_DIM)
                sums = slice((1 - hh) * DIFF_V_DIM, (2 - hh) * DIFF_V_DIM)
                a1, a2 = done[g - 1], done[g]
                o = a1[:, cols] / a1[:, sums] - lam * (a2[:, cols] / a2[:, sums])
                o_ref[0, qi * t:(qi + 1) * t, cols] = (
                    _rms_rows(o, sn_ref[0, hh:hh + 1, :]) * (1.0 - lam_init)).astype(BF16)

    for c in range(n_q // 2):
        for qi in (c, n_q - 1 - c):
            @pl.when(qp == c)
            def _(qi=qi):
                qs = stacked_q(qi)
                for kb in range(qi):
                    pass1(qs, kb, False, kb == 0)
                pass1(qs, qi, True, qi == 0)

            @pl.when(qp == c)
            def _(qi=qi):
                for kb in range(qi):
                    pass2(qi, kb, kb == 0, False)
                pass2(qi, qi, qi == 0, True)


def _diff_attn(qd, kd, vd, lq, lk, subnorm, lam_init):
    t = DIFF_T
    assert SEQ // t == 4
    seq_spec = pl.BlockSpec((1, SEQ, SEG), lambda b, p, i: (b, 0, p))
    return pl.pallas_call(
        functools.partial(_diff_kernel, lam_init=lam_init),
        out_shape=jax.ShapeDtypeStruct((BATCH, SEQ, DIFF_V_WIDTH), BF16),
        grid=(BATCH, 2, SEQ // t // 2),
        in_specs=[
            seq_spec, seq_spec, seq_spec,
            pl.BlockSpec((2, HEAD_DIM), lambda b, p, i: (0, 0)),
            pl.BlockSpec((2, HEAD_DIM), lambda b, p, i: (0, 0)),
            pl.BlockSpec((1, 2, DIFF_V_DIM), lambda b, p, i: (p, 0, 0)),
        ],
        out_specs=seq_spec,
        scratch_shapes=[pltpu.VMEM((SEQ // t, 4 * t, t), F32),
                        pltpu.VMEM((4 * t, LANES), F32),
                        pltpu.VMEM((4 * t, LANES), F32),
                        pltpu.VMEM((4 * t, SEG), F32)],
        compiler_params=pltpu.CompilerParams(
            dimension_semantics=("arbitrary", "arbitrary", "arbitrary"), vmem_limit_bytes=VMEM_LIMIT_BYTES),
        name="diff_attn",
    )(qd, kd, vd, lq, lk, subnorm)


def _qk_gain_rows(qk_gain_dil, qk_gain_diff):
    k_scale = HEAD_DIM ** 0.5
    q_scale = LOG2E
    rows = []
    for g in range(N_DIL_GROUPS):
        rows.append(qk_gain_dil[0, g].reshape(SEG) * q_scale)
        rows.append(qk_gain_dil[1, g].reshape(SEG) * k_scale)
    qd = qk_gain_diff[0].reshape(2, SEG) * q_scale
    kd = qk_gain_diff[1].reshape(2, SEG) * k_scale
    rows += [qd[0], qd[1], kd[0], kd[1]]
    rows += [jnp.zeros((SEG,), F32)] * (16 - len(rows))
    return jnp.stack(rows).astype(F32)


def kernel(x, ffn1_norm, ffn1_w_in, ffn1_w_out, mix_norm, w_in, qk_gain_dil, qk_gain_diff, lambda_q, lambda_k,
           diff_subnorm, w_branch_dil, w_branch_diff, w_out, ffn2_norm, ffn2_w_in, ffn2_w_out):
    b, s, d = x.shape
    assert (b, s, d) == (BATCH, SEQ, D_MODEL)
    xr = x.reshape(M_ROWS, D_MODEL)
    dil_bias = jnp.asarray(_dil_bias_table())
    wa, wb, wo = w_branch_dil.astype(BF16), w_branch_diff.astype(BF16), w_out.astype(BF16)
    for l in range(DEPTH):
        lam_init = 0.8 - 0.6 * math.exp(-0.3 * l)
        xr, proj_w = _ffn(xr, ffn1_norm[l], ffn1_w_in, ffn1_w_out, w_in, mix_norm[l], l)
        g0, g1, g2, qd, kd, vd, gates = _proj(xr, proj_w, _qk_gain_rows(qk_gain_dil[l], qk_gain_diff[l]))
        o_dil = _dil_attn(g0.reshape(BATCH, SEQ, DIL_WIDTH), g1.reshape(BATCH, SEQ, DIL_WIDTH),
                          g2.reshape(BATCH, SEQ, DIL_WIDTH), dil_bias)
        o_diff = _diff_attn(qd.reshape(BATCH, SEQ, DIFF_QK_WIDTH), kd.reshape(BATCH, SEQ, DIFF_QK_WIDTH),
                            vd.reshape(BATCH, SEQ, DIFF_V_WIDTH), lambda_q[l], lambda_k[l],
                            diff_subnorm[l].reshape(2, 2, DIFF_V_DIM), lam_init)
        xr = _merge_ffn(xr, o_dil.reshape(M_ROWS, DIL_OUT), o_diff.reshape(M_ROWS, DIFF_V_WIDTH), gates,
                        wa, wb, wo, ffn2_norm[l], ffn2_w_in, ffn2_w_out, l)
    return xr.reshape(BATCH, SEQ, D_MODEL)
```

```python
import functools
import math

import jax
import jax.numpy as jnp
import numpy as np
from jax import lax
from jax.experimental import pallas as pl
from jax.experimental.pallas import tpu as pltpu

F32 = jnp.float32
BF16 = jnp.bfloat16

D_MODEL = 1024
BATCH = 8
SEQ = 2048
DEPTH = 2
HEAD_DIM = 64
DIL_CONFIGS = ((128, 1), (512, 4), (2048, 16))
N_DIL_GROUPS = 3
HEADS_PER_GROUP = 4
N_DIL_HEADS = 12
N_DIFF_HEADS = 4
DIFF_V_DIM = 128
N_ALIBI_HEADS = 16
D_FF = 2816
BLOCK = 128
EPS = 1e-6
DIL_WIDTH = 768
DIL_OUT = 256
DIFF_QK_WIDTH = 512
DIFF_V_WIDTH = 512
IN_COLS = 5888
NEG = -1e30

LANES = 128
VMEM_LIMIT_BYTES = 56 * 1024 * 1024

M_ROWS = BATCH * SEQ
SEG = 256
N_SEG = IN_COLS // SEG
FFN_TM = 512
FFN_FC = 256
PROJ_TM = 512
DIL_JS = 8
DIFF_T = 512
DIFF_Q_GROUPS = ((3, 0), (2, 1))
LOG2E = math.log2(math.e)

SLOPES = tuple(float(np.float32(2.0) ** np.float32(-8.0 * i / N_ALIBI_HEADS)) for i in range(1, N_ALIBI_HEADS + 1))


def _rms_rows(x, gain):
    ms = jnp.mean(x * x, axis=-1, keepdims=True)
    return x * lax.rsqrt(ms + EPS) * gain


def _row_rsqrt(x):
    return lax.rsqrt(jnp.mean(x * x, axis=-1, keepdims=True) + EPS)


class _FfnWeightStager:
    def __init__(self, layer, gain_ref, win_hbm, wout_hbm, win_bf, wout_bf, stage_in, stage_out, sems):
        self.layer, self.gain_ref = layer, gain_ref
        self.win_hbm, self.wout_hbm, self.win_bf, self.wout_bf = win_hbm, wout_hbm, win_bf, wout_bf
        self.stage_in, self.stage_out, self.sems = stage_in, stage_out, sems

    def _copies(self, c):
        slot, lo = c % 2, c * FFN_FC
        return (
            pltpu.make_async_copy(self.win_hbm.at[self.layer, :, pl.ds(lo, FFN_FC)],
                                  self.stage_in.at[slot, 0], self.sems.at[slot, 0]),
            pltpu.make_async_copy(self.win_hbm.at[self.layer, :, pl.ds(D_FF + lo, FFN_FC)],
                                  self.stage_in.at[slot, 1], self.sems.at[slot, 1]),
            pltpu.make_async_copy(self.wout_hbm.at[self.layer, pl.ds(lo, FFN_FC), :],
                                  self.stage_out.at[slot], self.sems.at[slot, 2]),
        )

    def start(self, c):
        for cp in self._copies(c):
            cp.start()

    def finish(self, c):
        slot, lo = c % 2, c * FFN_FC
        for cp in self._copies(c):
            cp.wait()
        gain = jnp.concatenate([self.gain_ref[...]] * (FFN_FC // LANES), axis=1)
        self.win_bf[:, lo:lo + FFN_FC] = (self.stage_in[slot, 0] * gain).astype(BF16)
        self.win_bf[:, D_FF + lo:D_FF + lo + FFN_FC] = (self.stage_in[slot, 1] * gain).astype(BF16)
        self.wout_bf[lo:lo + FFN_FC, :] = self.stage_out[slot].astype(BF16)


def _ffn_rows(x, win_ref, wout_ref, a_scr, stager=None):
    h = x.astype(BF16)
    r = _row_rsqrt(x)
    n_chunks = D_FF // FFN_FC
    if stager is not None:
        stager.start(0)
    for c in range(n_chunks):
        lo = c * FFN_FC
        if stager is not None:
            if c + 1 < n_chunks:
                stager.start(c + 1)
            stager.finish(c)
        gate = jnp.dot(h, win_ref[:, lo:lo + FFN_FC], preferred_element_type=F32) * r
        up = jnp.dot(h, win_ref[:, D_FF + lo:D_FF + lo + FFN_FC], preferred_element_type=F32) * r
        a_scr[:, lo:lo + FFN_FC] = (gate * jax.nn.sigmoid(gate) * up).astype(BF16)
    y = jnp.dot(a_scr[...], wout_ref[...], preferred_element_type=F32)
    return x + 0.5 * y


def _ffn_tile(x_fn, o_ref, a_scr, stager):
    first = pl.program_id(0) == 0

    @pl.when(first)
    def _():
        o_ref[...] = _ffn_rows(x_fn(), stager.win_bf, stager.wout_bf, a_scr, stager)

    @pl.when(jnp.logical_not(first))
    def _():
        o_ref[...] = _ffn_rows(x_fn(), stager.win_bf, stager.wout_bf, a_scr)


def _proj_src_col(i):
    if i < 3 * N_DIL_GROUPS:
        return (i % 3) * DIL_WIDTH + (i // 3) * SEG
    return i * SEG


def _ffn_kernel(x_ref, gain_ref, win_hbm, wout_hbm, pw_ref, pgain_ref, o_ref, pw_bf_ref, a_scr, *stage_refs, layer):
    pgain = jnp.concatenate([pgain_ref[...]] * (SEG // LANES), axis=1)
    for j in range(N_SEG):
        src = _proj_src_col(j)
        pw_bf_ref[:, j * SEG:(j + 1) * SEG] = (pw_ref[:, src:src + SEG] * pgain).astype(BF16)
    _ffn_tile(lambda: x_ref[...], o_ref, a_scr, _FfnWeightStager(layer, gain_ref, win_hbm, wout_hbm, *stage_refs))


def _merge_ffn_kernel(x_ref, od_ref, of_ref, gate_ref, wa_ref, wb_ref, wo_ref, gain_ref, win_hbm, wout_hbm,
                      o_ref, a_scr, *stage_refs, layer):
    def merged():
        ya = jnp.dot(od_ref[...], wa_ref[...], preferred_element_type=F32)
        yb = jnp.dot(of_ref[...], wb_ref[...], preferred_element_type=F32)
        y = gate_ref[:, 0:D_MODEL].astype(F32) * ya + gate_ref[:, D_MODEL:2 * D_MODEL].astype(F32) * yb
        return x_ref[...] + jnp.dot(y.astype(BF16), wo_ref[...], preferred_element_type=F32)

    _ffn_tile(merged, o_ref, a_scr, _FfnWeightStager(layer, gain_ref, win_hbm, wout_hbm, *stage_refs))


def _resident(shape):
    return pl.BlockSpec(shape, lambda i: (0,) * len(shape), pipeline_mode=pl.Buffered(1))


def _resident_layer(shape, layer):
    return pl.BlockSpec((None,) + shape, lambda i: (layer,) + (0,) * len(shape), pipeline_mode=pl.Buffered(1))


def _ffn_weight_specs():
    return [_resident((D_MODEL, LANES)), pl.BlockSpec(memory_space=pl.ANY), pl.BlockSpec(memory_space=pl.ANY)]


def _ffn_scratch(tm):
    return [pltpu.VMEM((tm, D_FF), BF16),
            pltpu.VMEM((D_MODEL, 2 * D_FF), BF16), pltpu.VMEM((D_FF, D_MODEL), BF16),
            pltpu.VMEM((2, 2, D_MODEL, FFN_FC), F32), pltpu.VMEM((2, FFN_FC, D_MODEL), F32),
            pltpu.SemaphoreType.DMA((2, 3))]


def _lane_gain(norm_gain):
    return jnp.broadcast_to(norm_gain.reshape(D_MODEL, 1), (D_MODEL, LANES))


def _ffn(x, norm_gain, w_in, w_out, proj_w, proj_norm_gain, layer):
    tm = FFN_TM
    n_steps = M_ROWS // tm
    pw_rows = D_MODEL // n_steps
    row = lambda i: (i, 0)
    return pl.pallas_call(
        functools.partial(_ffn_kernel, layer=layer),
        out_shape=(jax.ShapeDtypeStruct((M_ROWS, D_MODEL), F32), jax.ShapeDtypeStruct((D_MODEL, IN_COLS), BF16)),
        grid=(n_steps,),
        in_specs=[pl.BlockSpec((tm, D_MODEL), row)] + _ffn_weight_specs() + [
            pl.BlockSpec((None, pw_rows, IN_COLS), lambda i: (layer, i, 0)),
            pl.BlockSpec((pw_rows, LANES), row)],
        out_specs=(pl.BlockSpec((tm, D_MODEL), row), pl.BlockSpec((pw_rows, IN_COLS), row)),
        scratch_shapes=_ffn_scratch(tm),
        compiler_params=pltpu.CompilerParams(
            dimension_semantics=("arbitrary",), vmem_limit_bytes=VMEM_LIMIT_BYTES),
        name="ffn",
    )(x, _lane_gain(norm_gain), w_in, w_out, proj_w, _lane_gain(proj_norm_gain))


def _merge_ffn(x, o_dil, o_diff, gates, wa, wb, wo, norm_gain, w_in, w_out, layer):
    tm = FFN_TM
    row = lambda i: (i, 0)
    return pl.pallas_call(
        functools.partial(_merge_ffn_kernel, layer=layer),
        out_shape=jax.ShapeDtypeStruct((M_ROWS, D_MODEL), F32),
        grid=(M_ROWS // tm,),
        in_specs=[
            pl.BlockSpec((tm, D_MODEL), row),
            pl.BlockSpec((tm, DIL_OUT), row),
            pl.BlockSpec((tm, DIFF_V_WIDTH), row),
            pl.BlockSpec((tm, 2 * D_MODEL), row),
            _resident_layer((DIL_OUT, D_MODEL), layer),
            _resident_layer((DIFF_V_WIDTH, D_MODEL), layer),
            _resident_layer((D_MODEL, D_MODEL), layer),
        ] + _ffn_weight_specs(),
        out_specs=pl.BlockSpec((tm, D_MODEL), row),
        scratch_shapes=_ffn_scratch(tm),
        compiler_params=pltpu.CompilerParams(
            dimension_semantics=("arbitrary",), vmem_limit_bytes=VMEM_LIMIT_BYTES),
        name="merge_ffn",
    )(x, o_dil, o_diff, gates, wa, wb, wo, _lane_gain(norm_gain), w_in, w_out)


def _proj_kernel(x_ref, w_ref, gain_ref,
                 o_g0, o_g1, o_g2, o_qd, o_kd, o_vd, o_gate, scr1, scr2, scr3):
    tm = PROJ_TM
    x = x_ref[...]
    h = x.astype(BF16)
    r = _row_rsqrt(x)
    lane_lo = lax.broadcasted_iota(jnp.int32, (1, LANES), 1) < HEAD_DIM

    def seg(i):
        return jnp.dot(h, w_ref[:, i * SEG:(i + 1) * SEG], preferred_element_type=F32) * r

    def qk_norm(y, gi):
        tiles = []
        for u in range(SEG // LANES):
            yt = y[:, u * LANES:(u + 1) * LANES]
            sq = yt * yt
            low = jnp.sum(jnp.where(lane_lo, sq, 0.0), axis=-1, keepdims=True)
            high = jnp.sum(jnp.where(lane_lo, 0.0, sq), axis=-1, keepdims=True)
            ssq = jnp.where(lane_lo, low, high)
            tiles.append(yt * lax.rsqrt(ssq + HEAD_DIM * EPS))
        return jnp.concatenate(tiles, axis=-1) * gain_ref[gi:gi + 1, :]

    gate_segments = iter(range(8))

    def gate():
        t = next(gate_segments)
        o_gate[:, t * SEG:(t + 1) * SEG] = jax.nn.sigmoid(seg(15 + t)).astype(BF16)

    n_t = DIL_WIDTH // LANES
    quarter = tm // 4
    for g, (scr, o_g) in ((1, (scr1, o_g1)), (2, (scr2, o_g2))):
        ys = []
        for kind in range(2):
            ys.append(qk_norm(seg(3 * g + kind), 2 * g + kind))
            gate()
        ys.append(seg(3 * g + 2))
        for t, y in enumerate(ys):
            for u in range(SEG // LANES):
                scr[2 * t + u] = y[:, u * LANES:(u + 1) * LANES]
        if g == 1:
            for c in range(4):
                for t in range(n_t):
                    o_g[0, c, :, t * LANES:(t + 1) * LANES] = scr[t, pl.ds(c, quarter, stride=4), :].astype(BF16)
        else:
            for c1 in range(4):
                for t in range(n_t):
                    scr3[t, c1 * quarter:(c1 + 1) * quarter, :] = scr[t, pl.ds(c1, quarter, stride=4), :]
            for c in range(16):
                c1, c2 = c % 4, c // 4
                for t in range(n_t):
                    o_g[0, c, :, t * LANES:(t + 1) * LANES] = (
                        scr3[t, pl.ds(c1 * quarter + c2, tm // 16, stride=4), :].astype(BF16))
    for t in range(2):
        o_qd[:, t * SEG:(t + 1) * SEG] = qk_norm(seg(9 + t), 6 + t).astype(BF16)
        gate()
        o_kd[:, t * SEG:(t + 1) * SEG] = qk_norm(seg(11 + t), 8 + t).astype(BF16)
        gate()
    o_g0[:, 0:SEG] = qk_norm(seg(0), 0).astype(BF16)
    o_g0[:, SEG:2 * SEG] = qk_norm(seg(1), 1).astype(BF16)
    o_g0[:, 2 * SEG:3 * SEG] = seg(2).astype(BF16)
    for t in range(2):
        o_vd[:, t * SEG:(t + 1) * SEG] = seg(13 + t).astype(BF16)


def _proj(x, w, qk_gain):
    tm = PROJ_TM
    tiles_per_seq = SEQ // tm
    const = lambda i: (0, 0)
    row = lambda i: (i, 0)
    perm = lambda i: (i // tiles_per_seq, 0, i % tiles_per_seq, 0)
    out_shape = (
        jax.ShapeDtypeStruct((M_ROWS, DIL_WIDTH), BF16),
        jax.ShapeDtypeStruct((BATCH, 4, SEQ // 4, DIL_WIDTH), BF16),
        jax.ShapeDtypeStruct((BATCH, 16, SEQ // 16, DIL_WIDTH), BF16),
        jax.ShapeDtypeStruct((M_ROWS, DIFF_QK_WIDTH), BF16),
        jax.ShapeDtypeStruct((M_ROWS, DIFF_QK_WIDTH), BF16),
        jax.ShapeDtypeStruct((M_ROWS, DIFF_V_WIDTH), BF16),
        jax.ShapeDtypeStruct((M_ROWS, 2 * D_MODEL), BF16),
    )
    out_specs = (
        pl.BlockSpec((tm, DIL_WIDTH), row),
        pl.BlockSpec((1, 4, tm // 4, DIL_WIDTH), perm),
        pl.BlockSpec((1, 16, tm // 16, DIL_WIDTH), perm),
        pl.BlockSpec((tm, DIFF_QK_WIDTH), row),
        pl.BlockSpec((tm, DIFF_QK_WIDTH), row),
        pl.BlockSpec((tm, DIFF_V_WIDTH), row),
        pl.BlockSpec((tm, 2 * D_MODEL), row),
    )
    return pl.pallas_call(
        _proj_kernel,
        out_shape=out_shape,
        grid=(M_ROWS // tm,),
        in_specs=[
            pl.BlockSpec((tm, D_MODEL), row),
            _resident((D_MODEL, IN_COLS)),
            pl.BlockSpec((16, SEG), const),
        ],
        out_specs=out_specs,
        scratch_shapes=[pltpu.VMEM((DIL_WIDTH // LANES, tm, LANES), F32)] * 3,
        compiler_params=pltpu.CompilerParams(
            dimension_semantics=("arbitrary",), vmem_limit_bytes=VMEM_LIMIT_BYTES),
        name="proj",
    )(x, w, qk_gain)


def _dil_block(ref, row_cur, row_prev, bias, lane_lo):
    q = ref[0, pl.ds(row_cur, BLOCK), 0:SEG]
    k = ref[0, pl.ds(row_cur, BLOCK), SEG:2 * SEG]
    v = ref[0, pl.ds(row_cur, BLOCK), 2 * SEG:3 * SEG]
    if row_prev is not None:
        k = jnp.concatenate([ref[0, pl.ds(row_prev, BLOCK), SEG:2 * SEG], k], axis=0)
        v = jnp.concatenate([ref[0, pl.ds(row_prev, BLOCK), 2 * SEG:3 * SEG], v], axis=0)
    zero = jnp.zeros((BLOCK, LANES), BF16)
    copies = []
    for j in range(HEADS_PER_GROUP):
        qt = q[:, (j // 2) * LANES:(j // 2 + 1) * LANES]
        kept = jnp.where(lane_lo, qt, zero) if j % 2 == 0 else jnp.where(lane_lo, zero, qt)
        copies.append(jnp.concatenate([kept, zero] if j < 2 else [zero, kept], axis=1))
    qs = jnp.concatenate(copies, axis=0)
    s = lax.dot_general(qs, k, (((1,), (1,)), ((), ())), preferred_element_type=F32) + bias
    m = jnp.max(s, axis=-1, keepdims=True)
    p = jnp.exp2(s - m)
    l = jnp.sum(p, axis=-1, keepdims=True)
    pv = jnp.dot(p.astype(BF16), v, preferred_element_type=F32)
    stats = []
    for u in range(SEG // LANES):
        ra, rb = slice(2 * u * BLOCK, (2 * u + 1) * BLOCK), slice((2 * u + 1) * BLOCK, (2 * u + 2) * BLOCK)
        cols = slice(u * LANES, (u + 1) * LANES)
        stats.append((jnp.where(lane_lo, pv[ra, cols], pv[rb, cols]),
                      jnp.where(lane_lo, m[ra], m[rb]), jnp.where(lane_lo, l[ra], l[rb])))
    return stats


def _dil_kernel(g0_ref, g1_ref, g2_ref, bias_ref, o_ref, o_scr, m_scr, l_scr):
    step = pl.program_id(1)
    lane_lo = lax.broadcasted_iota(jnp.int32, (1, LANES), 1) < HEAD_DIM

    def put(g, rows, stats):
        for u, (o, m, l) in enumerate(stats):
            o_scr[g, u, rows, :] = o
            m_scr[g, u, rows, :] = m
            l_scr[g, u, rows, :] = l

    for jj in range(DIL_JS):
        j = step * DIL_JS + jj
        row = pl.multiple_of(j * BLOCK, BLOCK)

        prev = pl.multiple_of(jnp.maximum(j - 1, 0) * BLOCK, BLOCK)
        tab = jnp.where(j == 0, N_DIL_GROUPS, 0) if jj == 0 else 0
        put(0, pl.ds(row, BLOCK), _dil_block(g0_ref, row, prev, bias_ref[tab], lane_lo))

        c, n = step * (DIL_JS // 4) + jj // 4, jj % 4
        if n == 0:
            stats = _dil_block(g1_ref, row, None, bias_ref[1, :, BLOCK:2 * BLOCK], lane_lo)
        else:
            prev = pl.multiple_of(row - BLOCK, BLOCK)
            stats = _dil_block(g1_ref, row, prev, bias_ref[1], lane_lo)
        put(1, pl.ds(n * (4 * BLOCK) + c, BLOCK, stride=4), stats)

        put(2, pl.ds(j, BLOCK, stride=16),
            _dil_block(g2_ref, row, None, bias_ref[2, :, BLOCK:2 * BLOCK], lane_lo))

    @pl.when(step == pl.num_programs(1) - 1)
    def _():
        rows = 256
        for r in range(SEQ // rows):
            sl = pl.ds(r * rows, rows)
            for u in range(SEG // LANES):
                m0, m1, m2 = m_scr[0, u, sl, :], m_scr[1, u, sl, :], m_scr[2, u, sl, :]
                mx = jnp.maximum(jnp.maximum(m0, m1), m2)
                w0, w1, w2 = jnp.exp2(m0 - mx), jnp.exp2(m1 - mx), jnp.exp2(m2 - mx)
                num = w0 * o_scr[0, u, sl, :] + w1 * o_scr[1, u, sl, :] + w2 * o_scr[2, u, sl, :]
                den = w0 * l_scr[0, u, sl, :] + w1 * l_scr[1, u, sl, :] + w2 * l_scr[2, u, sl, :]
                o_ref[0, sl, u * LANES:(u + 1) * LANES] = (num / den).astype(BF16)


def _dil_bias_table():
    qi = np.arange(BLOCK)[:, None]
    ki = np.arange(2 * BLOCK)[None, :]
    dist = BLOCK + qi - ki
    valid = (dist >= 0) & (dist <= BLOCK)
    tab = np.full((N_DIL_GROUPS + 1, HEADS_PER_GROUP * BLOCK, 2 * BLOCK), NEG, np.float32)
    for g, (_, dil) in enumerate(DIL_CONFIGS):
        for h in range(HEADS_PER_GROUP):
            slope = np.float32(SLOPES[g * HEADS_PER_GROUP + h])
            b = np.where(valid, -slope * np.float32(LOG2E) * (dil * dist).astype(np.float32),
                         np.float32(NEG)).astype(np.float32)
            tab[g, h * BLOCK:(h + 1) * BLOCK] = b
            if g == 0:
                tab[N_DIL_GROUPS, h * BLOCK:(h + 1) * BLOCK] = np.where(ki >= BLOCK, b, np.float32(NEG))
    return tab


def _dil_attn(g0, g1, g2, bias):
    n_blk = SEQ // BLOCK
    seq_spec = pl.BlockSpec((1, SEQ, DIL_WIDTH), lambda b, j: (b, 0, 0))
    return pl.pallas_call(
        _dil_kernel,
        out_shape=jax.ShapeDtypeStruct((BATCH, SEQ, DIL_OUT), BF16),
        grid=(BATCH, n_blk // DIL_JS),
        in_specs=[seq_spec, seq_spec, seq_spec,
                  pl.BlockSpec(bias.shape, lambda b, j: (0, 0, 0))],
        out_specs=pl.BlockSpec((1, SEQ, DIL_OUT), lambda b, j: (b, 0, 0)),
        scratch_shapes=[pltpu.VMEM((N_DIL_GROUPS, DIL_OUT // LANES, SEQ, LANES), F32)] * 3,
        compiler_params=pltpu.CompilerParams(
            dimension_semantics=("arbitrary", "arbitrary"), vmem_limit_bytes=VMEM_LIMIT_BYTES),
        name="dil_attn",
    )(g0, g1, g2, bias)


def _diff_kernel(q_ref, k_ref, v_ref, lq_ref, lk_ref, sn_ref, o_ref,
                 s_scr, mp_scr, m_scr, acc_scr, *, lam_init):
    t = DIFF_T
    n_lt = t // LANES
    n_q = SEQ // t
    pair = pl.program_id(1)
    qp = pl.program_id(2)
    lane_map = lax.broadcasted_iota(jnp.int32, (1, SEG), 1) // HEAD_DIM

    def stacked_q(qi):
        q = q_ref[0, qi * t:(qi + 1) * t, :]
        zero = jnp.zeros_like(q)
        return [jnp.where(lane_map == g, q, zero) for g in range(4)]
    slopes = (jnp.where(pair == 0, SLOPES[N_DIL_HEADS] * LOG2E, SLOPES[N_DIL_HEADS + 2] * LOG2E).astype(F32),
              jnp.where(pair == 0, SLOPES[N_DIL_HEADS + 1] * LOG2E, SLOPES[N_DIL_HEADS + 3] * LOG2E).astype(F32))
    lane_k = lax.broadcasted_iota(jnp.int32, (1, t), 1)
    lane_v0 = lax.broadcasted_iota(jnp.int32, (1, SEG), 1) < DIFF_V_DIM

    def lane_tiles(x):
        return [x[:, u * LANES:(u + 1) * LANES] for u in range(x.shape[1] // LANES)]

    half = t // 2
    above = (lax.broadcasted_iota(jnp.int32, (half, half), 1)
             > lax.broadcasted_iota(jnp.int32, (half, half), 0))
    nt = (((1,), (1,)), ((), ()))

    def pass1(qs, kb, diag, init):
        k = k_ref[0, kb * t:(kb + 1) * t, :]
        bias = [sl * (kb * t + lane_k).astype(F32) for sl in slopes]
        for g in range(4):
            rows = pl.ds(g * t, t)
            if diag:
                sa = lax.dot_general(qs[g][:half], k[:half], nt, preferred_element_type=F32)
                sa = jnp.where(above, NEG, sa + bias[g // 2][:, :half])
                sb = lax.dot_general(qs[g][half:], k, nt, preferred_element_type=F32) + bias[g // 2]
                sb = jnp.concatenate([sb[:, :half], jnp.where(above, NEG, sb[:, half:])], axis=1)
                s_scr[kb, pl.ds(g * t, half), 0:half] = sa
                s_scr[kb, pl.ds(g * t + half, half), :] = sb
                ta, tb = lane_tiles(sa), lane_tiles(sb)
                tmax = jnp.concatenate([jnp.maximum(ta[0], ta[1]),
                                        jnp.maximum(jnp.maximum(tb[0], tb[1]), jnp.maximum(tb[2], tb[3]))], axis=0)
            else:
                s = lax.dot_general(qs[g], k, nt, preferred_element_type=F32) + bias[g // 2]
                s_scr[kb, rows, :] = s
                tl = lane_tiles(s)
                tmax = jnp.maximum(jnp.maximum(tl[0], tl[1]), jnp.maximum(tl[2], tl[3]))
            if not init:
                tmax = jnp.maximum(mp_scr[rows, :], tmax)
            if diag:
                m_scr[rows, :] = jnp.broadcast_to(jnp.max(tmax, axis=-1, keepdims=True), (t, LANES))
            else:
                mp_scr[rows, :] = tmax

    lq = lq_ref[...]
    lk = lk_ref[...]
    lam = (jnp.exp(jnp.sum(lq[0:1] * lk[0:1], axis=-1, keepdims=True))
           - jnp.exp(jnp.sum(lq[1:2] * lk[1:2], axis=-1, keepdims=True)) + lam_init)

    def pass2(qi, kb, init, last):
        v = v_ref[0, kb * t:(kb + 1) * t, :]
        one = jnp.ones_like(v)
        vh = (jnp.where(lane_v0, v, one), jnp.where(lane_v0, one, v))
        done = []
        for g in range(4):
            rows = pl.ds(g * t, t)
            m = m_scr[rows, :]
            if last:
                pa = jnp.exp2(s_scr[kb, pl.ds(g * t, half), 0:half]
                              - jnp.concatenate([m[:half]] * (half // LANES), axis=1))
                pb = jnp.exp2(s_scr[kb, pl.ds(g * t + half, half), :] - jnp.concatenate([m[half:]] * n_lt, axis=1))
                pv = jnp.concatenate(
                    [jnp.dot(pa.astype(BF16), vh[g // 2][:half], preferred_element_type=F32),
                     jnp.dot(pb.astype(BF16), vh[g // 2], preferred_element_type=F32)], axis=0)
            else:
                p = jnp.exp2(s_scr[kb, rows, :] - jnp.concatenate([m] * n_lt, axis=1))
                pv = jnp.dot(p.astype(BF16), vh[g // 2], preferred_element_type=F32)
            acc = pv if init else acc_scr[rows, :] + pv
            if not last:
                acc_scr[rows, :] = acc
                continue
            done.append(acc)
            if g % 2 == 1:
                hh = g // 2
                cols = slice(hh * DIFF_V_DIM, (hh + 1) * DIFF_V_DIM)
                sums = slice((1 - hh) * DIFF_V_DIM, (2 - hh) * DIFF_V_DIM)
                a1, a2 = done[g - 1], done[g]
                o = a1[:, cols] / a1[:, sums] - lam * (a2[:, cols] / a2[:, sums])
                o_ref[0, qi * t:(qi + 1) * t, cols] = (
                    _rms_rows(o, sn_ref[0, hh:hh + 1, :]) * (1.0 - lam_init)).astype(BF16)

    for c, group in enumerate(DIFF_Q_GROUPS):
        @pl.when(qp == c)
        def _(group=group):
            for qi in group:
                qs = stacked_q(qi)
                for kb in range(qi):
                    pass1(qs, kb, False, kb == 0)
                pass1(qs, qi, True, qi == 0)
                for kb in range(qi):
                    pass2(qi, kb, kb == 0, False)
                pass2(qi, qi, qi == 0, True)


def _diff_attn(qd, kd, vd, lq, lk, subnorm, lam_init):
    t = DIFF_T
    assert SEQ // t == 4
    seq_spec = pl.BlockSpec((1, SEQ, SEG), lambda b, p, i: (b, 0, p))
    return pl.pallas_call(
        functools.partial(_diff_kernel, lam_init=lam_init),
        out_shape=jax.ShapeDtypeStruct((BATCH, SEQ, DIFF_V_WIDTH), BF16),
        grid=(BATCH, 2, len(DIFF_Q_GROUPS)),
        in_specs=[
            seq_spec, seq_spec, seq_spec,
            pl.BlockSpec((2, HEAD_DIM), lambda b, p, i: (0, 0)),
            pl.BlockSpec((2, HEAD_DIM), lambda b, p, i: (0, 0)),
            pl.BlockSpec((1, 2, DIFF_V_DIM), lambda b, p, i: (p, 0, 0)),
        ],
        out_specs=seq_spec,
        scratch_shapes=[pltpu.VMEM((SEQ // t, 4 * t, t), F32),
                        pltpu.VMEM((4 * t, LANES), F32),
                        pltpu.VMEM((4 * t, LANES), F32),
                        pltpu.VMEM((4 * t, SEG), F32)],
        compiler_params=pltpu.CompilerParams(
            dimension_semantics=("arbitrary", "arbitrary", "arbitrary"), vmem_limit_bytes=VMEM_LIMIT_BYTES),
        name="diff_attn",
    )(qd, kd, vd, lq, lk, subnorm)


def _qk_gain_rows(qk_gain_dil, qk_gain_diff):
    k_scale = HEAD_DIM ** 0.5
    q_scale = LOG2E
    rows = []
    for g in range(N_DIL_GROUPS):
        rows.append(qk_gain_dil[0, g].reshape(SEG) * q_scale)
        rows.append(qk_gain_dil[1, g].reshape(SEG) * k_scale)
    qd = qk_gain_diff[0].reshape(2, SEG) * q_scale
    kd = qk_gain_diff[1].reshape(2, SEG) * k_scale
    rows += [qd[0], qd[1], kd[0], kd[1]]
    rows += [jnp.zeros((SEG,), F32)] * (16 - len(rows))
    return jnp.stack(rows).astype(F32)


def kernel(x, ffn1_norm, ffn1_w_in, ffn1_w_out, mix_norm, w_in, qk_gain_dil, qk_gain_diff, lambda_q, lambda_k,
           diff_subnorm, w_branch_dil, w_branch_diff, w_out, ffn2_norm, ffn2_w_in, ffn2_w_out):
    b, s, d = x.shape
    assert (b, s, d) == (BATCH, SEQ, D_MODEL)
    xr = x.reshape(M_ROWS, D_MODEL)
    dil_bias = jnp.asarray(_dil_bias_table())
    wa, wb, wo = w_branch_dil.astype(BF16), w_branch_diff.astype(BF16), w_out.astype(BF16)
    for l in range(DEPTH):
        lam_init = 0.8 - 0.6 * math.exp(-0.3 * l)
        xr, proj_w = _ffn(xr, ffn1_norm[l], ffn1_w_in, ffn1_w_out, w_in, mix_norm[l], l)
        g0, g1, g2, qd, kd, vd, gates = _proj(xr, proj_w, _qk_gain_rows(qk_gain_dil[l], qk_gain_diff[l]))
        o_dil = _dil_attn(g0.reshape(BATCH, SEQ, DIL_WIDTH), g1.reshape(BATCH, SEQ, DIL_WIDTH),
                          g2.reshape(BATCH, SEQ, DIL_WIDTH), dil_bias)
        o_diff = _diff_attn(qd.reshape(BATCH, SEQ, DIFF_QK_WIDTH), kd.reshape(BATCH, SEQ, DIFF_QK_WIDTH),
                            vd.reshape(BATCH, SEQ, DIFF_V_WIDTH), lambda_q[l], lambda_k[l],
                            diff_subnorm[l].reshape(2, 2, DIFF_V_DIM), lam_init)
        xr = _merge_ffn(xr, o_dil.reshape(M_ROWS, DIL_OUT), o_diff.reshape(M_ROWS, DIFF_V_WIDTH), gates,
                        wa, wb, wo, ffn2_norm[l], ffn2_w_in, ffn2_w_out, l)
    return xr.reshape(BATCH, SEQ, D_MODEL)
```

```python
import functools
import math

import jax
import jax.numpy as jnp
import numpy as np
from jax import lax
from jax.experimental import pallas as pl
from jax.experimental.pallas import tpu as pltpu

F32 = jnp.float32
BF16 = jnp.bfloat16

D_MODEL = 1024
BATCH = 8
SEQ = 2048
DEPTH = 2
HEAD_DIM = 64
DIL_CONFIGS = ((128, 1), (512, 4), (2048, 16))
N_DIL_GROUPS = 3
HEADS_PER_GROUP = 4
N_DIL_HEADS = 12
N_DIFF_HEADS = 4
DIFF_V_DIM = 128
N_ALIBI_HEADS = 16
D_FF = 2816
BLOCK = 128
EPS = 1e-6
DIL_WIDTH = 768
DIL_OUT = 256
DIFF_QK_WIDTH = 512
DIFF_V_WIDTH = 512
IN_COLS = 5888
NEG = -1e30

LANES = 128
VMEM_LIMIT_BYTES = 56 * 1024 * 1024

M_ROWS = BATCH * SEQ
SEG = 256
N_SEG = IN_COLS // SEG
FFN_TM = 512
FFN_FC = 256
PROJ_TM = 512
NAT_GATE, NAT_G0, NAT_QD, NAT_KD, NAT_VD = 0, 8, 11, 13, 15
NAT_COLS = 17 * SEG
DIL_JS = 8
DIFF_T = 512
DIFF_Q_GROUPS = ((0, 3), (1, 2))
LOG2E = math.log2(math.e)

SLOPES = tuple(float(np.float32(2.0) ** np.float32(-8.0 * i / N_ALIBI_HEADS)) for i in range(1, N_ALIBI_HEADS + 1))


def _rms_rows(x, gain):
    ms = jnp.mean(x * x, axis=-1, keepdims=True)
    return x * lax.rsqrt(ms + EPS) * gain


def _row_rsqrt(x):
    return lax.rsqrt(jnp.mean(x * x, axis=-1, keepdims=True) + EPS)


class _FfnWeightStager:
    def __init__(self, layer, gain_ref, win_hbm, wout_hbm, win_bf, wout_bf, stage_in, stage_out, sems):
        self.layer, self.gain_ref = layer, gain_ref
        self.win_hbm, self.wout_hbm, self.win_bf, self.wout_bf = win_hbm, wout_hbm, win_bf, wout_bf
        self.stage_in, self.stage_out, self.sems = stage_in, stage_out, sems

    def _copies(self, c):
        slot, lo = c % 2, c * FFN_FC
        return (
            pltpu.make_async_copy(self.win_hbm.at[self.layer, :, pl.ds(lo, FFN_FC)],
                                  self.stage_in.at[slot, 0], self.sems.at[slot, 0]),
            pltpu.make_async_copy(self.win_hbm.at[self.layer, :, pl.ds(D_FF + lo, FFN_FC)],
                                  self.stage_in.at[slot, 1], self.sems.at[slot, 1]),
            pltpu.make_async_copy(self.wout_hbm.at[self.layer, pl.ds(lo, FFN_FC), :],
                                  self.stage_out.at[slot], self.sems.at[slot, 2]),
        )

    def start(self, c):
        for cp in self._copies(c):
            cp.start()

    def finish(self, c):
        slot, lo = c % 2, c * FFN_FC
        for cp in self._copies(c):
            cp.wait()
        gain = jnp.concatenate([self.gain_ref[...]] * (FFN_FC // LANES), axis=1)
        self.win_bf[:, lo:lo + FFN_FC] = (self.stage_in[slot, 0] * gain).astype(BF16)
        self.win_bf[:, D_FF + lo:D_FF + lo + FFN_FC] = (self.stage_in[slot, 1] * gain).astype(BF16)
        self.wout_bf[lo:lo + FFN_FC, :] = self.stage_out[slot].astype(BF16)


def _ffn_rows(x, win_ref, wout_ref, a_scr, stager=None):
    h = x.astype(BF16)
    r = _row_rsqrt(x)
    n_chunks = D_FF // FFN_FC
    if stager is not None:
        stager.start(0)
    for c in range(n_chunks):
        lo = c * FFN_FC
        if stager is not None:
            if c + 1 < n_chunks:
                stager.start(c + 1)
            stager.finish(c)
        gate = jnp.dot(h, win_ref[:, lo:lo + FFN_FC], preferred_element_type=F32) * r
        up = jnp.dot(h, win_ref[:, D_FF + lo:D_FF + lo + FFN_FC], preferred_element_type=F32) * r
        a_scr[:, lo:lo + FFN_FC] = (gate * jax.nn.sigmoid(gate) * up).astype(BF16)
    y = jnp.dot(a_scr[...], wout_ref[...], preferred_element_type=F32)
    return x + 0.5 * y


def _ffn_tile(x_fn, o_ref, a_scr, stager):
    first = pl.program_id(0) == 0

    @pl.when(first)
    def _():
        o_ref[...] = _ffn_rows(x_fn(), stager.win_bf, stager.wout_bf, a_scr, stager)

    @pl.when(jnp.logical_not(first))
    def _():
        o_ref[...] = _ffn_rows(x_fn(), stager.win_bf, stager.wout_bf, a_scr)


def _proj_src_col(i):
    if i < 3 * N_DIL_GROUPS:
        return (i % 3) * DIL_WIDTH + (i // 3) * SEG
    return i * SEG


def _ffn_kernel(x_ref, gain_ref, win_hbm, wout_hbm, pw_ref, pgain_ref, o_ref, pw_bf_ref, a_scr, *stage_refs, layer):
    pgain = jnp.concatenate([pgain_ref[...]] * (SEG // LANES), axis=1)
    for j in range(N_SEG):
        src = _proj_src_col(j)
        pw_bf_ref[:, j * SEG:(j + 1) * SEG] = (pw_ref[:, src:src + SEG] * pgain).astype(BF16)
    _ffn_tile(lambda: x_ref[...], o_ref, a_scr, _FfnWeightStager(layer, gain_ref, win_hbm, wout_hbm, *stage_refs))


def _merge_ffn_kernel(x_ref, od_ref, of_ref, gate_ref, wa_ref, wb_ref, wo_ref, gain_ref, win_hbm, wout_hbm,
                      o_ref, a_scr, *stage_refs, layer):
    def merged():
        ya = jnp.dot(od_ref[...], wa_ref[...], preferred_element_type=F32)
        yb = jnp.dot(of_ref[...], wb_ref[...], preferred_element_type=F32)
        y = gate_ref[:, 0:D_MODEL].astype(F32) * ya + gate_ref[:, D_MODEL:2 * D_MODEL].astype(F32) * yb
        return x_ref[...] + jnp.dot(y.astype(BF16), wo_ref[...], preferred_element_type=F32)

    _ffn_tile(merged, o_ref, a_scr, _FfnWeightStager(layer, gain_ref, win_hbm, wout_hbm, *stage_refs))


def _resident(shape):
    return pl.BlockSpec(shape, lambda i: (0,) * len(shape), pipeline_mode=pl.Buffered(1))


def _resident_layer(shape, layer):
    return pl.BlockSpec((None,) + shape, lambda i: (layer,) + (0,) * len(shape), pipeline_mode=pl.Buffered(1))


def _ffn_weight_specs():
    return [_resident((D_MODEL, LANES)), pl.BlockSpec(memory_space=pl.ANY), pl.BlockSpec(memory_space=pl.ANY)]


def _ffn_scratch(tm):
    return [pltpu.VMEM((tm, D_FF), BF16),
            pltpu.VMEM((D_MODEL, 2 * D_FF), BF16), pltpu.VMEM((D_FF, D_MODEL), BF16),
            pltpu.VMEM((2, 2, D_MODEL, FFN_FC), F32), pltpu.VMEM((2, FFN_FC, D_MODEL), F32),
            pltpu.SemaphoreType.DMA((2, 3))]


def _lane_gain(norm_gain):
    return jnp.broadcast_to(norm_gain.reshape(D_MODEL, 1), (D_MODEL, LANES))


def _ffn(x, norm_gain, w_in, w_out, proj_w, proj_norm_gain, layer):
    tm = FFN_TM
    n_steps = M_ROWS // tm
    pw_rows = D_MODEL // n_steps
    row = lambda i: (i, 0)
    return pl.pallas_call(
        functools.partial(_ffn_kernel, layer=layer),
        out_shape=(jax.ShapeDtypeStruct((M_ROWS, D_MODEL), F32), jax.ShapeDtypeStruct((D_MODEL, IN_COLS), BF16)),
        grid=(n_steps,),
        in_specs=[pl.BlockSpec((tm, D_MODEL), row)] + _ffn_weight_specs() + [
            pl.BlockSpec((None, pw_rows, IN_COLS), lambda i: (layer, i, 0)),
            pl.BlockSpec((pw_rows, LANES), row)],
        out_specs=(pl.BlockSpec((tm, D_MODEL), row), pl.BlockSpec((pw_rows, IN_COLS), row)),
        scratch_shapes=_ffn_scratch(tm),
        compiler_params=pltpu.CompilerParams(
            dimension_semantics=("arbitrary",), vmem_limit_bytes=VMEM_LIMIT_BYTES),
        name="ffn",
    )(x, _lane_gain(norm_gain), w_in, w_out, proj_w, _lane_gain(proj_norm_gain))


def _merge_ffn(x, o_dil, o_diff, gates, wa, wb, wo, norm_gain, w_in, w_out, layer):
    tm = FFN_TM
    row = lambda i: (i, 0)
    return pl.pallas_call(
        functools.partial(_merge_ffn_kernel, layer=layer),
        out_shape=jax.ShapeDtypeStruct((M_ROWS, D_MODEL), F32),
        grid=(M_ROWS // tm,),
        in_specs=[
            pl.BlockSpec((tm, D_MODEL), row),
            pl.BlockSpec((tm, DIL_OUT), row),
            pl.BlockSpec((tm, DIFF_V_WIDTH), row),
            pl.BlockSpec((tm, 2 * D_MODEL), row),
            _resident_layer((DIL_OUT, D_MODEL), layer),
            _resident_layer((DIFF_V_WIDTH, D_MODEL), layer),
            _resident_layer((D_MODEL, D_MODEL), layer),
        ] + _ffn_weight_specs(),
        out_specs=pl.BlockSpec((tm, D_MODEL), row),
        scratch_shapes=_ffn_scratch(tm),
        compiler_params=pltpu.CompilerParams(
            dimension_semantics=("arbitrary",), vmem_limit_bytes=VMEM_LIMIT_BYTES),
        name="merge_ffn",
    )(x, o_dil, o_diff, gates, wa, wb, wo, _lane_gain(norm_gain), w_in, w_out)


def _proj_kernel(x_ref, w_ref, gain_ref,
                 o_nat, o_g1, o_g2, scr1, scr2, scr3):
    tm = PROJ_TM
    x = x_ref[...]
    h = x.astype(BF16)
    r = _row_rsqrt(x)
    lane_lo = lax.broadcasted_iota(jnp.int32, (1, LANES), 1) < HEAD_DIM

    def seg(i):
        return jnp.dot(h, w_ref[:, i * SEG:(i + 1) * SEG], preferred_element_type=F32) * r

    def qk_norm(y, gi):
        tiles = []
        for u in range(SEG // LANES):
            yt = y[:, u * LANES:(u + 1) * LANES]
            sq = yt * yt
            low = jnp.sum(jnp.where(lane_lo, sq, 0.0), axis=-1, keepdims=True)
            high = jnp.sum(jnp.where(lane_lo, 0.0, sq), axis=-1, keepdims=True)
            ssq = jnp.where(lane_lo, low, high)
            tiles.append(yt * lax.rsqrt(ssq + HEAD_DIM * EPS))
        return jnp.concatenate(tiles, axis=-1) * gain_ref[gi:gi + 1, :]

    gate_segments = iter(range(8))

    def put_nat(col_block, y):
        o_nat[:, col_block * SEG:(col_block + 1) * SEG] = y.astype(BF16)

    def gate():
        t = next(gate_segments)
        put_nat(NAT_GATE + t, jax.nn.sigmoid(seg(15 + t)))

    n_t = DIL_WIDTH // LANES
    quarter = tm // 4
    for g, (scr, o_g) in ((1, (scr1, o_g1)), (2, (scr2, o_g2))):
        ys = []
        for kind in range(2):
            ys.append(qk_norm(seg(3 * g + kind), 2 * g + kind))
            gate()
        ys.append(seg(3 * g + 2))
        for t, y in enumerate(ys):
            for u in range(SEG // LANES):
                scr[2 * t + u] = y[:, u * LANES:(u + 1) * LANES]
        if g == 1:
            for c in range(4):
                for t in range(n_t):
                    o_g[0, c, :, t * LANES:(t + 1) * LANES] = scr[t, pl.ds(c, quarter, stride=4), :].astype(BF16)
        else:
            for c1 in range(4):
                for t in range(n_t):
                    scr3[t, c1 * quarter:(c1 + 1) * quarter, :] = scr[t, pl.ds(c1, quarter, stride=4), :]
            for c in range(16):
                c1, c2 = c % 4, c // 4
                for t in range(n_t):
                    o_g[0, c, :, t * LANES:(t + 1) * LANES] = (
                        scr3[t, pl.ds(c1 * quarter + c2, tm // 16, stride=4), :].astype(BF16))
    for t in range(2):
        put_nat(NAT_QD + t, qk_norm(seg(9 + t), 6 + t))
        gate()
        put_nat(NAT_KD + t, qk_norm(seg(11 + t), 8 + t))
        gate()
    put_nat(NAT_G0, qk_norm(seg(0), 0))
    put_nat(NAT_G0 + 1, qk_norm(seg(1), 1))
    put_nat(NAT_G0 + 2, seg(2))
    for t in range(2):
        put_nat(NAT_VD + t, seg(13 + t))


def _proj(x, w, qk_gain):
    tm = PROJ_TM
    tiles_per_seq = SEQ // tm
    const = lambda i: (0, 0)
    row = lambda i: (i, 0)
    perm = lambda i: (i // tiles_per_seq, 0, i % tiles_per_seq, 0)
    out_shape = (
        jax.ShapeDtypeStruct((M_ROWS, NAT_COLS), BF16),
        jax.ShapeDtypeStruct((BATCH, 4, SEQ // 4, DIL_WIDTH), BF16),
        jax.ShapeDtypeStruct((BATCH, 16, SEQ // 16, DIL_WIDTH), BF16),
    )
    out_specs = (
        pl.BlockSpec((tm, NAT_COLS), row),
        pl.BlockSpec((1, 4, tm // 4, DIL_WIDTH), perm),
        pl.BlockSpec((1, 16, tm // 16, DIL_WIDTH), perm),
    )
    return pl.pallas_call(
        _proj_kernel,
        out_shape=out_shape,
        grid=(M_ROWS // tm,),
        in_specs=[
            pl.BlockSpec((tm, D_MODEL), row),
            _resident((D_MODEL, IN_COLS)),
            pl.BlockSpec((16, SEG), const),
        ],
        out_specs=out_specs,
        scratch_shapes=[pltpu.VMEM((DIL_WIDTH // LANES, tm, LANES), F32)] * 3,
        compiler_params=pltpu.CompilerParams(
            dimension_semantics=("arbitrary",), vmem_limit_bytes=VMEM_LIMIT_BYTES),
        name="proj",
    )(x, w, qk_gain)


def _dil_block(qkv, row_cur, row_prev, bias, lane_lo):
    def rows(which, row):
        ref, col = qkv[which]
        return ref[0, pl.ds(row, BLOCK), col:col + SEG]

    q, k, v = rows(0, row_cur), rows(1, row_cur), rows(2, row_cur)
    if row_prev is not None:
        k = jnp.concatenate([rows(1, row_prev), k], axis=0)
        v = jnp.concatenate([rows(2, row_prev), v], axis=0)
    zero = jnp.zeros((BLOCK, LANES), BF16)
    copies = []
    for j in range(HEADS_PER_GROUP):
        qt = q[:, (j // 2) * LANES:(j // 2 + 1) * LANES]
        kept = jnp.where(lane_lo, qt, zero) if j % 2 == 0 else jnp.where(lane_lo, zero, qt)
        copies.append(jnp.concatenate([kept, zero] if j < 2 else [zero, kept], axis=1))
    qs = jnp.concatenate(copies, axis=0)
    s = lax.dot_general(qs, k, (((1,), (1,)), ((), ())), preferred_element_type=F32) + bias
    m = jnp.max(s, axis=-1, keepdims=True)
    p = jnp.exp2(s - m)
    l = jnp.sum(p, axis=-1, keepdims=True)
    pv = jnp.dot(p.astype(BF16), v, preferred_element_type=F32)
    stats = []
    for u in range(SEG // LANES):
        ra, rb = slice(2 * u * BLOCK, (2 * u + 1) * BLOCK), slice((2 * u + 1) * BLOCK, (2 * u + 2) * BLOCK)
        cols = slice(u * LANES, (u + 1) * LANES)
        stats.append((jnp.where(lane_lo, pv[ra, cols], pv[rb, cols]),
                      jnp.where(lane_lo, m[ra], m[rb]), jnp.where(lane_lo, l[ra], l[rb])))
    return stats


def _dil_kernel(q0_ref, k0_ref, v0_ref, g1_ref, g2_ref, bias_ref, o_ref, o_scr, m_scr, l_scr):
    step = pl.program_id(1)
    lane_lo = lax.broadcasted_iota(jnp.int32, (1, LANES), 1) < HEAD_DIM
    g0_qkv = ((q0_ref, 0), (k0_ref, 0), (v0_ref, 0))
    g1_qkv, g2_qkv = (tuple((ref, kind * SEG) for kind in range(3)) for ref in (g1_ref, g2_ref))

    def put(g, rows, stats):
        for u, (o, m, l) in enumerate(stats):
            o_scr[g, u, rows, :] = o
            m_scr[g, u, rows, :] = m
            l_scr[g, u, rows, :] = l

    for jj in range(DIL_JS):
        j = step * DIL_JS + jj
        row = pl.multiple_of(j * BLOCK, BLOCK)

        prev = pl.multiple_of(jnp.maximum(j - 1, 0) * BLOCK, BLOCK)
        tab = jnp.where(j == 0, N_DIL_GROUPS, 0) if jj == 0 else 0
        put(0, pl.ds(row, BLOCK), _dil_block(g0_qkv, row, prev, bias_ref[tab], lane_lo))

        c, n = step * (DIL_JS // 4) + jj // 4, jj % 4
        if n == 0:
            stats = _dil_block(g1_qkv, row, None, bias_ref[1, :, BLOCK:2 * BLOCK], lane_lo)
        else:
            prev = pl.multiple_of(row - BLOCK, BLOCK)
            stats = _dil_block(g1_qkv, row, prev, bias_ref[1], lane_lo)
        put(1, pl.ds(n * (4 * BLOCK) + c, BLOCK, stride=4), stats)

        put(2, pl.ds(j, BLOCK, stride=16),
            _dil_block(g2_qkv, row, None, bias_ref[2, :, BLOCK:2 * BLOCK], lane_lo))

    @pl.when(step == pl.num_programs(1) - 1)
    def _():
        rows = 256
        for r in range(SEQ // rows):
            sl = pl.ds(r * rows, rows)
            for u in range(SEG // LANES):
                m0, m1, m2 = m_scr[0, u, sl, :], m_scr[1, u, sl, :], m_scr[2, u, sl, :]
                mx = jnp.maximum(jnp.maximum(m0, m1), m2)
                w0, w1, w2 = jnp.exp2(m0 - mx), jnp.exp2(m1 - mx), jnp.exp2(m2 - mx)
                num = w0 * o_scr[0, u, sl, :] + w1 * o_scr[1, u, sl, :] + w2 * o_scr[2, u, sl, :]
                den = w0 * l_scr[0, u, sl, :] + w1 * l_scr[1, u, sl, :] + w2 * l_scr[2, u, sl, :]
                o_ref[0, sl, u * LANES:(u + 1) * LANES] = (num / den).astype(BF16)


def _dil_bias_table():
    qi = np.arange(BLOCK)[:, None]
    ki = np.arange(2 * BLOCK)[None, :]
    dist = BLOCK + qi - ki
    valid = (dist >= 0) & (dist <= BLOCK)
    tab = np.full((N_DIL_GROUPS + 1, HEADS_PER_GROUP * BLOCK, 2 * BLOCK), NEG, np.float32)
    for g, (_, dil) in enumerate(DIL_CONFIGS):
        for h in range(HEADS_PER_GROUP):
            slope = np.float32(SLOPES[g * HEADS_PER_GROUP + h])
            b = np.where(valid, -slope * np.float32(LOG2E) * (dil * dist).astype(np.float32),
                         np.float32(NEG)).astype(np.float32)
            tab[g, h * BLOCK:(h + 1) * BLOCK] = b
            if g == 0:
                tab[N_DIL_GROUPS, h * BLOCK:(h + 1) * BLOCK] = np.where(ki >= BLOCK, b, np.float32(NEG))
    return tab


def _nat_block_spec(col_block_fn):
    return pl.BlockSpec((1, SEQ, SEG), lambda b, *rest: (b, 0, col_block_fn(*rest)))


def _dil_attn(nat, g1, g2, bias):
    n_blk = SEQ // BLOCK
    seq_spec = pl.BlockSpec((1, SEQ, DIL_WIDTH), lambda b, j: (b, 0, 0))
    g0_specs = [_nat_block_spec(lambda j, kind=kind: NAT_G0 + kind) for kind in range(3)]
    return pl.pallas_call(
        _dil_kernel,
        out_shape=jax.ShapeDtypeStruct((BATCH, SEQ, DIL_OUT), BF16),
        grid=(BATCH, n_blk // DIL_JS),
        in_specs=g0_specs + [seq_spec, seq_spec,
                             pl.BlockSpec(bias.shape, lambda b, j: (0, 0, 0))],
        out_specs=pl.BlockSpec((1, SEQ, DIL_OUT), lambda b, j: (b, 0, 0)),
        scratch_shapes=[pltpu.VMEM((N_DIL_GROUPS, DIL_OUT // LANES, SEQ, LANES), F32)] * 3,
        compiler_params=pltpu.CompilerParams(
            dimension_semantics=("arbitrary", "arbitrary"), vmem_limit_bytes=VMEM_LIMIT_BYTES),
        name="dil_attn",
    )(nat, nat, nat, g1, g2, bias)


def _diff_kernel(q_ref, k_ref, v_ref, lq_ref, lk_ref, sn_ref, o_ref,
                 s_scr, mp_scr, m_scr, acc_scr, *, lam_init):
    t = DIFF_T
    n_lt = t // LANES
    n_q = SEQ // t
    pair = pl.program_id(1)
    qp = pl.program_id(2)
    lane_map = lax.broadcasted_iota(jnp.int32, (1, SEG), 1) // HEAD_DIM

    def stacked_q(qi):
        q = q_ref[0, qi * t:(qi + 1) * t, :]
        zero = jnp.zeros_like(q)
        return [jnp.where(lane_map == g, q, zero) for g in range(4)]
    slopes = (jnp.where(pair == 0, SLOPES[N_DIL_HEADS] * LOG2E, SLOPES[N_DIL_HEADS + 2] * LOG2E).astype(F32),
              jnp.where(pair == 0, SLOPES[N_DIL_HEADS + 1] * LOG2E, SLOPES[N_DIL_HEADS + 3] * LOG2E).astype(F32))
    lane_k = lax.broadcasted_iota(jnp.int32, (1, t), 1)
    lane_v0 = lax.broadcasted_iota(jnp.int32, (1, SEG), 1) < DIFF_V_DIM

    def lane_tiles(x):
        return [x[:, u * LANES:(u + 1) * LANES] for u in range(x.shape[1] // LANES)]

    half = t // 2
    above = (lax.broadcasted_iota(jnp.int32, (half, half), 1)
             > lax.broadcasted_iota(jnp.int32, (half, half), 0))
    nt = (((1,), (1,)), ((), ()))

    def pass1(qs, kb, diag, init):
        k = k_ref[0, kb * t:(kb + 1) * t, :]
        bias = [sl * (kb * t + lane_k).astype(F32) for sl in slopes]
        for g in range(4):
            rows = pl.ds(g * t, t)
            if diag:
                sa = lax.dot_general(qs[g][:half], k[:half], nt, preferred_element_type=F32)
                sa = jnp.where(above, NEG, sa + bias[g // 2][:, :half])
                sb = lax.dot_general(qs[g][half:], k, nt, preferred_element_type=F32) + bias[g // 2]
                sb = jnp.concatenate([sb[:, :half], jnp.where(above, NEG, sb[:, half:])], axis=1)
                s_scr[kb, pl.ds(g * t, half), 0:half] = sa
                s_scr[kb, pl.ds(g * t + half, half), :] = sb
                ta, tb = lane_tiles(sa), lane_tiles(sb)
                tmax = jnp.concatenate([jnp.maximum(ta[0], ta[1]),
                                        jnp.maximum(jnp.maximum(tb[0], tb[1]), jnp.maximum(tb[2], tb[3]))], axis=0)
            else:
                s = lax.dot_general(qs[g], k, nt, preferred_element_type=F32) + bias[g // 2]
                s_scr[kb, rows, :] = s
                tl = lane_tiles(s)
                tmax = jnp.maximum(jnp.maximum(tl[0], tl[1]), jnp.maximum(tl[2], tl[3]))
            if not init:
                tmax = jnp.maximum(mp_scr[rows, :], tmax)
            if diag:
                m_scr[rows, :] = jnp.broadcast_to(jnp.max(tmax, axis=-1, keepdims=True), (t, LANES))
            else:
                mp_scr[rows, :] = tmax

    lq = lq_ref[...]
    lk = lk_ref[...]
    lam = (jnp.exp(jnp.sum(lq[0:1] * lk[0:1], axis=-1, keepdims=True))
           - jnp.exp(jnp.sum(lq[1:2] * lk[1:2], axis=-1, keepdims=True)) + lam_init)

    def pass2(qi, kb, init, last):
        v = v_ref[0, kb * t:(kb + 1) * t, :]
        one = jnp.ones_like(v)
        vh = (jnp.where(lane_v0, v, one), jnp.where(lane_v0, one, v))
        done = []
        for g in range(4):
            rows = pl.ds(g * t, t)
            m = m_scr[rows, :]
            if last:
                pa = jnp.exp2(s_scr[kb, pl.ds(g * t, half), 0:half]
                              - jnp.concatenate([m[:half]] * (half // LANES), axis=1))
                pb = jnp.exp2(s_scr[kb, pl.ds(g * t + half, half), :] - jnp.concatenate([m[half:]] * n_lt, axis=1))
                pv = jnp.concatenate(
                    [jnp.dot(pa.astype(BF16), vh[g // 2][:half], preferred_element_type=F32),
                     jnp.dot(pb.astype(BF16), vh[g // 2], preferred_element_type=F32)], axis=0)
            else:
                p = jnp.exp2(s_scr[kb, rows, :] - jnp.concatenate([m] * n_lt, axis=1))
                pv = jnp.dot(p.astype(BF16), vh[g // 2], preferred_element_type=F32)
            acc = pv if init else acc_scr[rows, :] + pv
            if not last:
                acc_scr[rows, :] = acc
                continue
            done.append(acc)
            if g % 2 == 1:
                hh = g // 2
                cols = slice(hh * DIFF_V_DIM, (hh + 1) * DIFF_V_DIM)
                sums = slice((1 - hh) * DIFF_V_DIM, (2 - hh) * DIFF_V_DIM)
                a1, a2 = done[g - 1], done[g]
                o = a1[:, cols] / a1[:, sums] - lam * (a2[:, cols] / a2[:, sums])
                o_ref[0, qi * t:(qi + 1) * t, cols] = (
                    _rms_rows(o, sn_ref[0, hh:hh + 1, :]) * (1.0 - lam_init)).astype(BF16)

    for c, group in enumerate(DIFF_Q_GROUPS):
        @pl.when(qp == c)
        def _(group=group):
            for qi in group:
                qs = stacked_q(qi)
                for kb in range(qi):
                    pass1(qs, kb, False, kb == 0)
                pass1(qs, qi, True, qi == 0)
                for kb in range(qi):
                    pass2(qi, kb, kb == 0, False)
                pass2(qi, qi, qi == 0, True)


def _diff_attn(nat, lq, lk, subnorm, lam_init):
    t = DIFF_T
    assert SEQ // t == 4
    seq_spec = pl.BlockSpec((1, SEQ, SEG), lambda b, p, i: (b, 0, p))
    return pl.pallas_call(
        functools.partial(_diff_kernel, lam_init=lam_init),
        out_shape=jax.ShapeDtypeStruct((BATCH, SEQ, DIFF_V_WIDTH), BF16),
        grid=(BATCH, 2, len(DIFF_Q_GROUPS)),
        in_specs=[
            _nat_block_spec(lambda p, i: NAT_QD + p), _nat_block_spec(lambda p, i: NAT_KD + p),
            _nat_block_spec(lambda p, i: NAT_VD + p),
            pl.BlockSpec((2, HEAD_DIM), lambda b, p, i: (0, 0)),
            pl.BlockSpec((2, HEAD_DIM), lambda b, p, i: (0, 0)),
            pl.BlockSpec((1, 2, DIFF_V_DIM), lambda b, p, i: (p, 0, 0)),
        ],
        out_specs=seq_spec,
        scratch_shapes=[pltpu.VMEM((SEQ // t, 4 * t, t), F32),
                        pltpu.VMEM((4 * t, LANES), F32),
                        pltpu.VMEM((4 * t, LANES), F32),
                        pltpu.VMEM((4 * t, SEG), F32)],
        compiler_params=pltpu.CompilerParams(
            dimension_semantics=("arbitrary", "arbitrary", "arbitrary"), vmem_limit_bytes=VMEM_LIMIT_BYTES),
        name="diff_attn",
    )(nat, nat, nat, lq, lk, subnorm)


def _qk_gain_rows(qk_gain_dil, qk_gain_diff):
    k_scale = HEAD_DIM ** 0.5
    q_scale = LOG2E
    rows = []
    for g in range(N_DIL_GROUPS):
        rows.append(qk_gain_dil[0, g].reshape(SEG) * q_scale)
        rows.append(qk_gain_dil[1, g].reshape(SEG) * k_scale)
    qd = qk_gain_diff[0].reshape(2, SEG) * q_scale
    kd = qk_gain_diff[1].reshape(2, SEG) * k_scale
    rows += [qd[0], qd[1], kd[0], kd[1]]
    rows += [jnp.zeros((SEG,), F32)] * (16 - len(rows))
    return jnp.stack(rows).astype(F32)


def kernel(x, ffn1_norm, ffn1_w_in, ffn1_w_out, mix_norm, w_in, qk_gain_dil, qk_gain_diff, lambda_q, lambda_k,
           diff_subnorm, w_branch_dil, w_branch_diff, w_out, ffn2_norm, ffn2_w_in, ffn2_w_out):
    b, s, d = x.shape
    assert (b, s, d) == (BATCH, SEQ, D_MODEL)
    xr = x.reshape(M_ROWS, D_MODEL)
    dil_bias = jnp.asarray(_dil_bias_table())
    wa, wb, wo = w_branch_dil.astype(BF16), w_branch_diff.astype(BF16), w_out.astype(BF16)
    for l in range(DEPTH):
        lam_init = 0.8 - 0.6 * math.exp(-0.3 * l)
        xr, proj_w = _ffn(xr, ffn1_norm[l], ffn1_w_in, ffn1_w_out, w_in, mix_norm[l], l)
        nat, g1, g2 = _proj(xr, proj_w, _qk_gain_rows(qk_gain_dil[l], qk_gain_diff[l]))
        nat_seq = nat.reshape(BATCH, SEQ, NAT_COLS)
        o_dil = _dil_attn(nat_seq, g1.reshape(BATCH, SEQ, DIL_WIDTH), g2.reshape(BATCH, SEQ, DIL_WIDTH), dil_bias)
        o_diff = _diff_attn(nat_seq, lambda_q[l], lambda_k[l], diff_subnorm[l].reshape(2, 2, DIFF_V_DIM), lam_init)
        assert NAT_GATE == 0
        xr = _merge_ffn(xr, o_dil.reshape(M_ROWS, DIL_OUT), o_diff.reshape(M_ROWS, DIFF_V_WIDTH), nat,
                        wa, wb, wo, ffn2_norm[l], ffn2_w_in, ffn2_w_out, l)
    return xr.reshape(BATCH, SEQ, D_MODEL)
```

```python
import functools
import math

import jax
import jax.numpy as jnp
import numpy as np
from jax import lax
from jax.experimental import pallas as pl
from jax.experimental.pallas import tpu as pltpu

F32 = jnp.float32
BF16 = jnp.bfloat16

D_MODEL = 1024
BATCH = 8
SEQ = 2048
DEPTH = 2
HEAD_DIM = 64
DIL_CONFIGS = ((128, 1), (512, 4), (2048, 16))
N_DIL_GROUPS = 3
HEADS_PER_GROUP = 4
N_DIL_HEADS = 12
N_DIFF_HEADS = 4
DIFF_V_DIM = 128
N_ALIBI_HEADS = 16
D_FF = 2816
BLOCK = 128
EPS = 1e-6
DIL_WIDTH = 768
DIL_OUT = 256
DIFF_QK_WIDTH = 512
DIFF_V_WIDTH = 512
IN_COLS = 5888
NEG = -1e30

LANES = 128
VMEM_LIMIT_BYTES = 56 * 1024 * 1024

M_ROWS = BATCH * SEQ
SEG = 256
N_SEG = IN_COLS // SEG
FFN_TM = 512
FFN_FC = 256
PROJ_TM = 512
NAT_GATE, NAT_G0, NAT_QD, NAT_KD, NAT_VD = 0, 8, 11, 13, 15
NAT_COLS = 17 * SEG
DIL_JS = 8
DIFF_T = 512
DIFF_Q_GROUPS = ((0, 3), (1, 2))
LOG2E = math.log2(math.e)

SLOPES = tuple(float(np.float32(2.0) ** np.float32(-8.0 * i / N_ALIBI_HEADS)) for i in range(1, N_ALIBI_HEADS + 1))


def _rms_rows(x, gain):
    ms = jnp.mean(x * x, axis=-1, keepdims=True)
    return x * lax.rsqrt(ms + EPS) * gain


def _row_rsqrt(x):
    return lax.rsqrt(jnp.mean(x * x, axis=-1, keepdims=True) + EPS)


class _FfnWeightStager:
    def __init__(self, layer, gain_ref, win_hbm, wout_hbm, win_bf, wout_bf, stage_in, stage_out, sems):
        self.layer, self.gain_ref = layer, gain_ref
        self.win_hbm, self.wout_hbm, self.win_bf, self.wout_bf = win_hbm, wout_hbm, win_bf, wout_bf
        self.stage_in, self.stage_out, self.sems = stage_in, stage_out, sems

    def _copies(self, c):
        slot, lo = c % 2, c * FFN_FC
        return (
            pltpu.make_async_copy(self.win_hbm.at[self.layer, :, pl.ds(lo, FFN_FC)],
                                  self.stage_in.at[slot, 0], self.sems.at[slot, 0]),
            pltpu.make_async_copy(self.win_hbm.at[self.layer, :, pl.ds(D_FF + lo, FFN_FC)],
                                  self.stage_in.at[slot, 1], self.sems.at[slot, 1]),
            pltpu.make_async_copy(self.wout_hbm.at[self.layer, pl.ds(lo, FFN_FC), :],
                                  self.stage_out.at[slot], self.sems.at[slot, 2]),
        )

    def start(self, c):
        for cp in self._copies(c):
            cp.start()

    def finish(self, c):
        slot, lo = c % 2, c * FFN_FC
        for cp in self._copies(c):
            cp.wait()
        gain = jnp.concatenate([self.gain_ref[...]] * (FFN_FC // LANES), axis=1)
        self.win_bf[:, lo:lo + FFN_FC] = (self.stage_in[slot, 0] * gain).astype(BF16)
        self.win_bf[:, D_FF + lo:D_FF + lo + FFN_FC] = (self.stage_in[slot, 1] * gain).astype(BF16)
        self.wout_bf[lo:lo + FFN_FC, :] = self.stage_out[slot].astype(BF16)


def _ffn_rows(x, win_ref, wout_ref, a_scr, stager=None):
    h = x.astype(BF16)
    r = _row_rsqrt(x)
    n_chunks = D_FF // FFN_FC
    if stager is not None:
        stager.start(0)
    for c in range(n_chunks):
        lo = c * FFN_FC
        if stager is not None:
            if c + 1 < n_chunks:
                stager.start(c + 1)
            stager.finish(c)
        gate = jnp.dot(h, win_ref[:, lo:lo + FFN_FC], preferred_element_type=F32) * r
        up = jnp.dot(h, win_ref[:, D_FF + lo:D_FF + lo + FFN_FC], preferred_element_type=F32) * r
        a_scr[:, lo:lo + FFN_FC] = (gate * jax.nn.sigmoid(gate) * up).astype(BF16)
    y = jnp.dot(a_scr[...], wout_ref[...], preferred_element_type=F32)
    return x + 0.5 * y


def _ffn_tile(x_fn, o_ref, a_scr, stager):
    first = pl.program_id(0) == 0

    @pl.when(first)
    def _():
        o_ref[...] = _ffn_rows(x_fn(), stager.win_bf, stager.wout_bf, a_scr, stager)

    @pl.when(jnp.logical_not(first))
    def _():
        o_ref[...] = _ffn_rows(x_fn(), stager.win_bf, stager.wout_bf, a_scr)


def _proj_src_col(i):
    if i < 3 * N_DIL_GROUPS:
        return (i % 3) * DIL_WIDTH + (i // 3) * SEG
    return i * SEG


def _ffn_kernel(x_ref, gain_ref, win_hbm, wout_hbm, pw_ref, pgain_ref, o_ref, pw_bf_ref, a_scr, *stage_refs, layer):
    pgain = jnp.concatenate([pgain_ref[...]] * (SEG // LANES), axis=1)
    for j in range(N_SEG):
        src = _proj_src_col(j)
        pw_bf_ref[:, j * SEG:(j + 1) * SEG] = (pw_ref[:, src:src + SEG] * pgain).astype(BF16)
    _ffn_tile(lambda: x_ref[...], o_ref, a_scr, _FfnWeightStager(layer, gain_ref, win_hbm, wout_hbm, *stage_refs))


def _merge_ffn_kernel(x_ref, od_ref, of_ref, gate_ref, wa_ref, wb_ref, wo_ref, gain_ref, win_hbm, wout_hbm,
                      o_ref, a_scr, *stage_refs, layer):
    def merged():
        ya = jnp.dot(od_ref[...], wa_ref[...], preferred_element_type=F32)
        yb = jnp.dot(of_ref[...], wb_ref[...], preferred_element_type=F32)
        y = gate_ref[:, 0:D_MODEL].astype(F32) * ya + gate_ref[:, D_MODEL:2 * D_MODEL].astype(F32) * yb
        return x_ref[...] + jnp.dot(y.astype(BF16), wo_ref[...], preferred_element_type=F32)

    _ffn_tile(merged, o_ref, a_scr, _FfnWeightStager(layer, gain_ref, win_hbm, wout_hbm, *stage_refs))


def _resident(shape):
    return pl.BlockSpec(shape, lambda i: (0,) * len(shape), pipeline_mode=pl.Buffered(1))


def _resident_layer(shape, layer):
    return pl.BlockSpec((None,) + shape, lambda i: (layer,) + (0,) * len(shape), pipeline_mode=pl.Buffered(1))


NORM_FFN1, NORM_MIX, NORM_FFN2 = 0, 1, 2


def _lane_gains(ffn1_norm, mix_norm, ffn2_norm):
    stacked = jnp.stack([ffn1_norm, mix_norm, ffn2_norm])
    return jnp.broadcast_to(stacked[..., None], stacked.shape + (LANES,))


def _ffn_weight_specs(kind, layer):
    gain = pl.BlockSpec((None, None, D_MODEL, LANES), lambda i: (kind, layer, 0, 0), pipeline_mode=pl.Buffered(1))
    return [gain, pl.BlockSpec(memory_space=pl.ANY), pl.BlockSpec(memory_space=pl.ANY)]


def _ffn_scratch(tm):
    return [pltpu.VMEM((tm, D_FF), BF16),
            pltpu.VMEM((D_MODEL, 2 * D_FF), BF16), pltpu.VMEM((D_FF, D_MODEL), BF16),
            pltpu.VMEM((2, 2, D_MODEL, FFN_FC), F32), pltpu.VMEM((2, FFN_FC, D_MODEL), F32),
            pltpu.SemaphoreType.DMA((2, 3))]


def _ffn(x, lane_gains, w_in, w_out, proj_w, layer):
    tm = FFN_TM
    n_steps = M_ROWS // tm
    pw_rows = D_MODEL // n_steps
    row = lambda i: (i, 0)
    return pl.pallas_call(
        functools.partial(_ffn_kernel, layer=layer),
        out_shape=(jax.ShapeDtypeStruct((M_ROWS, D_MODEL), F32), jax.ShapeDtypeStruct((D_MODEL, IN_COLS), BF16)),
        grid=(n_steps,),
        in_specs=[pl.BlockSpec((tm, D_MODEL), row)] + _ffn_weight_specs(NORM_FFN1, layer) + [
            pl.BlockSpec((None, pw_rows, IN_COLS), lambda i: (layer, i, 0)),
            pl.BlockSpec((None, None, pw_rows, LANES), lambda i: (NORM_MIX, layer, i, 0))],
        out_specs=(pl.BlockSpec((tm, D_MODEL), row), pl.BlockSpec((pw_rows, IN_COLS), row)),
        scratch_shapes=_ffn_scratch(tm),
        compiler_params=pltpu.CompilerParams(
            dimension_semantics=("arbitrary",), vmem_limit_bytes=VMEM_LIMIT_BYTES),
        name="ffn",
    )(x, lane_gains, w_in, w_out, proj_w, lane_gains)


def _merge_ffn(x, o_dil, o_diff, gates, wa, wb, wo, lane_gains, w_in, w_out, layer):
    tm = FFN_TM
    row = lambda i: (i, 0)
    return pl.pallas_call(
        functools.partial(_merge_ffn_kernel, layer=layer),
        out_shape=jax.ShapeDtypeStruct((M_ROWS, D_MODEL), F32),
        grid=(M_ROWS // tm,),
        in_specs=[
            pl.BlockSpec((tm, D_MODEL), row),
            pl.BlockSpec((tm, DIL_OUT), row),
            pl.BlockSpec((tm, DIFF_V_WIDTH), row),
            pl.BlockSpec((tm, 2 * D_MODEL), row),
            _resident_layer((DIL_OUT, D_MODEL), layer),
            _resident_layer((DIFF_V_WIDTH, D_MODEL), layer),
            _resident_layer((D_MODEL, D_MODEL), layer),
        ] + _ffn_weight_specs(NORM_FFN2, layer),
        out_specs=pl.BlockSpec((tm, D_MODEL), row),
        scratch_shapes=_ffn_scratch(tm),
        compiler_params=pltpu.CompilerParams(
            dimension_semantics=("arbitrary",), vmem_limit_bytes=VMEM_LIMIT_BYTES),
        name="merge_ffn",
    )(x, o_dil, o_diff, gates, wa, wb, wo, lane_gains, w_in, w_out)


def _proj_kernel(x_ref, w_ref, dil_gain_ref, diff_gain_ref,
                 o_nat, o_g1, o_g2, scr1, scr2, scr3):
    tm = PROJ_TM
    x = x_ref[...]
    h = x.astype(BF16)
    r = _row_rsqrt(x)
    lane_lo = lax.broadcasted_iota(jnp.int32, (1, LANES), 1) < HEAD_DIM

    def seg(i):
        return jnp.dot(h, w_ref[:, i * SEG:(i + 1) * SEG], preferred_element_type=F32) * r

    kind_scale = (LOG2E, HEAD_DIM ** 0.5)

    def dil_gain(kind, g):
        return dil_gain_ref[3 * kind + g:3 * kind + g + 1, :] * kind_scale[kind]

    def diff_gain(kind, t):
        return diff_gain_ref[2 * kind + t:2 * kind + t + 1, :] * kind_scale[kind]

    def qk_norm(y, gain_row):
        tiles = []
        for u in range(SEG // LANES):
            yt = y[:, u * LANES:(u + 1) * LANES]
            sq = yt * yt
            low = jnp.sum(jnp.where(lane_lo, sq, 0.0), axis=-1, keepdims=True)
            high = jnp.sum(jnp.where(lane_lo, 0.0, sq), axis=-1, keepdims=True)
            ssq = jnp.where(lane_lo, low, high)
            tiles.append(yt * lax.rsqrt(ssq + HEAD_DIM * EPS))
        return jnp.concatenate(tiles, axis=-1) * gain_row

    gate_segments = iter(range(8))

    def put_nat(col_block, y):
        o_nat[:, col_block * SEG:(col_block + 1) * SEG] = y.astype(BF16)

    def gate():
        t = next(gate_segments)
        put_nat(NAT_GATE + t, jax.nn.sigmoid(seg(15 + t)))

    n_t = DIL_WIDTH // LANES
    quarter = tm // 4
    for g, (scr, o_g) in ((1, (scr1, o_g1)), (2, (scr2, o_g2))):
        ys = []
        for kind in range(2):
            ys.append(qk_norm(seg(3 * g + kind), dil_gain(kind, g)))
            gate()
        ys.append(seg(3 * g + 2))
        for t, y in enumerate(ys):
            for u in range(SEG // LANES):
                scr[2 * t + u] = y[:, u * LANES:(u + 1) * LANES]
        if g == 1:
            for c in range(4):
                for t in range(n_t):
                    o_g[0, c, :, t * LANES:(t + 1) * LANES] = scr[t, pl.ds(c, quarter, stride=4), :].astype(BF16)
        else:
            for c1 in range(4):
                for t in range(n_t):
                    scr3[t, c1 * quarter:(c1 + 1) * quarter, :] = scr[t, pl.ds(c1, quarter, stride=4), :]
            for c in range(16):
                c1, c2 = c % 4, c // 4
                for t in range(n_t):
                    o_g[0, c, :, t * LANES:(t + 1) * LANES] = (
                        scr3[t, pl.ds(c1 * quarter + c2, tm // 16, stride=4), :].astype(BF16))
    for t in range(2):
        put_nat(NAT_QD + t, qk_norm(seg(9 + t), diff_gain(0, t)))
        gate()
        put_nat(NAT_KD + t, qk_norm(seg(11 + t), diff_gain(1, t)))
        gate()
    put_nat(NAT_G0, qk_norm(seg(0), dil_gain(0, 0)))
    put_nat(NAT_G0 + 1, qk_norm(seg(1), dil_gain(1, 0)))
    put_nat(NAT_G0 + 2, seg(2))
    for t in range(2):
        put_nat(NAT_VD + t, seg(13 + t))


def _proj(x, w, dil_gain, diff_gain, layer):
    tm = PROJ_TM
    tiles_per_seq = SEQ // tm
    row = lambda i: (i, 0)
    perm = lambda i: (i // tiles_per_seq, 0, i % tiles_per_seq, 0)
    out_shape = (
        jax.ShapeDtypeStruct((M_ROWS, NAT_COLS), BF16),
        jax.ShapeDtypeStruct((BATCH, 4, SEQ // 4, DIL_WIDTH), BF16),
        jax.ShapeDtypeStruct((BATCH, 16, SEQ // 16, DIL_WIDTH), BF16),
    )
    out_specs = (
        pl.BlockSpec((tm, NAT_COLS), row),
        pl.BlockSpec((1, 4, tm // 4, DIL_WIDTH), perm),
        pl.BlockSpec((1, 16, tm // 16, DIL_WIDTH), perm),
    )
    return pl.pallas_call(
        _proj_kernel,
        out_shape=out_shape,
        grid=(M_ROWS // tm,),
        in_specs=[
            pl.BlockSpec((tm, D_MODEL), row),
            _resident((D_MODEL, IN_COLS)),
            _resident_layer((2 * N_DIL_GROUPS, SEG), layer),
            _resident_layer((4, SEG), layer),
        ],
        out_specs=out_specs,
        scratch_shapes=[pltpu.VMEM((DIL_WIDTH // LANES, tm, LANES), F32)] * 3,
        compiler_params=pltpu.CompilerParams(
            dimension_semantics=("arbitrary",), vmem_limit_bytes=VMEM_LIMIT_BYTES),
        name="proj",
    )(x, w, dil_gain, diff_gain)


def _dil_block(qkv, row_cur, row_prev, bias, lane_lo):
    def rows(which, row):
        ref, col = qkv[which]
        return ref[0, pl.ds(row, BLOCK), col:col + SEG]

    q, k, v = rows(0, row_cur), rows(1, row_cur), rows(2, row_cur)
    if row_prev is not None:
        k = jnp.concatenate([rows(1, row_prev), k], axis=0)
        v = jnp.concatenate([rows(2, row_prev), v], axis=0)
    zero = jnp.zeros((BLOCK, LANES), BF16)
    copies = []
    for j in range(HEADS_PER_GROUP):
        qt = q[:, (j // 2) * LANES:(j // 2 + 1) * LANES]
        kept = jnp.where(lane_lo, qt, zero) if j % 2 == 0 else jnp.where(lane_lo, zero, qt)
        copies.append(jnp.concatenate([kept, zero] if j < 2 else [zero, kept], axis=1))
    qs = jnp.concatenate(copies, axis=0)
    s = lax.dot_general(qs, k, (((1,), (1,)), ((), ())), preferred_element_type=F32) + bias
    m = jnp.max(s, axis=-1, keepdims=True)
    p = jnp.exp2(s - m)
    l = jnp.sum(p, axis=-1, keepdims=True)
    pv = jnp.dot(p.astype(BF16), v, preferred_element_type=F32)
    stats = []
    for u in range(SEG // LANES):
        ra, rb = slice(2 * u * BLOCK, (2 * u + 1) * BLOCK), slice((2 * u + 1) * BLOCK, (2 * u + 2) * BLOCK)
        cols = slice(u * LANES, (u + 1) * LANES)
        stats.append((jnp.where(lane_lo, pv[ra, cols], pv[rb, cols]),
                      jnp.where(lane_lo, m[ra], m[rb]), jnp.where(lane_lo, l[ra], l[rb])))
    return stats


def _dil_kernel(q0_ref, k0_ref, v0_ref, g1_ref, g2_ref, bias_ref, o_ref, o_scr, m_scr, l_scr):
    step = pl.program_id(1)
    lane_lo = lax.broadcasted_iota(jnp.int32, (1, LANES), 1) < HEAD_DIM
    g0_qkv = ((q0_ref, 0), (k0_ref, 0), (v0_ref, 0))
    g1_qkv, g2_qkv = (tuple((ref, kind * SEG) for kind in range(3)) for ref in (g1_ref, g2_ref))

    def put(g, rows, stats):
        for u, (o, m, l) in enumerate(stats):
            o_scr[g, u, rows, :] = o
            m_scr[g, u, rows, :] = m
            l_scr[g, u, rows, :] = l

    for jj in range(DIL_JS):
        j = step * DIL_JS + jj
        row = pl.multiple_of(j * BLOCK, BLOCK)

        prev = pl.multiple_of(jnp.maximum(j - 1, 0) * BLOCK, BLOCK)
        tab = jnp.where(j == 0, N_DIL_GROUPS, 0) if jj == 0 else 0
        put(0, pl.ds(row, BLOCK), _dil_block(g0_qkv, row, prev, bias_ref[tab], lane_lo))

        c, n = step * (DIL_JS // 4) + jj // 4, jj % 4
        if n == 0:
            stats = _dil_block(g1_qkv, row, None, bias_ref[1, :, BLOCK:2 * BLOCK], lane_lo)
        else:
            prev = pl.multiple_of(row - BLOCK, BLOCK)
            stats = _dil_block(g1_qkv, row, prev, bias_ref[1], lane_lo)
        put(1, pl.ds(n * (4 * BLOCK) + c, BLOCK, stride=4), stats)

        put(2, pl.ds(j, BLOCK, stride=16),
            _dil_block(g2_qkv, row, None, bias_ref[2, :, BLOCK:2 * BLOCK], lane_lo))

    @pl.when(step == pl.num_programs(1) - 1)
    def _():
        rows = 256
        for r in range(SEQ // rows):
            sl = pl.ds(r * rows, rows)
            for u in range(SEG // LANES):
                m0, m1, m2 = m_scr[0, u, sl, :], m_scr[1, u, sl, :], m_scr[2, u, sl, :]
                mx = jnp.maximum(jnp.maximum(m0, m1), m2)
                w0, w1, w2 = jnp.exp2(m0 - mx), jnp.exp2(m1 - mx), jnp.exp2(m2 - mx)
                num = w0 * o_scr[0, u, sl, :] + w1 * o_scr[1, u, sl, :] + w2 * o_scr[2, u, sl, :]
                den = w0 * l_scr[0, u, sl, :] + w1 * l_scr[1, u, sl, :] + w2 * l_scr[2, u, sl, :]
                o_ref[0, sl, u * LANES:(u + 1) * LANES] = (num / den).astype(BF16)


def _dil_bias_table():
    qi = np.arange(BLOCK)[:, None]
    ki = np.arange(2 * BLOCK)[None, :]
    dist = BLOCK + qi - ki
    valid = (dist >= 0) & (dist <= BLOCK)
    tab = np.full((N_DIL_GROUPS + 1, HEADS_PER_GROUP * BLOCK, 2 * BLOCK), NEG, np.float32)
    for g, (_, dil) in enumerate(DIL_CONFIGS):
        for h in range(HEADS_PER_GROUP):
            slope = np.float32(SLOPES[g * HEADS_PER_GROUP + h])
            b = np.where(valid, -slope * np.float32(LOG2E) * (dil * dist).astype(np.float32),
                         np.float32(NEG)).astype(np.float32)
            tab[g, h * BLOCK:(h + 1) * BLOCK] = b
            if g == 0:
                tab[N_DIL_GROUPS, h * BLOCK:(h + 1) * BLOCK] = np.where(ki >= BLOCK, b, np.float32(NEG))
    return tab


def _nat_block_spec(col_block_fn):
    return pl.BlockSpec((1, SEQ, SEG), lambda b, *rest: (b, 0, col_block_fn(*rest)))


def _dil_attn(nat, g1, g2, bias):
    n_blk = SEQ // BLOCK
    seq_spec = pl.BlockSpec((1, SEQ, DIL_WIDTH), lambda b, j: (b, 0, 0))
    g0_specs = [_nat_block_spec(lambda j, kind=kind: NAT_G0 + kind) for kind in range(3)]
    return pl.pallas_call(
        _dil_kernel,
        out_shape=jax.ShapeDtypeStruct((BATCH, SEQ, DIL_OUT), BF16),
        grid=(BATCH, n_blk // DIL_JS),
        in_specs=g0_specs + [seq_spec, seq_spec,
                             pl.BlockSpec(bias.shape, lambda b, j: (0, 0, 0))],
        out_specs=pl.BlockSpec((1, SEQ, DIL_OUT), lambda b, j: (b, 0, 0)),
        scratch_shapes=[pltpu.VMEM((N_DIL_GROUPS, DIL_OUT // LANES, SEQ, LANES), F32)] * 3,
        compiler_params=pltpu.CompilerParams(
            dimension_semantics=("arbitrary", "arbitrary"), vmem_limit_bytes=VMEM_LIMIT_BYTES),
        name="dil_attn",
    )(nat, nat, nat, g1, g2, bias)


def _diff_kernel(q_ref, k_ref, v_ref, lq_ref, lk_ref, sn_ref, o_ref,
                 s_scr, mp_scr, m_scr, acc_scr, *, lam_init):
    t = DIFF_T
    n_lt = t // LANES
    n_q = SEQ // t
    pair = pl.program_id(1)
    qp = pl.program_id(2)
    lane_map = lax.broadcasted_iota(jnp.int32, (1, SEG), 1) // HEAD_DIM

    def stacked_q(qi):
        q = q_ref[0, qi * t:(qi + 1) * t, :]
        zero = jnp.zeros_like(q)
        return [jnp.where(lane_map == g, q, zero) for g in range(4)]
    slopes = (jnp.where(pair == 0, SLOPES[N_DIL_HEADS] * LOG2E, SLOPES[N_DIL_HEADS + 2] * LOG2E).astype(F32),
              jnp.where(pair == 0, SLOPES[N_DIL_HEADS + 1] * LOG2E, SLOPES[N_DIL_HEADS + 3] * LOG2E).astype(F32))
    lane_k = lax.broadcasted_iota(jnp.int32, (1, t), 1)
    lane_v0 = lax.broadcasted_iota(jnp.int32, (1, SEG), 1) < DIFF_V_DIM

    def lane_tiles(x):
        return [x[:, u * LANES:(u + 1) * LANES] for u in range(x.shape[1] // LANES)]

    half = t // 2
    above = (lax.broadcasted_iota(jnp.int32, (half, half), 1)
             > lax.broadcasted_iota(jnp.int32, (half, half), 0))
    nt = (((1,), (1,)), ((), ()))

    def pass1(qs, kb, diag, init):
        k = k_ref[0, kb * t:(kb + 1) * t, :]
        bias = [sl * (kb * t + lane_k).astype(F32) for sl in slopes]
        for g in range(4):
            rows = pl.ds(g * t, t)
            if diag:
                sa = lax.dot_general(qs[g][:half], k[:half], nt, preferred_element_type=F32)
                sa = jnp.where(above, NEG, sa + bias[g // 2][:, :half])
                sb = lax.dot_general(qs[g][half:], k, nt, preferred_element_type=F32) + bias[g // 2]
                sb = jnp.concatenate([sb[:, :half], jnp.where(above, NEG, sb[:, half:])], axis=1)
                s_scr[kb, pl.ds(g * t, half), 0:half] = sa
                s_scr[kb, pl.ds(g * t + half, half), :] = sb
                ta, tb = lane_tiles(sa), lane_tiles(sb)
                tmax = jnp.concatenate([jnp.maximum(ta[0], ta[1]),
                                        jnp.maximum(jnp.maximum(tb[0], tb[1]), jnp.maximum(tb[2], tb[3]))], axis=0)
            else:
                s = lax.dot_general(qs[g], k, nt, preferred_element_type=F32) + bias[g // 2]
                s_scr[kb, rows, :] = s
                tl = lane_tiles(s)
                tmax = jnp.maximum(jnp.maximum(tl[0], tl[1]), jnp.maximum(tl[2], tl[3]))
            if not init:
                tmax = jnp.maximum(mp_scr[rows, :], tmax)
            if diag:
                m_scr[rows, :] = jnp.broadcast_to(jnp.max(tmax, axis=-1, keepdims=True), (t, LANES))
            else:
                mp_scr[rows, :] = tmax

    lq = lq_ref[...]
    lk = lk_ref[...]
    lam = (jnp.exp(jnp.sum(lq[0:1] * lk[0:1], axis=-1, keepdims=True))
           - jnp.exp(jnp.sum(lq[1:2] * lk[1:2], axis=-1, keepdims=True)) + lam_init)

    def pass2(qi, kb, init, last):
        v = v_ref[0, kb * t:(kb + 1) * t, :]
        one = jnp.ones_like(v)
        vh = (jnp.where(lane_v0, v, one), jnp.where(lane_v0, one, v))
        done = []
        for g in range(4):
            rows = pl.ds(g * t, t)
            m = m_scr[rows, :]
            if last:
                pa = jnp.exp2(s_scr[kb, pl.ds(g * t, half), 0:half]
                              - jnp.concatenate([m[:half]] * (half // LANES), axis=1))
                pb = jnp.exp2(s_scr[kb, pl.ds(g * t + half, half), :] - jnp.concatenate([m[half:]] * n_lt, axis=1))
                pv = jnp.concatenate(
                    [jnp.dot(pa.astype(BF16), vh[g // 2][:half], preferred_element_type=F32),
                     jnp.dot(pb.astype(BF16), vh[g // 2], preferred_element_type=F32)], axis=0)
            else:
                p = jnp.exp2(s_scr[kb, rows, :] - jnp.concatenate([m] * n_lt, axis=1))
                pv = jnp.dot(p.astype(BF16), vh[g // 2], preferred_element_type=F32)
            acc = pv if init else acc_scr[rows, :] + pv
            if not last:
                acc_scr[rows, :] = acc
                continue
            done.append(acc)
            if g % 2 == 1:
                hh = g // 2
                cols = slice(hh * DIFF_V_DIM, (hh + 1) * DIFF_V_DIM)
                sums = slice((1 - hh) * DIFF_V_DIM, (2 - hh) * DIFF_V_DIM)
                a1, a2 = done[g - 1], done[g]
                o = a1[:, cols] / a1[:, sums] - lam * (a2[:, cols] / a2[:, sums])
                o_ref[0, qi * t:(qi + 1) * t, cols] = (
                    _rms_rows(o, sn_ref[0, hh:hh + 1, :]) * (1.0 - lam_init)).astype(BF16)

    for c, group in enumerate(DIFF_Q_GROUPS):
        @pl.when(qp == c)
        def _(group=group):
            for qi in group:
                qs = stacked_q(qi)
                for kb in range(qi):
                    pass1(qs, kb, False, kb == 0)
                pass1(qs, qi, True, qi == 0)
                for kb in range(qi):
                    pass2(qi, kb, kb == 0, False)
                pass2(qi, qi, qi == 0, True)


def _diff_attn(nat, lq, lk, subnorm, lam_init, layer):
    t = DIFF_T
    assert SEQ // t == 4
    seq_spec = pl.BlockSpec((1, SEQ, SEG), lambda b, p, i: (b, 0, p))
    return pl.pallas_call(
        functools.partial(_diff_kernel, lam_init=lam_init),
        out_shape=jax.ShapeDtypeStruct((BATCH, SEQ, DIFF_V_WIDTH), BF16),
        grid=(BATCH, 2, len(DIFF_Q_GROUPS)),
        in_specs=[
            _nat_block_spec(lambda p, i: NAT_QD + p), _nat_block_spec(lambda p, i: NAT_KD + p),
            _nat_block_spec(lambda p, i: NAT_VD + p),
            pl.BlockSpec((None, 2, HEAD_DIM), lambda b, p, i: (layer, 0, 0)),
            pl.BlockSpec((None, 2, HEAD_DIM), lambda b, p, i: (layer, 0, 0)),
            pl.BlockSpec((None, 1, 2, DIFF_V_DIM), lambda b, p, i: (layer, p, 0, 0)),
        ],
        out_specs=seq_spec,
        scratch_shapes=[pltpu.VMEM((SEQ // t, 4 * t, t), F32),
                        pltpu.VMEM((4 * t, LANES), F32),
                        pltpu.VMEM((4 * t, LANES), F32),
                        pltpu.VMEM((4 * t, SEG), F32)],
        compiler_params=pltpu.CompilerParams(
            dimension_semantics=("arbitrary", "arbitrary", "arbitrary"), vmem_limit_bytes=VMEM_LIMIT_BYTES),
        name="diff_attn",
    )(nat, nat, nat, lq, lk, subnorm)


def kernel(x, ffn1_norm, ffn1_w_in, ffn1_w_out, mix_norm, w_in, qk_gain_dil, qk_gain_diff, lambda_q, lambda_k,
           diff_subnorm, w_branch_dil, w_branch_diff, w_out, ffn2_norm, ffn2_w_in, ffn2_w_out):
    b, s, d = x.shape
    assert (b, s, d) == (BATCH, SEQ, D_MODEL)
    xr = x.reshape(M_ROWS, D_MODEL)
    dil_bias = jnp.asarray(_dil_bias_table())
    wa, wb, wo = w_branch_dil.astype(BF16), w_branch_diff.astype(BF16), w_out.astype(BF16)
    lane_gains = _lane_gains(ffn1_norm, mix_norm, ffn2_norm)
    dil_gain = qk_gain_dil.reshape(DEPTH, 2 * N_DIL_GROUPS, SEG)
    diff_gain = qk_gain_diff.reshape(DEPTH, 4, SEG)
    subnorm = diff_subnorm.reshape(DEPTH, 2, 2, DIFF_V_DIM)
    for l in range(DEPTH):
        lam_init = 0.8 - 0.6 * math.exp(-0.3 * l)
        xr, proj_w = _ffn(xr, lane_gains, ffn1_w_in, ffn1_w_out, w_in, l)
        nat, g1, g2 = _proj(xr, proj_w, dil_gain, diff_gain, l)
        nat_seq = nat.reshape(BATCH, SEQ, NAT_COLS)
        o_dil = _dil_attn(nat_seq, g1.reshape(BATCH, SEQ, DIL_WIDTH), g2.reshape(BATCH, SEQ, DIL_WIDTH), dil_bias)
        o_diff = _diff_attn(nat_seq, lambda_q, lambda_k, subnorm, lam_init, l)
        assert NAT_GATE == 0
        xr = _merge_ffn(xr, o_dil.reshape(M_ROWS, DIL_OUT), o_diff.reshape(M_ROWS, DIFF_V_WIDTH), nat,
                        wa, wb, wo, lane_gains, ffn2_w_in, ffn2_w_out, l)
    return xr.reshape(BATCH, SEQ, D_MODEL)
```

```python
import functools
import math

import jax
import jax.numpy as jnp
import numpy as np
from jax import lax
from jax.experimental import pallas as pl
from jax.experimental.pallas import tpu as pltpu

F32 = jnp.float32
BF16 = jnp.bfloat16

D_MODEL = 1024
BATCH = 8
SEQ = 2048
DEPTH = 2
HEAD_DIM = 64
DIL_CONFIGS = ((128, 1), (512, 4), (2048, 16))
N_DIL_GROUPS = 3
HEADS_PER_GROUP = 4
N_DIL_HEADS = 12
N_DIFF_HEADS = 4
DIFF_V_DIM = 128
N_ALIBI_HEADS = 16
D_FF = 2816
BLOCK = 128
EPS = 1e-6
DIL_WIDTH = 768
DIL_OUT = 256
DIFF_QK_WIDTH = 512
DIFF_V_WIDTH = 512
IN_COLS = 5888
NEG = -1e30

LANES = 128
VMEM_LIMIT_BYTES = 56 * 1024 * 1024

M_ROWS = BATCH * SEQ
SEG = 256
N_SEG = IN_COLS // SEG
FFN_TM = 512
FFN_FC = 256
PROJ_TM = 512
NAT_GATE, NAT_G0, NAT_QD, NAT_KD, NAT_VD = 0, 8, 11, 13, 15
NAT_COLS = 17 * SEG
DIL_JS = 8
DIFF_T = 512
DIFF_Q_GROUPS = ((0, 3), (1, 2))
LOG2E = math.log2(math.e)

SLOPES = tuple(float(np.float32(2.0) ** np.float32(-8.0 * i / N_ALIBI_HEADS)) for i in range(1, N_ALIBI_HEADS + 1))


def _rms_rows(x, gain):
    ms = jnp.mean(x * x, axis=-1, keepdims=True)
    return x * lax.rsqrt(ms + EPS) * gain


def _row_rsqrt(x):
    return lax.rsqrt(jnp.mean(x * x, axis=-1, keepdims=True) + EPS)


class _FfnWeightStager:
    def __init__(self, layer, gain_ref, win_hbm, wout_hbm, win_bf, wout_bf, stage_in, stage_out, sems):
        self.layer, self.gain_ref = layer, gain_ref
        self.win_hbm, self.wout_hbm, self.win_bf, self.wout_bf = win_hbm, wout_hbm, win_bf, wout_bf
        self.stage_in, self.stage_out, self.sems = stage_in, stage_out, sems

    def _copies(self, c):
        slot, lo = c % 2, c * FFN_FC
        return (
            pltpu.make_async_copy(self.win_hbm.at[self.layer, :, pl.ds(lo, FFN_FC)],
                                  self.stage_in.at[slot, 0], self.sems.at[slot, 0]),
            pltpu.make_async_copy(self.win_hbm.at[self.layer, :, pl.ds(D_FF + lo, FFN_FC)],
                                  self.stage_in.at[slot, 1], self.sems.at[slot, 1]),
            pltpu.make_async_copy(self.wout_hbm.at[self.layer, pl.ds(lo, FFN_FC), :],
                                  self.stage_out.at[slot], self.sems.at[slot, 2]),
        )

    def start(self, c):
        for cp in self._copies(c):
            cp.start()

    def finish(self, c):
        slot, lo = c % 2, c * FFN_FC
        for cp in self._copies(c):
            cp.wait()
        gain = jnp.concatenate([self.gain_ref[...]] * (FFN_FC // LANES), axis=1)
        self.win_bf[:, lo:lo + FFN_FC] = (self.stage_in[slot, 0] * gain).astype(BF16)
        self.win_bf[:, D_FF + lo:D_FF + lo + FFN_FC] = (self.stage_in[slot, 1] * gain).astype(BF16)
        self.wout_bf[lo:lo + FFN_FC, :] = self.stage_out[slot].astype(BF16)


def _ffn_rows(x, win_ref, wout_ref, a_scr, stager=None):
    h = x.astype(BF16)
    r = _row_rsqrt(x)
    n_chunks = D_FF // FFN_FC
    if stager is not None:
        stager.start(0)
    for c in range(n_chunks):
        lo = c * FFN_FC
        if stager is not None:
            if c + 1 < n_chunks:
                stager.start(c + 1)
            stager.finish(c)
        gate = jnp.dot(h, win_ref[:, lo:lo + FFN_FC], preferred_element_type=F32) * r
        up = jnp.dot(h, win_ref[:, D_FF + lo:D_FF + lo + FFN_FC], preferred_element_type=F32) * r
        a_scr[:, lo:lo + FFN_FC] = (gate * jax.nn.sigmoid(gate) * up).astype(BF16)
    y = jnp.dot(a_scr[...], wout_ref[...], preferred_element_type=F32)
    return x + 0.5 * y


def _ffn_tile(x_fn, o_ref, a_scr, stager):
    first = pl.program_id(0) == 0

    @pl.when(first)
    def _():
        o_ref[...] = _ffn_rows(x_fn(), stager.win_bf, stager.wout_bf, a_scr, stager)

    @pl.when(jnp.logical_not(first))
    def _():
        o_ref[...] = _ffn_rows(x_fn(), stager.win_bf, stager.wout_bf, a_scr)


def _proj_src_col(i):
    if i < 3 * N_DIL_GROUPS:
        return (i % 3) * DIL_WIDTH + (i // 3) * SEG
    return i * SEG


def _ffn_kernel(x_ref, gain_ref, win_hbm, wout_hbm, pw_ref, pgain_ref, o_ref, pw_bf_ref, a_scr, *stage_refs, layer):
    pgain = jnp.concatenate([pgain_ref[...]] * (SEG // LANES), axis=1)
    for j in range(N_SEG):
        src = _proj_src_col(j)
        pw_bf_ref[:, j * SEG:(j + 1) * SEG] = (pw_ref[:, src:src + SEG] * pgain).astype(BF16)
    _ffn_tile(lambda: x_ref[...], o_ref, a_scr, _FfnWeightStager(layer, gain_ref, win_hbm, wout_hbm, *stage_refs))


def _merge_ffn_kernel(x_ref, od_ref, of_ref, gate_ref, wa_ref, wb_ref, wo_ref, gain_ref, win_hbm, wout_hbm,
                      o_ref, a_scr, *stage_refs, layer):
    def merged():
        ya = jnp.dot(od_ref[...], wa_ref[...], preferred_element_type=F32)
        yb = jnp.dot(of_ref[...], wb_ref[...], preferred_element_type=F32)
        y = gate_ref[:, 0:D_MODEL].astype(F32) * ya + gate_ref[:, D_MODEL:2 * D_MODEL].astype(F32) * yb
        return x_ref[...] + jnp.dot(y.astype(BF16), wo_ref[...], preferred_element_type=F32)

    _ffn_tile(merged, o_ref, a_scr, _FfnWeightStager(layer, gain_ref, win_hbm, wout_hbm, *stage_refs))


def _resident(shape):
    return pl.BlockSpec(shape, lambda i: (0,) * len(shape), pipeline_mode=pl.Buffered(1))


def _resident_layer(shape, layer):
    return pl.BlockSpec((None,) + shape, lambda i: (layer,) + (0,) * len(shape), pipeline_mode=pl.Buffered(1))


NORM_FFN1, NORM_MIX, NORM_FFN2 = 0, 1, 2


def _lane_gains(ffn1_norm, mix_norm, ffn2_norm):
    stacked = jnp.stack([ffn1_norm, mix_norm, ffn2_norm])
    return jnp.broadcast_to(stacked[..., None], stacked.shape + (LANES,))


def _ffn_weight_specs(kind, layer):
    gain = pl.BlockSpec((None, None, D_MODEL, LANES), lambda i: (kind, layer, 0, 0), pipeline_mode=pl.Buffered(1))
    return [gain, pl.BlockSpec(memory_space=pl.ANY), pl.BlockSpec(memory_space=pl.ANY)]


def _ffn_scratch(tm):
    return [pltpu.VMEM((tm, D_FF), BF16),
            pltpu.VMEM((D_MODEL, 2 * D_FF), BF16), pltpu.VMEM((D_FF, D_MODEL), BF16),
            pltpu.VMEM((2, 2, D_MODEL, FFN_FC), F32), pltpu.VMEM((2, FFN_FC, D_MODEL), F32),
            pltpu.SemaphoreType.DMA((2, 3))]


def _ffn(x, lane_gains, w_in, w_out, proj_w, layer):
    tm = FFN_TM
    n_steps = M_ROWS // tm
    pw_rows = D_MODEL // n_steps
    row = lambda i: (i, 0)
    return pl.pallas_call(
        functools.partial(_ffn_kernel, layer=layer),
        out_shape=(jax.ShapeDtypeStruct((M_ROWS, D_MODEL), F32), jax.ShapeDtypeStruct((D_MODEL, IN_COLS), BF16)),
        grid=(n_steps,),
        in_specs=[pl.BlockSpec((tm, D_MODEL), row)] + _ffn_weight_specs(NORM_FFN1, layer) + [
            pl.BlockSpec((None, pw_rows, IN_COLS), lambda i: (layer, i, 0)),
            pl.BlockSpec((None, None, pw_rows, LANES), lambda i: (NORM_MIX, layer, i, 0))],
        out_specs=(pl.BlockSpec((tm, D_MODEL), row), pl.BlockSpec((pw_rows, IN_COLS), row)),
        scratch_shapes=_ffn_scratch(tm),
        compiler_params=pltpu.CompilerParams(
            dimension_semantics=("arbitrary",), vmem_limit_bytes=VMEM_LIMIT_BYTES),
        name="ffn",
    )(x, lane_gains, w_in, w_out, proj_w, lane_gains)


def _merge_ffn(x, o_dil, o_diff, gates, wa, wb, wo, lane_gains, w_in, w_out, layer):
    tm = FFN_TM
    row = lambda i: (i, 0)
    return pl.pallas_call(
        functools.partial(_merge_ffn_kernel, layer=layer),
        out_shape=jax.ShapeDtypeStruct((M_ROWS, D_MODEL), F32),
        grid=(M_ROWS // tm,),
        in_specs=[
            pl.BlockSpec((tm, D_MODEL), row),
            pl.BlockSpec((tm, DIL_OUT), row),
            pl.BlockSpec((tm, DIFF_V_WIDTH), row),
            pl.BlockSpec((tm, 2 * D_MODEL), row),
            _resident((DIL_OUT, D_MODEL)),
            _resident((DIFF_V_WIDTH, D_MODEL)),
            _resident((D_MODEL, D_MODEL)),
        ] + _ffn_weight_specs(NORM_FFN2, layer),
        out_specs=pl.BlockSpec((tm, D_MODEL), row),
        scratch_shapes=_ffn_scratch(tm),
        compiler_params=pltpu.CompilerParams(
            dimension_semantics=("arbitrary",), vmem_limit_bytes=VMEM_LIMIT_BYTES),
        name="merge_ffn",
    )(x, o_dil, o_diff, gates, wa, wb, wo, lane_gains, w_in, w_out)


def _proj_kernel(x_ref, w_ref, dil_gain_ref, diff_gain_ref,
                 o_nat, o_g1, o_g2, scr1, scr2, scr3):
    tm = PROJ_TM
    x = x_ref[...]
    h = x.astype(BF16)
    r = _row_rsqrt(x)
    lane_lo = lax.broadcasted_iota(jnp.int32, (1, LANES), 1) < HEAD_DIM

    def seg(i):
        return jnp.dot(h, w_ref[:, i * SEG:(i + 1) * SEG], preferred_element_type=F32) * r

    kind_scale = (LOG2E, HEAD_DIM ** 0.5)

    def dil_gain(kind, g):
        return dil_gain_ref[3 * kind + g:3 * kind + g + 1, :] * kind_scale[kind]

    def diff_gain(kind, t):
        return diff_gain_ref[2 * kind + t:2 * kind + t + 1, :] * kind_scale[kind]

    def qk_norm(y, gain_row):
        tiles = []
        for u in range(SEG // LANES):
            yt = y[:, u * LANES:(u + 1) * LANES]
            sq = yt * yt
            low = jnp.sum(jnp.where(lane_lo, sq, 0.0), axis=-1, keepdims=True)
            high = jnp.sum(jnp.where(lane_lo, 0.0, sq), axis=-1, keepdims=True)
            ssq = jnp.where(lane_lo, low, high)
            tiles.append(yt * lax.rsqrt(ssq + HEAD_DIM * EPS))
        return jnp.concatenate(tiles, axis=-1) * gain_row

    gate_segments = iter(range(8))

    def put_nat(col_block, y):
        o_nat[:, col_block * SEG:(col_block + 1) * SEG] = y.astype(BF16)

    def gate():
        t = next(gate_segments)
        put_nat(NAT_GATE + t, jax.nn.sigmoid(seg(15 + t)))

    n_t = DIL_WIDTH // LANES
    quarter = tm // 4
    for g, (scr, o_g) in ((1, (scr1, o_g1)), (2, (scr2, o_g2))):
        ys = []
        for kind in range(2):
            ys.append(qk_norm(seg(3 * g + kind), dil_gain(kind, g)))
            gate()
        ys.append(seg(3 * g + 2))
        for t, y in enumerate(ys):
            for u in range(SEG // LANES):
                scr[2 * t + u] = y[:, u * LANES:(u + 1) * LANES]
        if g == 1:
            for c in range(4):
                for t in range(n_t):
                    o_g[0, c, :, t * LANES:(t + 1) * LANES] = scr[t, pl.ds(c, quarter, stride=4), :].astype(BF16)
        else:
            for c1 in range(4):
                for t in range(n_t):
                    scr3[t, c1 * quarter:(c1 + 1) * quarter, :] = scr[t, pl.ds(c1, quarter, stride=4), :]
            for c in range(16):
                c1, c2 = c % 4, c // 4
                for t in range(n_t):
                    o_g[0, c, :, t * LANES:(t + 1) * LANES] = (
                        scr3[t, pl.ds(c1 * quarter + c2, tm // 16, stride=4), :].astype(BF16))
    for t in range(2):
        put_nat(NAT_QD + t, qk_norm(seg(9 + t), diff_gain(0, t)))
        gate()
        put_nat(NAT_KD + t, qk_norm(seg(11 + t), diff_gain(1, t)))
        gate()
    put_nat(NAT_G0, qk_norm(seg(0), dil_gain(0, 0)))
    put_nat(NAT_G0 + 1, qk_norm(seg(1), dil_gain(1, 0)))
    put_nat(NAT_G0 + 2, seg(2))
    for t in range(2):
        put_nat(NAT_VD + t, seg(13 + t))


def _proj(x, w, dil_gain, diff_gain, layer):
    tm = PROJ_TM
    tiles_per_seq = SEQ // tm
    row = lambda i: (i, 0)
    perm = lambda i: (i // tiles_per_seq, 0, i % tiles_per_seq, 0)
    out_shape = (
        jax.ShapeDtypeStruct((M_ROWS, NAT_COLS), BF16),
        jax.ShapeDtypeStruct((BATCH, 4, SEQ // 4, DIL_WIDTH), BF16),
        jax.ShapeDtypeStruct((BATCH, 16, SEQ // 16, DIL_WIDTH), BF16),
    )
    out_specs = (
        pl.BlockSpec((tm, NAT_COLS), row),
        pl.BlockSpec((1, 4, tm // 4, DIL_WIDTH), perm),
        pl.BlockSpec((1, 16, tm // 16, DIL_WIDTH), perm),
    )
    return pl.pallas_call(
        _proj_kernel,
        out_shape=out_shape,
        grid=(M_ROWS // tm,),
        in_specs=[
            pl.BlockSpec((tm, D_MODEL), row),
            _resident((D_MODEL, IN_COLS)),
            _resident_layer((2 * N_DIL_GROUPS, SEG), layer),
            _resident_layer((4, SEG), layer),
        ],
        out_specs=out_specs,
        scratch_shapes=[pltpu.VMEM((DIL_WIDTH // LANES, tm, LANES), F32)] * 3,
        compiler_params=pltpu.CompilerParams(
            dimension_semantics=("arbitrary",), vmem_limit_bytes=VMEM_LIMIT_BYTES),
        name="proj",
    )(x, w, dil_gain, diff_gain)


def _dil_block(qkv, row_cur, row_prev, bias, lane_lo):
    def rows(which, row):
        ref, col = qkv[which]
        return ref[0, pl.ds(row, BLOCK), col:col + SEG]

    q, k, v = rows(0, row_cur), rows(1, row_cur), rows(2, row_cur)
    if row_prev is not None:
        k = jnp.concatenate([rows(1, row_prev), k], axis=0)
        v = jnp.concatenate([rows(2, row_prev), v], axis=0)
    zero = jnp.zeros((BLOCK, LANES), BF16)
    copies = []
    for j in range(HEADS_PER_GROUP):
        qt = q[:, (j // 2) * LANES:(j // 2 + 1) * LANES]
        kept = jnp.where(lane_lo, qt, zero) if j % 2 == 0 else jnp.where(lane_lo, zero, qt)
        copies.append(jnp.concatenate([kept, zero] if j < 2 else [zero, kept], axis=1))
    qs = jnp.concatenate(copies, axis=0)
    s = lax.dot_general(qs, k, (((1,), (1,)), ((), ())), preferred_element_type=F32) + bias
    m = jnp.max(s, axis=-1, keepdims=True)
    p = jnp.exp2(s - m)
    l = jnp.sum(p, axis=-1, keepdims=True)
    pv = jnp.dot(p.astype(BF16), v, preferred_element_type=F32)
    stats = []
    for u in range(SEG // LANES):
        ra, rb = slice(2 * u * BLOCK, (2 * u + 1) * BLOCK), slice((2 * u + 1) * BLOCK, (2 * u + 2) * BLOCK)
        cols = slice(u * LANES, (u + 1) * LANES)
        stats.append((jnp.where(lane_lo, pv[ra, cols], pv[rb, cols]),
                      jnp.where(lane_lo, m[ra], m[rb]), jnp.where(lane_lo, l[ra], l[rb])))
    return stats


def _dil_kernel(q0_ref, k0_ref, v0_ref, g1_ref, g2_ref, bias_ref, wa_ref, wb_ref, wo_ref,
                o_ref, wa_bf_ref, wb_bf_ref, wo_bf_ref, o_scr, m_scr, l_scr):
    for src, dst in ((wa_ref, wa_bf_ref), (wb_ref, wb_bf_ref), (wo_ref, wo_bf_ref)):
        dst[...] = src[...].astype(BF16)

    step = pl.program_id(1)
    lane_lo = lax.broadcasted_iota(jnp.int32, (1, LANES), 1) < HEAD_DIM
    g0_qkv = ((q0_ref, 0), (k0_ref, 0), (v0_ref, 0))
    g1_qkv, g2_qkv = (tuple((ref, kind * SEG) for kind in range(3)) for ref in (g1_ref, g2_ref))

    def put(g, rows, stats):
        for u, (o, m, l) in enumerate(stats):
            o_scr[g, u, rows, :] = o
            m_scr[g, u, rows, :] = m
            l_scr[g, u, rows, :] = l

    for jj in range(DIL_JS):
        j = step * DIL_JS + jj
        row = pl.multiple_of(j * BLOCK, BLOCK)

        prev = pl.multiple_of(jnp.maximum(j - 1, 0) * BLOCK, BLOCK)
        tab = jnp.where(j == 0, N_DIL_GROUPS, 0) if jj == 0 else 0
        put(0, pl.ds(row, BLOCK), _dil_block(g0_qkv, row, prev, bias_ref[tab], lane_lo))

        c, n = step * (DIL_JS // 4) + jj // 4, jj % 4
        if n == 0:
            stats = _dil_block(g1_qkv, row, None, bias_ref[1, :, BLOCK:2 * BLOCK], lane_lo)
        else:
            prev = pl.multiple_of(row - BLOCK, BLOCK)
            stats = _dil_block(g1_qkv, row, prev, bias_ref[1], lane_lo)
        put(1, pl.ds(n * (4 * BLOCK) + c, BLOCK, stride=4), stats)

        put(2, pl.ds(j, BLOCK, stride=16),
            _dil_block(g2_qkv, row, None, bias_ref[2, :, BLOCK:2 * BLOCK], lane_lo))

    @pl.when(step == pl.num_programs(1) - 1)
    def _():
        rows = 256
        for r in range(SEQ // rows):
            sl = pl.ds(r * rows, rows)
            for u in range(SEG // LANES):
                m0, m1, m2 = m_scr[0, u, sl, :], m_scr[1, u, sl, :], m_scr[2, u, sl, :]
                mx = jnp.maximum(jnp.maximum(m0, m1), m2)
                w0, w1, w2 = jnp.exp2(m0 - mx), jnp.exp2(m1 - mx), jnp.exp2(m2 - mx)
                num = w0 * o_scr[0, u, sl, :] + w1 * o_scr[1, u, sl, :] + w2 * o_scr[2, u, sl, :]
                den = w0 * l_scr[0, u, sl, :] + w1 * l_scr[1, u, sl, :] + w2 * l_scr[2, u, sl, :]
                o_ref[0, sl, u * LANES:(u + 1) * LANES] = (num / den).astype(BF16)


def _dil_bias_table():
    qi = np.arange(BLOCK)[:, None]
    ki = np.arange(2 * BLOCK)[None, :]
    dist = BLOCK + qi - ki
    valid = (dist >= 0) & (dist <= BLOCK)
    tab = np.full((N_DIL_GROUPS + 1, HEADS_PER_GROUP * BLOCK, 2 * BLOCK), NEG, np.float32)
    for g, (_, dil) in enumerate(DIL_CONFIGS):
        for h in range(HEADS_PER_GROUP):
            slope = np.float32(SLOPES[g * HEADS_PER_GROUP + h])
            b = np.where(valid, -slope * np.float32(LOG2E) * (dil * dist).astype(np.float32),
                         np.float32(NEG)).astype(np.float32)
            tab[g, h * BLOCK:(h + 1) * BLOCK] = b
            if g == 0:
                tab[N_DIL_GROUPS, h * BLOCK:(h + 1) * BLOCK] = np.where(ki >= BLOCK, b, np.float32(NEG))
    return tab


def _nat_block_spec(col_block_fn):
    return pl.BlockSpec((1, SEQ, SEG), lambda b, *rest: (b, 0, col_block_fn(*rest)))


def _dil_attn(nat, g1, g2, bias, merge_weights, layer):
    n_blk = SEQ // BLOCK
    steps_per_seq = n_blk // DIL_JS
    n_steps = BATCH * steps_per_seq
    seq_spec = pl.BlockSpec((1, SEQ, DIL_WIDTH), lambda b, j: (b, 0, 0))
    g0_specs = [_nat_block_spec(lambda j, kind=kind: NAT_G0 + kind) for kind in range(3)]
    w_rows = [w.shape[1] // n_steps for w in merge_weights]
    w_in_specs = [pl.BlockSpec((None, r, D_MODEL), lambda b, j: (layer, b * steps_per_seq + j, 0)) for r in w_rows]
    w_out_specs = [pl.BlockSpec((r, D_MODEL), lambda b, j: (b * steps_per_seq + j, 0)) for r in w_rows]
    return pl.pallas_call(
        _dil_kernel,
        out_shape=[jax.ShapeDtypeStruct((BATCH, SEQ, DIL_OUT), BF16)]
        + [jax.ShapeDtypeStruct(w.shape[1:], BF16) for w in merge_weights],
        grid=(BATCH, steps_per_seq),
        in_specs=g0_specs + [seq_spec, seq_spec,
                             pl.BlockSpec(bias.shape, lambda b, j: (0, 0, 0))] + w_in_specs,
        out_specs=[pl.BlockSpec((1, SEQ, DIL_OUT), lambda b, j: (b, 0, 0))] + w_out_specs,
        scratch_shapes=[pltpu.VMEM((N_DIL_GROUPS, DIL_OUT // LANES, SEQ, LANES), F32)] * 3,
        compiler_params=pltpu.CompilerParams(
            dimension_semantics=("arbitrary", "arbitrary"), vmem_limit_bytes=VMEM_LIMIT_BYTES),
        name="dil_attn",
    )(nat, nat, nat, g1, g2, bias, *merge_weights)


def _diff_kernel(q_ref, k_ref, v_ref, lq_ref, lk_ref, sn_ref, o_ref,
                 s_scr, mp_scr, m_scr, acc_scr, *, lam_init):
    t = DIFF_T
    n_lt = t // LANES
    n_q = SEQ // t
    pair = pl.program_id(1)
    qp = pl.program_id(2)
    lane_map = lax.broadcasted_iota(jnp.int32, (1, SEG), 1) // HEAD_DIM

    def stacked_q(qi):
        q = q_ref[0, qi * t:(qi + 1) * t, :]
        zero = jnp.zeros_like(q)
        return [jnp.where(lane_map == g, q, zero) for g in range(4)]
    slopes = (jnp.where(pair == 0, SLOPES[N_DIL_HEADS] * LOG2E, SLOPES[N_DIL_HEADS + 2] * LOG2E).astype(F32),
              jnp.where(pair == 0, SLOPES[N_DIL_HEADS + 1] * LOG2E, SLOPES[N_DIL_HEADS + 3] * LOG2E).astype(F32))
    lane_k = lax.broadcasted_iota(jnp.int32, (1, t), 1)
    lane_v0 = lax.broadcasted_iota(jnp.int32, (1, SEG), 1) < DIFF_V_DIM

    def lane_tiles(x):
        return [x[:, u * LANES:(u + 1) * LANES] for u in range(x.shape[1] // LANES)]

    half = t // 2
    above = (lax.broadcasted_iota(jnp.int32, (half, half), 1)
             > lax.broadcasted_iota(jnp.int32, (half, half), 0))
    nt = (((1,), (1,)), ((), ()))

    def pass1(qs, kb, diag, init):
        k = k_ref[0, kb * t:(kb + 1) * t, :]
        bias = [sl * (kb * t + lane_k).astype(F32) for sl in slopes]
        for g in range(4):
            rows = pl.ds(g * t, t)
            if diag:
                sa = lax.dot_general(qs[g][:half], k[:half], nt, preferred_element_type=F32)
                sa = jnp.where(above, NEG, sa + bias[g // 2][:, :half])
                sb = lax.dot_general(qs[g][half:], k, nt, preferred_element_type=F32) + bias[g // 2]
                sb = jnp.concatenate([sb[:, :half], jnp.where(above, NEG, sb[:, half:])], axis=1)
                s_scr[kb, pl.ds(g * t, half), 0:half] = sa
                s_scr[kb, pl.ds(g * t + half, half), :] = sb
                ta, tb = lane_tiles(sa), lane_tiles(sb)
                tmax = jnp.concatenate([jnp.maximum(ta[0], ta[1]),
                                        jnp.maximum(jnp.maximum(tb[0], tb[1]), jnp.maximum(tb[2], tb[3]))], axis=0)
            else:
                s = lax.dot_general(qs[g], k, nt, preferred_element_type=F32) + bias[g // 2]
                s_scr[kb, rows, :] = s
                tl = lane_tiles(s)
                tmax = jnp.maximum(jnp.maximum(tl[0], tl[1]), jnp.maximum(tl[2], tl[3]))
            if not init:
                tmax = jnp.maximum(mp_scr[rows, :], tmax)
            if diag:
                m_scr[rows, :] = jnp.broadcast_to(jnp.max(tmax, axis=-1, keepdims=True), (t, LANES))
            else:
                mp_scr[rows, :] = tmax

    lq = lq_ref[...]
    lk = lk_ref[...]
    lam = (jnp.exp(jnp.sum(lq[0:1] * lk[0:1], axis=-1, keepdims=True))
           - jnp.exp(jnp.sum(lq[1:2] * lk[1:2], axis=-1, keepdims=True)) + lam_init)

    def pass2(qi, kb, init, last):
        v = v_ref[0, kb * t:(kb + 1) * t, :]
        one = jnp.ones_like(v)
        vh = (jnp.where(lane_v0, v, one), jnp.where(lane_v0, one, v))
        done = []
        for g in range(4):
            rows = pl.ds(g * t, t)
            m = m_scr[rows, :]
            if last:
                pa = jnp.exp2(s_scr[kb, pl.ds(g * t, half), 0:half]
                              - jnp.concatenate([m[:half]] * (half // LANES), axis=1))
                pb = jnp.exp2(s_scr[kb, pl.ds(g * t + half, half), :] - jnp.concatenate([m[half:]] * n_lt, axis=1))
                pv = jnp.concatenate(
                    [jnp.dot(pa.astype(BF16), vh[g // 2][:half], preferred_element_type=F32),
                     jnp.dot(pb.astype(BF16), vh[g // 2], preferred_element_type=F32)], axis=0)
            else:
                p = jnp.exp2(s_scr[kb, rows, :] - jnp.concatenate([m] * n_lt, axis=1))
                pv = jnp.dot(p.astype(BF16), vh[g // 2], preferred_element_type=F32)
            acc = pv if init else acc_scr[rows, :] + pv
            if not last:
                acc_scr[rows, :] = acc
                continue
            done.append(acc)
            if g % 2 == 1:
                hh = g // 2
                cols = slice(hh * DIFF_V_DIM, (hh + 1) * DIFF_V_DIM)
                sums = slice((1 - hh) * DIFF_V_DIM, (2 - hh) * DIFF_V_DIM)
                a1, a2 = done[g - 1], done[g]
                o = a1[:, cols] / a1[:, sums] - lam * (a2[:, cols] / a2[:, sums])
                o_ref[0, qi * t:(qi + 1) * t, cols] = (
                    _rms_rows(o, sn_ref[0, hh:hh + 1, :]) * (1.0 - lam_init)).astype(BF16)

    for c, group in enumerate(DIFF_Q_GROUPS):
        @pl.when(qp == c)
        def _(group=group):
            for qi in group:
                qs = stacked_q(qi)
                for kb in range(qi):
                    pass1(qs, kb, False, kb == 0)
                pass1(qs, qi, True, qi == 0)
                for kb in range(qi):
                    pass2(qi, kb, kb == 0, False)
                pass2(qi, qi, qi == 0, True)


def _diff_attn(nat, lq, lk, subnorm, lam_init, layer):
    t = DIFF_T
    assert SEQ // t == 4
    seq_spec = pl.BlockSpec((1, SEQ, SEG), lambda b, p, i: (b, 0, p))
    return pl.pallas_call(
        functools.partial(_diff_kernel, lam_init=lam_init),
        out_shape=jax.ShapeDtypeStruct((BATCH, SEQ, DIFF_V_WIDTH), BF16),
        grid=(BATCH, 2, len(DIFF_Q_GROUPS)),
        in_specs=[
            _nat_block_spec(lambda p, i: NAT_QD + p), _nat_block_spec(lambda p, i: NAT_KD + p),
            _nat_block_spec(lambda p, i: NAT_VD + p),
            pl.BlockSpec((None, 2, HEAD_DIM), lambda b, p, i: (layer, 0, 0)),
            pl.BlockSpec((None, 2, HEAD_DIM), lambda b, p, i: (layer, 0, 0)),
            pl.BlockSpec((None, 1, 2, DIFF_V_DIM), lambda b, p, i: (layer, p, 0, 0)),
        ],
        out_specs=seq_spec,
        scratch_shapes=[pltpu.VMEM((SEQ // t, 4 * t, t), F32),
                        pltpu.VMEM((4 * t, LANES), F32),
                        pltpu.VMEM((4 * t, LANES), F32),
                        pltpu.VMEM((4 * t, SEG), F32)],
        compiler_params=pltpu.CompilerParams(
            dimension_semantics=("arbitrary", "arbitrary", "arbitrary"), vmem_limit_bytes=VMEM_LIMIT_BYTES),
        name="diff_attn",
    )(nat, nat, nat, lq, lk, subnorm)


def kernel(x, ffn1_norm, ffn1_w_in, ffn1_w_out, mix_norm, w_in, qk_gain_dil, qk_gain_diff, lambda_q, lambda_k,
           diff_subnorm, w_branch_dil, w_branch_diff, w_out, ffn2_norm, ffn2_w_in, ffn2_w_out):
    b, s, d = x.shape
    assert (b, s, d) == (BATCH, SEQ, D_MODEL)
    xr = x.reshape(M_ROWS, D_MODEL)
    dil_bias = jnp.asarray(_dil_bias_table())
    lane_gains = _lane_gains(ffn1_norm, mix_norm, ffn2_norm)
    dil_gain = qk_gain_dil.reshape(DEPTH, 2 * N_DIL_GROUPS, SEG)
    diff_gain = qk_gain_diff.reshape(DEPTH, 4, SEG)
    subnorm = diff_subnorm.reshape(DEPTH, 2, 2, DIFF_V_DIM)
    for l in range(DEPTH):
        lam_init = 0.8 - 0.6 * math.exp(-0.3 * l)
        xr, proj_w = _ffn(xr, lane_gains, ffn1_w_in, ffn1_w_out, w_in, l)
        nat, g1, g2 = _proj(xr, proj_w, dil_gain, diff_gain, l)
        nat_seq = nat.reshape(BATCH, SEQ, NAT_COLS)
        o_dil, wa, wb, wo = _dil_attn(nat_seq, g1.reshape(BATCH, SEQ, DIL_WIDTH), g2.reshape(BATCH, SEQ, DIL_WIDTH),
                                      dil_bias, (w_branch_dil, w_branch_diff, w_out), l)
        o_diff = _diff_attn(nat_seq, lambda_q, lambda_k, subnorm, lam_init, l)
        assert NAT_GATE == 0
        xr = _merge_ffn(xr, o_dil.reshape(M_ROWS, DIL_OUT), o_diff.reshape(M_ROWS, DIFF_V_WIDTH), nat,
                        wa, wb, wo, lane_gains, ffn2_w_in, ffn2_w_out, l)
    return xr.reshape(BATCH, SEQ, D_MODEL)
```
